```python
import functools
import jax, jax.numpy as jnp
from jax import lax
import numpy as np

D_MODEL = 1024
BATCH = 4
SEQ = 4096
DEPTH = 1
DEC_BATCH = 128
DEC_SEQ = 8
PAST_LEN = 8192
PAGE_SIZE = 128

N_HEADS = 8
QK_NOPE_DIM = 128
QK_ROPE_DIM = 64
QK_HEAD_DIM = QK_NOPE_DIM + QK_ROPE_DIM
V_HEAD_DIM = 128
Q_LORA_RANK = 256
KV_LORA_RANK = 256
ROPE_BASE = 10000.0
ATTN_WIDTH = N_HEADS * V_HEAD_DIM
ATTN_SCALE = QK_HEAD_DIM ** -0.5
Q_BLOCK = 128
GMLP_GROUPS = 4
GMLP_WIDTH = 1024
GMLP_GROUP_DIM = GMLP_WIDTH // GMLP_GROUPS
CHUNK = 128
D_FF = 2816
EPS = 1e-6
NEG_INF = -1e30
IN_SIZES = (Q_LORA_RANK, KV_LORA_RANK, QK_ROPE_DIM, GMLP_WIDTH, GMLP_WIDTH, D_MODEL, D_MODEL)
IN_SPLIT_POINTS = tuple(int(s) for s in np.cumsum(IN_SIZES)[:-1])
D_IN = sum(IN_SIZES)

kernel_name = "mla_gmlp_macaron_hybrid_step"


def _rmsnorm(x, g):
    x32 = x.astype(jnp.float32)
    y = x32 * lax.rsqrt(jnp.mean(x32 * x32, axis=-1, keepdims=True) + EPS)
    return (y * g.astype(jnp.float32)).astype(x.dtype)


def _rope(x, pos):
    half = QK_ROPE_DIM // 2
    inv = ROPE_BASE ** (-jnp.arange(half, dtype=jnp.float32) / half)
    ang = pos.astype(jnp.float32)[:, None] * inv[None, :]
    ang = ang.reshape(ang.shape[0], *([1] * (x.ndim - 3)), half)
    cos, sin = jnp.cos(ang), jnp.sin(ang)
    x32 = x.astype(jnp.float32)
    x1, x2 = x32[..., :half], x32[..., half:]
    return jnp.concatenate([x1 * cos - x2 * sin, x1 * sin + x2 * cos], axis=-1).astype(x.dtype)


def _swiglu(x, w_gate, w_up, w_down):
    return (jax.nn.silu(x @ w_gate) * (x @ w_up)) @ w_down


def _expand_latent(c_kv, p):
    lead = c_kv.shape[:-1]
    k_nope = _rmsnorm((c_kv @ p["w_uk"]).reshape(*lead, N_HEADS, QK_NOPE_DIM), p["k_nope_norm"])
    vals = (c_kv @ p["w_uv"]).reshape(*lead, N_HEADS, V_HEAD_DIM)
    return k_nope, vals


def _attend(q_nope, q_rope, k_nope, k_rope, vals, q_pos, k_pos):
    s = jnp.einsum('nqhd,nkhd->nhqk', q_nope, k_nope) + jnp.einsum('nqhd,nkd->nhqk', q_rope, k_rope)
    s = s.astype(jnp.float32) * ATTN_SCALE
    s = jnp.where(k_pos[None, :] <= q_pos[:, None], s, NEG_INF)
    prob = jax.nn.softmax(s, axis=-1).astype(vals.dtype)
    return jnp.einsum('nhqk,nkhd->nqhd', prob, vals)


def _prompt_attend(q_nope, q_rope, c_kv, k_rope, p):
    n, t = q_nope.shape[:2]
    k_nope, vals = _expand_latent(c_kv, p)
    k_pos = jnp.arange(t)

    def block(i):
        start = i * Q_BLOCK
        qn = lax.dynamic_slice_in_dim(q_nope, start, Q_BLOCK, axis=1)
        qr = lax.dynamic_slice_in_dim(q_rope, start, Q_BLOCK, axis=1)
        return _attend(qn, qr, k_nope, k_rope, vals, start + jnp.arange(Q_BLOCK), k_pos)

    out = lax.map(block, jnp.arange(t // Q_BLOCK))
    return jnp.moveaxis(out, 0, 1).reshape(n, t, N_HEADS, V_HEAD_DIM)


def _sample_attend(q_nope, q_rope, c_kv, k_rope, p, cache_lat, cache_rope, page_table):
    n_new = q_nope.shape[1]
    past = page_table.shape[1] * PAGE_SIZE
    q_pos = past + jnp.arange(n_new)
    k_pos = jnp.arange(past + n_new)

    def per_seq(args):
        qn, qr, lat_new, kr_new, pt = args
        lat = jnp.concatenate([cache_lat[pt].reshape(past, KV_LORA_RANK), lat_new], axis=0)
        kr = jnp.concatenate([cache_rope[pt].reshape(past, QK_ROPE_DIM), kr_new], axis=0)
        k_nope, vals = _expand_latent(lat[None], p)
        return _attend(qn[None], qr[None], k_nope, kr[None], vals, q_pos, k_pos)[0]

    return lax.map(per_seq, (q_nope, q_rope, c_kv, k_rope, page_table))


def _spatial_gate(u, v, w_s, b_s, chunk_len):
    n, t, _ = v.shape
    vc = v.reshape(n, t // chunk_len, chunk_len, GMLP_GROUPS, GMLP_GROUP_DIM)
    causal = jnp.tril(jnp.ones((chunk_len, chunk_len), dtype=bool))
    w = jnp.where(causal[None], w_s[:, :chunk_len, :chunk_len], 0.0).astype(v.dtype)
    mix = jnp.einsum('gts,ncsgd->nctgd', w, vc) + jnp.transpose(b_s[:, :chunk_len])[None, None, :, :, None]
    return u * mix.reshape(n, t, GMLP_WIDTH)


def _layer(x, pos, attend_fn, chunk_len, p):
    n, t, _ = x.shape
    x = x + 0.5 * _swiglu(_rmsnorm(x, p["ffn1_norm"]), p["ffn1_w_gate"], p["ffn1_w_up"], p["ffn1_w_down"])
    h = _rmsnorm(x, p["mix_norm"])
    c_q, c_kv, k_rope, u, v, g_attn, g_gmlp = jnp.split(h @ p["w_in"], IN_SPLIT_POINTS, axis=-1)
    q = (_rmsnorm(c_q, p["q_lora_norm"]) @ p["w_uq"]).reshape(n, t, N_HEADS, QK_HEAD_DIM)
    q_nope = _rmsnorm(q[..., :QK_NOPE_DIM], p["q_nope_norm"])
    q_rope = _rope(_rmsnorm(q[..., QK_NOPE_DIM:], p["q_rope_norm"]), pos)
    c_kv = _rmsnorm(c_kv, p["kv_lora_norm"])
    k_rope = _rope(_rmsnorm(k_rope, p["k_rope_norm"]), pos)
    attn = attend_fn(q_nope, q_rope, c_kv, k_rope, p).reshape(n, t, ATTN_WIDTH)
    u = jax.nn.gelu(u, approximate=False)
    v = _rmsnorm(jax.nn.gelu(v, approximate=False).reshape(n, t, GMLP_GROUPS, GMLP_GROUP_DIM),
                 p["gmlp_v_norm"]).reshape(n, t, GMLP_WIDTH)
    gm = _spatial_gate(u, v, p["gmlp_w_s"], p["gmlp_b_s"], chunk_len)
    merged = jax.nn.sigmoid(g_attn) * (attn @ p["w_o_attn"]) + jax.nn.sigmoid(g_gmlp) * (gm @ p["w_o_gmlp"])
    x = x + merged @ p["w_out"]
    x = x + 0.5 * _swiglu(_rmsnorm(x, p["ffn2_norm"]), p["ffn2_w_gate"], p["ffn2_w_up"], p["ffn2_w_down"])
    return x, c_kv, k_rope, v


def setup_inputs(seed: int = 0) -> dict:
    key = jax.random.key(seed)
    ks = list(jax.random.split(key, 40))

    def nrm(shape, scale=1.0):
        return scale * jax.random.normal(ks.pop(), shape, jnp.float32)

    def gain(shape):
        return 1.0 + 0.01 * nrm(shape)

    n_pages = PAST_LEN // PAGE_SIZE
    n_used = DEC_BATCH * n_pages
    n_pool = (n_used * 5) // 4
    L = DEPTH
    x_prompt = nrm((BATCH, SEQ, D_MODEL))
    x_sample = nrm((DEC_BATCH, DEC_SEQ, D_MODEL))
    cache_kv_latent = nrm((L, n_pool, PAGE_SIZE, KV_LORA_RANK))
    cache_k_rope = nrm((L, n_pool, PAGE_SIZE, QK_ROPE_DIM))
    page_table = jax.random.permutation(ks.pop(), n_pool)[:n_used].reshape(DEC_BATCH, n_pages).astype(jnp.int32)
    return {
        "x_prompt": x_prompt,
        "x_sample": x_sample,
        "cache_kv_latent": cache_kv_latent,
        "cache_k_rope": cache_k_rope,
        "page_table": page_table,
        "ffn1_norm": gain((L, D_MODEL)),
        "ffn1_w_gate": nrm((L, D_MODEL, D_FF), D_MODEL ** -0.5),
        "ffn1_w_up": nrm((L, D_MODEL, D_FF), D_MODEL ** -0.5),
        "ffn1_w_down": nrm((L, D_FF, D_MODEL), D_FF ** -0.5),
        "mix_norm": gain((L, D_MODEL)),
        "w_in": nrm((L, D_MODEL, D_IN), D_MODEL ** -0.5),
        "q_lora_norm": gain((L, Q_LORA_RANK)),
        "w_uq": nrm((L, Q_LORA_RANK, N_HEADS * QK_HEAD_DIM), Q_LORA_RANK ** -0.5),
        "kv_lora_norm": gain((L, KV_LORA_RANK)),
        "w_uk": nrm((L, KV_LORA_RANK, N_HEADS * QK_NOPE_DIM), KV_LORA_RANK ** -0.5),
        "w_uv": nrm((L, KV_LORA_RANK, N_HEADS * V_HEAD_DIM), KV_LORA_RANK ** -0.5),
        "q_nope_norm": gain((L, QK_NOPE_DIM)),
        "q_rope_norm": gain((L, QK_ROPE_DIM)),
        "k_nope_norm": gain((L, QK_NOPE_DIM)),
        "k_rope_norm": gain((L, QK_ROPE_DIM)),
        "gmlp_v_norm": gain((L, GMLP_GROUPS, GMLP_GROUP_DIM)),
        "gmlp_w_s": nrm((L, GMLP_GROUPS, CHUNK, CHUNK), CHUNK ** -0.5),
        "gmlp_b_s": gain((L, GMLP_GROUPS, CHUNK)),
        "w_o_attn": nrm((L, ATTN_WIDTH, D_MODEL), ATTN_WIDTH ** -0.5),
        "w_o_gmlp": nrm((L, GMLP_WIDTH, D_MODEL), GMLP_WIDTH ** -0.5),
        "w_out": nrm((L, D_MODEL, D_MODEL), D_MODEL ** -0.5),
        "ffn2_norm": gain((L, D_MODEL)),
        "ffn2_w_gate": nrm((L, D_MODEL, D_FF), D_MODEL ** -0.5),
        "ffn2_w_up": nrm((L, D_MODEL, D_FF), D_MODEL ** -0.5),
        "ffn2_w_down": nrm((L, D_FF, D_MODEL), D_FF ** -0.5),
    }


def reference(x_prompt, x_sample, cache_kv_latent, cache_k_rope, page_table,
              ffn1_norm, ffn1_w_gate, ffn1_w_up, ffn1_w_down, mix_norm, w_in,
              q_lora_norm, w_uq, kv_lora_norm, w_uk, w_uv, q_nope_norm, q_rope_norm,
              k_nope_norm, k_rope_norm, gmlp_v_norm, gmlp_w_s, gmlp_b_s,
              w_o_attn, w_o_gmlp, w_out, ffn2_norm, ffn2_w_gate, ffn2_w_up, ffn2_w_down):
    t_prompt = x_prompt.shape[1]
    t_new = x_sample.shape[1]
    past = page_table.shape[1] * PAGE_SIZE
    pos_prompt = jnp.arange(t_prompt)
    pos_sample = past + jnp.arange(t_new)
    xp, xs = x_prompt, x_sample
    lat_p, rope_p, lat_s, rope_s, v_s = [], [], [], [], []
    for l in range(DEPTH):
        p = dict(
            ffn1_norm=ffn1_norm[l], ffn1_w_gate=ffn1_w_gate[l], ffn1_w_up=ffn1_w_up[l], ffn1_w_down=ffn1_w_down[l],
            mix_norm=mix_norm[l], w_in=w_in[l], q_lora_norm=q_lora_norm[l], w_uq=w_uq[l],
            kv_lora_norm=kv_lora_norm[l], w_uk=w_uk[l], w_uv=w_uv[l], q_nope_norm=q_nope_norm[l],
            q_rope_norm=q_rope_norm[l], k_nope_norm=k_nope_norm[l], k_rope_norm=k_rope_norm[l],
            gmlp_v_norm=gmlp_v_norm[l], gmlp_w_s=gmlp_w_s[l], gmlp_b_s=gmlp_b_s[l],
            w_o_attn=w_o_attn[l], w_o_gmlp=w_o_gmlp[l], w_out=w_out[l], ffn2_norm=ffn2_norm[l],
            ffn2_w_gate=ffn2_w_gate[l], ffn2_w_up=ffn2_w_up[l], ffn2_w_down=ffn2_w_down[l])
        xp, c_kv_p, k_rope_p, _ = _layer(xp, pos_prompt, _prompt_attend, CHUNK, p)
        attend_s = functools.partial(_sample_attend, cache_lat=cache_kv_latent[l],
                                     cache_rope=cache_k_rope[l], page_table=page_table)
        xs, c_kv_s, k_rope_s, v_rows_s = _layer(xs, pos_sample, attend_s, t_new, p)
        lat_p.append(c_kv_p)
        rope_p.append(k_rope_p)
        lat_s.append(c_kv_s)
        rope_s.append(k_rope_s)
        v_s.append(v_rows_s)
    new_kv_latent_prompt = jnp.stack(lat_p)
    new_k_rope_prompt = jnp.stack(rope_p)
    new_kv_latent_sample = jnp.stack(lat_s)
    new_k_rope_sample = jnp.stack(rope_s)
    new_gmlp_v_sample = jnp.stack(v_s)
    return (xp, xs, new_kv_latent_prompt, new_k_rope_prompt, new_kv_latent_sample, new_k_rope_sample, new_gmlp_v_sample)
```

```python
import functools

import jax
import jax.numpy as jnp
import numpy as np
from jax import lax
from jax.experimental import pallas as pl
from jax.experimental.pallas import tpu as pltpu

D_MODEL = 1024
N_HEADS = 8
QK_NOPE_DIM = 128
QK_ROPE_DIM = 64
QK_HEAD_DIM = QK_NOPE_DIM + QK_ROPE_DIM
V_HEAD_DIM = 128
Q_LORA_RANK = 256
KV_LORA_RANK = 256
ROPE_BASE = 10000.0
ATTN_SCALE = QK_HEAD_DIM ** -0.5
GMLP_GROUPS = 4
GMLP_WIDTH = 1024
GMLP_GROUP_DIM = GMLP_WIDTH // GMLP_GROUPS
CHUNK = 128
PAGE_SIZE = 128
D_FF = 2816
EPS = 1e-6
NEG_INF = -1e30

LANE = 128
ROPE_SLOT = LANE

_OFF_CQ = 0
_OFF_CKV = _OFF_CQ + Q_LORA_RANK
_OFF_U = _OFF_CKV + KV_LORA_RANK
_OFF_V = _OFF_U + GMLP_WIDTH
_OFF_GA = _OFF_V + GMLP_WIDTH
_OFF_GG = _OFF_GA + D_MODEL
_OFF_KR = _OFF_GG + D_MODEL
_OFF_KRS = _OFF_KR + ROPE_SLOT
_D_IN_PADDED = _OFF_KRS + ROPE_SLOT

FFN_TM = 1024
FFN_TF = 256
PROJ_TM = 512
ATT_TQ = 512
ATT_TK = 512
MERGE_TM = 512
SATT_PAGES_PER_CHUNK = 8
VMEM_LIMIT = 56 * 1024 * 1024

_BF16 = jnp.bfloat16
_F32 = jnp.float32


def _dot(a, b):
    return jnp.dot(a, b, preferred_element_type=_F32)


def _dot_t(a, b):
    return lax.dot_general(a, b, (((1,), (1,)), ((), ())), preferred_element_type=_F32)


def _rms(x, n):
    return lax.rsqrt(jnp.sum(x * x, axis=-1, keepdims=True) * (1.0 / n) + EPS)


def _ffn_kernel(x_ref, g_ref, wg_ref, wu_ref, wd_ref, o_ref, h_scr, acc_scr):
    j = pl.program_id(1)

    @pl.when(j == 0)
    def _():
        x = x_ref[...]
        h_scr[...] = (x * _rms(x, D_MODEL) * g_ref[...]).astype(_BF16)
        acc_scr[...] = jnp.zeros_like(acc_scr)

    h = h_scr[...]
    gate = _dot(h, wg_ref[...])
    up = _dot(h, wu_ref[...])
    act = (gate * jax.nn.sigmoid(gate) * up).astype(_BF16)
    acc_scr[...] += _dot(act, wd_ref[...])

    @pl.when(j == pl.num_programs(1) - 1)
    def _():
        o_ref[...] = x_ref[...] + 0.5 * acc_scr[...]


def _ffn(x, norm_g, wg, wu, wd):
    n = x.shape[0]
    tm = min(FFN_TM, n)
    return pl.pallas_call(
        _ffn_kernel,
        grid=(n // tm, D_FF // FFN_TF),
        in_specs=[
            pl.BlockSpec((tm, D_MODEL), lambda i, j: (i, 0)),
            pl.BlockSpec((1, D_MODEL), lambda i, j: (0, 0)),
            pl.BlockSpec((D_MODEL, FFN_TF), lambda i, j: (0, j)),
            pl.BlockSpec((D_MODEL, FFN_TF), lambda i, j: (0, j)),
            pl.BlockSpec((FFN_TF, D_MODEL), lambda i, j: (j, 0)),
        ],
        out_specs=pl.BlockSpec((tm, D_MODEL), lambda i, j: (i, 0)),
        out_shape=jax.ShapeDtypeStruct((n, D_MODEL), _F32),
        scratch_shapes=[pltpu.VMEM((tm, D_MODEL), _BF16), pltpu.VMEM((tm, D_MODEL), _F32)],
        compiler_params=pltpu.CompilerParams(
            dimension_semantics=("parallel", "arbitrary"), vmem_limit_bytes=VMEM_LIMIT),
        name="ffn",
    )(x, norm_g, wg, wu, wd)


def _rope_slot(x, xs, g, gs, cos, sin):
    r = _rms(x, QK_ROPE_DIM)
    return (x * r * g) * cos + (xs * r * gs) * sin


def _gelu(x):
    return 0.5 * x * (1.0 + lax.erf(x * (2.0 ** -0.5)))


def _proj_kernel(sample, x_ref, mixg_ref, win_ref, qlg_ref, wuq_ref, kvg_ref, wuk_ref, wuv_ref,
                 qng_ref, qrg_ref, qrgs_ref, kng_ref, krg_ref, krgs_ref, vg_ref, ws_ref, bs_ref,
                 wog_ref, cos_ref, sin_ref, *rest):
    if sample:
        qcat_ref, ckv_ref, kr_ref, vn_ref, gate_ref, gout_ref, gm_scr = rest
    else:
        q_ref, k_ref, v_ref, ckv_ref, kr_ref, gate_ref, gout_ref, gm_scr = rest
    tm = x_ref.shape[0]
    x = x_ref[...]
    h = (x * _rms(x, D_MODEL) * mixg_ref[...]).astype(_BF16)
    cos = cos_ref[...]
    sin = sin_ref[...]

    def win(off, width):
        return _dot(h, win_ref[:, off:off + width])

    kr = _rope_slot(win(_OFF_KR, ROPE_SLOT), win(_OFF_KRS, ROPE_SLOT),
                    krg_ref[...], krgs_ref[...], cos, sin)[:, :QK_ROPE_DIM]
    kr_ref[...] = kr

    ckv = win(_OFF_CKV, KV_LORA_RANK)
    ckv = ckv * _rms(ckv, KV_LORA_RANK) * kvg_ref[...]
    ckv_ref[...] = ckv
    ckv_b = ckv.astype(_BF16)
    if not sample:
        kexp = _dot(ckv_b, wuk_ref[...])
        vals = _dot(ckv_b, wuv_ref[...])
        kr_b = kr.astype(_BF16)
        for hd in range(N_HEADS):
            sl = slice(hd * QK_NOPE_DIM, (hd + 1) * QK_NOPE_DIM)
            kh = kexp[:, sl]
            k_ref[hd, :, 0:QK_NOPE_DIM] = (kh * _rms(kh, QK_NOPE_DIM) * kng_ref[...]).astype(_BF16)
            k_ref[hd, :, QK_NOPE_DIM:QK_HEAD_DIM] = kr_b
            v_ref[hd] = vals[:, hd * V_HEAD_DIM:(hd + 1) * V_HEAD_DIM].astype(_BF16)

    cq = win(_OFF_CQ, Q_LORA_RANK)
    cq_b = (cq * _rms(cq, Q_LORA_RANK) * qlg_ref[...]).astype(_BF16)
    nq = N_HEADS * QK_NOPE_DIM
    nr = N_HEADS * ROPE_SLOT
    q_nope = _dot(cq_b, wuq_ref[:, 0:nq])
    q_rope = _dot(cq_b, wuq_ref[:, nq:nq + nr])
    q_rope_s = _dot(cq_b, wuq_ref[:, nq + nr:nq + 2 * nr])
    for hd in range(N_HEADS):
        sl = slice(hd * QK_NOPE_DIM, (hd + 1) * QK_NOPE_DIM)
        qh = q_nope[:, sl]
        qh = qh * _rms(qh, QK_NOPE_DIM) * qng_ref[...] * ATTN_SCALE
        rs = slice(hd * ROPE_SLOT, (hd + 1) * ROPE_SLOT)
        qr = _rope_slot(q_rope[:, rs], q_rope_s[:, rs], qrg_ref[...], qrgs_ref[...], cos, sin)
        qr = qr[:, :QK_ROPE_DIM] * ATTN_SCALE
        if sample:
            qa = _dot_t((qh * kng_ref[...]).astype(_BF16), wuk_ref[:, sl])
            qcat_ref[hd, :, 0:KV_LORA_RANK] = qa
            qcat_ref[hd, :, KV_LORA_RANK:KV_LORA_RANK + QK_ROPE_DIM] = qr
        else:
            q_ref[hd, :, 0:QK_NOPE_DIM] = qh.astype(_BF16)
            q_ref[hd, :, QK_NOPE_DIM:QK_HEAD_DIM] = qr.astype(_BF16)

    u = _gelu(win(_OFF_U, GMLP_WIDTH))
    v = _gelu(win(_OFF_V, GMLP_WIDTH))
    row = lax.broadcasted_iota(jnp.int32, (CHUNK, CHUNK), 0)
    col = lax.broadcasted_iota(jnp.int32, (CHUNK, CHUNK), 1)
    for g in range(GMLP_GROUPS):
        gs = slice(g * GMLP_GROUP_DIM, (g + 1) * GMLP_GROUP_DIM)
        vg = v[:, gs]
        vg = vg * _rms(vg, GMLP_GROUP_DIM) * vg_ref[:, gs]
        if sample:
            vn_ref[:, gs] = vg
        vg_b = vg.astype(_BF16)
        w = jnp.where(col <= row, ws_ref[g], 0.0).astype(_BF16)
        for c in range(tm // CHUNK):
            cs = slice(c * CHUNK, (c + 1) * CHUNK)
            mix = _dot(w, vg_b[cs]) + bs_ref[g]
            gm_scr[cs, gs] = (u[cs, gs] * mix).astype(_BF16)
    gout_ref[...] = jax.nn.sigmoid(win(_OFF_GG, D_MODEL)) * _dot(gm_scr[...], wog_ref[...])
    gate_ref[...] = jax.nn.sigmoid(win(_OFF_GA, D_MODEL)).astype(_BF16)


def _const_spec(shape):
    nd = len(shape)
    return pl.BlockSpec(shape, lambda i: (0,) * nd, pipeline_mode=pl.Buffered(1))


def _proj(sample, x, w, cos, sin, n_pos_tiles):
    n = x.shape[0]
    tm = PROJ_TM
    tok = lambda width: pl.BlockSpec((tm, width), lambda i: (i, 0))
    heads = lambda width: pl.BlockSpec((N_HEADS, tm, width), lambda i: (0, i, 0))
    weights = [w["mix_norm"], w["w_in"], w["q_lora_norm"], w["w_uq"], w["kv_lora_norm"], w["w_uk"],
               w["w_uv"], w["q_nope_norm"], w["q_rope_norm"], w["q_rope_norm_s"], w["k_nope_norm"],
               w["k_rope_norm"], w["k_rope_norm_s"], w["gmlp_v_norm"], w["gmlp_w_s"], w["gmlp_b_s"],
               w["w_o_gmlp"]]
    pos_spec = pl.BlockSpec((tm, ROPE_SLOT), lambda i: (i % n_pos_tiles, 0))
    in_specs = [tok(D_MODEL)] + [_const_spec(a.shape) for a in weights] + [pos_spec, pos_spec]
    sds = jax.ShapeDtypeStruct
    if sample:
        out_shape = [sds((N_HEADS, n, KV_LORA_RANK + QK_ROPE_DIM), _F32), sds((n, KV_LORA_RANK), _F32),
                     sds((n, QK_ROPE_DIM), _F32), sds((n, GMLP_WIDTH), _F32),
                     sds((n, D_MODEL), _BF16), sds((n, D_MODEL), _F32)]
        out_specs = [heads(KV_LORA_RANK + QK_ROPE_DIM), tok(KV_LORA_RANK), tok(QK_ROPE_DIM),
                     tok(GMLP_WIDTH), tok(D_MODEL), tok(D_MODEL)]
    else:
        out_shape = [sds((N_HEADS, n, QK_HEAD_DIM), _BF16), sds((N_HEADS, n, QK_HEAD_DIM), _BF16),
                     sds((N_HEADS, n, V_HEAD_DIM), _BF16), sds((n, KV_LORA_RANK), _F32),
                     sds((n, QK_ROPE_DIM), _F32), sds((n, D_MODEL), _BF16), sds((n, D_MODEL), _F32)]
        out_specs = [heads(QK_HEAD_DIM), heads(QK_HEAD_DIM), heads(V_HEAD_DIM), tok(KV_LORA_RANK),
                     tok(QK_ROPE_DIM), tok(D_MODEL), tok(D_MODEL)]
    return pl.pallas_call(
        functools.partial(_proj_kernel, sample),
        grid=(n // tm,),
        in_specs=in_specs,
        out_specs=out_specs,
        out_shape=out_shape,
        scratch_shapes=[pltpu.VMEM((tm, GMLP_WIDTH), _BF16)],
        compiler_params=pltpu.CompilerParams(
            dimension_semantics=("parallel",), vmem_limit_bytes=VMEM_LIMIT),
        name="proj_sample" if sample else "proj_prompt",
    )(x, *weights, cos, sin)


def _softmax_step(s, v_b, m_scr, l_scr, acc_scr):
    m_prev = m_scr[...]
    m_new = jnp.maximum(m_prev, jnp.max(s, axis=-1, keepdims=True))
    alpha = jnp.exp(m_prev - m_new)
    p = jnp.exp(s - m_new)
    l_scr[...] = alpha * l_scr[...] + jnp.sum(p, axis=-1, keepdims=True)
    acc_scr[...] = alpha * acc_scr[...] + _dot(p.astype(_BF16), v_b)
    m_scr[...] = m_new


def _pattn_kernel(q_ref, k_ref, v_ref, o_ref, m_scr, l_scr, acc_scr):
    qi = pl.program_id(2)
    q = q_ref[0]
    m_scr[...] = jnp.full_like(m_scr, NEG_INF)
    l_scr[...] = jnp.zeros_like(l_scr)
    acc_scr[...] = jnp.zeros_like(acc_scr)

    def step(kb, masked):
        off = pl.multiple_of(kb * ATT_TK, ATT_TK)
        s = _dot_t(q, k_ref[0, pl.ds(off, ATT_TK), :])
        if masked:
            row = lax.broadcasted_iota(jnp.int32, s.shape, 0)
            col = lax.broadcasted_iota(jnp.int32, s.shape, 1)
            s = jnp.where(col <= row, s, NEG_INF)
        _softmax_step(s, v_ref[0, pl.ds(off, ATT_TK), :], m_scr, l_scr, acc_scr)

    def body(kb, carry):
        step(kb, False)
        return carry

    lax.fori_loop(0, qi, body, 0)
    step(qi, True)
    o_ref[...] = (acc_scr[...] / l_scr[...]).astype(o_ref.dtype)


def _prompt_attention(q, k, v, batch, seq):
    nq = seq // ATT_TQ
    return pl.pallas_call(
        _pattn_kernel,
        grid=(batch, N_HEADS, nq),
        in_specs=[
            pl.BlockSpec((1, ATT_TQ, QK_HEAD_DIM), lambda b, h, i: (h, b * nq + i, 0)),
            pl.BlockSpec((1, seq, QK_HEAD_DIM), lambda b, h, i: (h, b, 0)),
            pl.BlockSpec((1, seq, V_HEAD_DIM), lambda b, h, i: (h, b, 0)),
        ],
        out_specs=pl.BlockSpec((ATT_TQ, V_HEAD_DIM), lambda b, h, i: (b * nq + i, h)),
        out_shape=jax.ShapeDtypeStruct((batch * seq, N_HEADS * V_HEAD_DIM), _BF16),
        scratch_shapes=[pltpu.VMEM((ATT_TQ, 1), _F32), pltpu.VMEM((ATT_TQ, 1), _F32),
                        pltpu.VMEM((ATT_TQ, V_HEAD_DIM), _F32)],
        compiler_params=pltpu.CompilerParams(
            dimension_semantics=("parallel", "parallel", "arbitrary"), vmem_limit_bytes=VMEM_LIMIT),
        name="prompt_attention",
    )(q, k, v)


def _sattn_kernel(n_pages, pt_ref, qcat_ref, cnew_ref, krnew_ref, wukt_ref, lat_hbm, rope_hbm, o_ref,
                  a_scr, lat_buf, rope_buf, sem_lat, sem_rope, m_scr, l_scr, acc_scr):
    pages = SATT_PAGES_PER_CHUNK
    n_chunks = n_pages // pages
    nk = N_HEADS * QK_NOPE_DIM
    nrow = N_HEADS * 8
    s = pl.program_id(0)
    n_seq = pl.num_programs(0)

    def copies(seq, c, slot):
        out = []
        for g in range(pages):
            page = pt_ref[seq * n_pages + c * pages + g]
            dst = pl.ds(g * PAGE_SIZE, PAGE_SIZE)
            out.append(pltpu.make_async_copy(lat_hbm.at[page], lat_buf.at[slot, dst], sem_lat.at[slot]))
            out.append(pltpu.make_async_copy(rope_hbm.at[page], rope_buf.at[slot, dst], sem_rope.at[slot]))
        return out

    @pl.when(s == 0)
    def _():
        a_scr[0:nk, :] = wukt_ref[...]
        for d in copies(0, 0, 0):
            d.start()

    q2 = qcat_ref[...].reshape(nrow, KV_LORA_RANK + QK_ROPE_DIM)
    a_scr[nk:nk + nrow, :] = q2[:, :KV_LORA_RANK].astype(_BF16)
    q_rope = q2[:, KV_LORA_RANK:].astype(_BF16)
    m_scr[...] = jnp.full_like(m_scr, NEG_INF)
    l_scr[...] = jnp.zeros_like(l_scr)
    acc_scr[...] = jnp.zeros_like(acc_scr)

    def process(c_b, kr_b, mask):
        big = _dot_t(a_scr[...], c_b)
        s_rope = _dot_t(q_rope, kr_b)
        rows = []
        for hd in range(N_HEADS):
            kx = big[hd * QK_NOPE_DIM:(hd + 1) * QK_NOPE_DIM, :]
            r = lax.rsqrt(jnp.sum(kx * kx, axis=0, keepdims=True) * (1.0 / QK_NOPE_DIM) + EPS)
            rows.append(big[nk + 8 * hd:nk + 8 * hd + 8, :] * r + s_rope[8 * hd:8 * hd + 8, :])
        sc = jnp.concatenate(rows, axis=0)
        if mask is not None:
            sc = jnp.where(mask, sc, NEG_INF)
        _softmax_step(sc, c_b, m_scr, l_scr, acc_scr)

    def chunk(c, carry):
        slot = c % 2

        @pl.when(c + 1 < n_chunks)
        def _():
            for d in copies(s, c + 1, 1 - slot):
                d.start()

        @pl.when((c + 1 == n_chunks) & (s + 1 < n_seq))
        def _():
            for d in copies(s + 1, 0, 1 - slot):
                d.start()

        for d in copies(s, c, slot):
            d.wait()
        process(lat_buf[slot].astype(_BF16), rope_buf[slot].astype(_BF16), None)
        return carry

    lax.fori_loop(0, n_chunks, chunk, 0)

    pad = PAGE_SIZE - cnew_ref.shape[0]
    c_new = jnp.concatenate([cnew_ref[...], jnp.zeros((pad, KV_LORA_RANK), _F32)], axis=0)
    kr_new = jnp.concatenate([krnew_ref[...], jnp.zeros((pad, QK_ROPE_DIM), _F32)], axis=0)
    row = lax.broadcasted_iota(jnp.int32, (nrow, PAGE_SIZE), 0)
    col = lax.broadcasted_iota(jnp.int32, (nrow, PAGE_SIZE), 1)
    process(c_new.astype(_BF16), kr_new.astype(_BF16), col <= (row & 7))

    o_ref[...] = (acc_scr[...] / l_scr[...]).reshape(o_ref.shape)


def _sample_attention(page_table, qcat, c_new, kr_new, wukt, cache_lat, cache_rope, t_new):
    n_seq, n_pages = page_table.shape
    assert t_new == 8 and n_pages % (2 * SATT_PAGES_PER_CHUNK) == 0
    tk = SATT_PAGES_PER_CHUNK * PAGE_SIZE
    nrow = N_HEADS * t_new
    dq = KV_LORA_RANK + QK_ROPE_DIM
    grid_spec = pltpu.PrefetchScalarGridSpec(
        num_scalar_prefetch=1,
        grid=(n_seq,),
        in_specs=[
            pl.BlockSpec((N_HEADS, t_new, dq), lambda s, pt: (0, s, 0)),
            pl.BlockSpec((t_new, KV_LORA_RANK), lambda s, pt: (s, 0)),
            pl.BlockSpec((t_new, QK_ROPE_DIM), lambda s, pt: (s, 0)),
            pl.BlockSpec((N_HEADS * QK_NOPE_DIM, KV_LORA_RANK), lambda s, pt: (0, 0)),
            pl.BlockSpec(memory_space=pl.ANY),
            pl.BlockSpec(memory_space=pl.ANY),
        ],
        out_specs=pl.BlockSpec((N_HEADS, t_new, KV_LORA_RANK), lambda s, pt: (0, s, 0)),
        scratch_shapes=[
            pltpu.VMEM((N_HEADS * QK_NOPE_DIM + nrow, KV_LORA_RANK), _BF16),
            pltpu.VMEM((2, tk, KV_LORA_RANK), _F32),
            pltpu.VMEM((2, tk, QK_ROPE_DIM), _F32),
            pltpu.SemaphoreType.DMA((2,)),
            pltpu.SemaphoreType.DMA((2,)),
            pltpu.VMEM((nrow, 1), _F32),
            pltpu.VMEM((nrow, 1), _F32),
            pltpu.VMEM((nrow, KV_LORA_RANK), _F32),
        ],
    )
    return pl.pallas_call(
        functools.partial(_sattn_kernel, n_pages),
        grid_spec=grid_spec,
        out_shape=jax.ShapeDtypeStruct((N_HEADS, n_seq * t_new, KV_LORA_RANK), _F32),
        compiler_params=pltpu.CompilerParams(
            dimension_semantics=("arbitrary",), vmem_limit_bytes=VMEM_LIMIT),
        name="sample_attention",
    )(page_table.reshape(-1), qcat, c_new, kr_new, wukt, cache_lat, cache_rope)


def _merge_kernel(from_latent, a_ref, gate_ref, gout_ref, x_ref, wuv_ref, woa_ref, wout_ref, o_ref):
    if from_latent:
        heads = [_dot(a_ref[hd].astype(_BF16), wuv_ref[:, hd * V_HEAD_DIM:(hd + 1) * V_HEAD_DIM])
                 for hd in range(N_HEADS)]
        attn = jnp.concatenate(heads, axis=-1).astype(_BF16)
    else:
        attn = a_ref[...]
    merged = gate_ref[...].astype(_F32) * _dot(attn, woa_ref[...]) + gout_ref[...]
    o_ref[...] = x_ref[...] + _dot(merged.astype(_BF16), wout_ref[...])


def _merge(from_latent, attn, gate, gout, x, wuv, woa, wout):
    n = x.shape[0]
    tm = MERGE_TM
    tok = pl.BlockSpec((tm, D_MODEL), lambda i: (i, 0))
    if from_latent:
        a_spec = pl.BlockSpec((N_HEADS, tm, KV_LORA_RANK), lambda i: (0, i, 0))
    else:
        a_spec = tok
    return pl.pallas_call(
        functools.partial(_merge_kernel, from_latent),
        grid=(n // tm,),
        in_specs=[a_spec, tok, tok, tok, _const_spec(wuv.shape), _const_spec(woa.shape),
                  _const_spec(wout.shape)],
        out_specs=tok,
        out_shape=jax.ShapeDtypeStruct((n, D_MODEL), _F32),
        compiler_params=pltpu.CompilerParams(
            dimension_semantics=("parallel",), vmem_limit_bytes=VMEM_LIMIT),
        name="merge_sample" if from_latent else "merge_prompt",
    )(attn, gate, gout, x, wuv, woa, wout)


def _swap_halves(a, axis):
    lo, hi = jnp.split(a, 2, axis=axis)
    return jnp.concatenate([hi, lo], axis=axis)


def _slot(a):
    return jnp.concatenate([a, jnp.zeros_like(a)], axis=-1)


def _rope_tables(pos):
    half = QK_ROPE_DIM // 2
    inv = ROPE_BASE ** (-jnp.arange(half, dtype=_F32) / half)
    ang = pos.astype(_F32)[:, None] * inv[None, :]
    cos, sin = jnp.cos(ang), jnp.sin(ang)
    return _slot(jnp.concatenate([cos, cos], axis=-1)), _slot(jnp.concatenate([-sin, sin], axis=-1))


def _prep_weights(w_in, w_uq, w_uk, w_uv, w_o_gmlp, mix_norm, q_lora_norm, kv_lora_norm, q_nope_norm,
                  q_rope_norm, k_nope_norm, k_rope_norm, gmlp_v_norm):
    sizes = (Q_LORA_RANK, KV_LORA_RANK, QK_ROPE_DIM, GMLP_WIDTH, GMLP_WIDTH, D_MODEL, D_MODEL)
    c_q, c_kv, k_r, u, v, g_a, g_g = jnp.split(w_in, np.cumsum(sizes)[:-1].tolist(), axis=-1)
    w_in_p = jnp.concatenate([c_q, c_kv, u, v, g_a, g_g, _slot(k_r), _slot(_swap_halves(k_r, -1))], axis=-1)
    assert w_in_p.shape[-1] == _D_IN_PADDED
    wq = w_uq.reshape(Q_LORA_RANK, N_HEADS, QK_HEAD_DIM)
    wq_nope = wq[:, :, :QK_NOPE_DIM].reshape(Q_LORA_RANK, -1)
    wq_rope = wq[:, :, QK_NOPE_DIM:]
    w_uq_p = jnp.concatenate([wq_nope, _slot(wq_rope).reshape(Q_LORA_RANK, -1),
                              _slot(_swap_halves(wq_rope, -1)).reshape(Q_LORA_RANK, -1)], axis=-1)
    row = lambda a: a.reshape(1, -1).astype(_F32)
    return dict(
        mix_norm=row(mix_norm), w_in=w_in_p.astype(_BF16), q_lora_norm=row(q_lora_norm),
        w_uq=w_uq_p.astype(_BF16), kv_lora_norm=row(kv_lora_norm), w_uk=w_uk.astype(_BF16),
        w_uv=w_uv.astype(_BF16), q_nope_norm=row(q_nope_norm), q_rope_norm=row(_slot(q_rope_norm)),
        q_rope_norm_s=row(_slot(_swap_halves(q_rope_norm, -1))), k_nope_norm=row(k_nope_norm),
        k_rope_norm=row(_slot(k_rope_norm)), k_rope_norm_s=row(_slot(_swap_halves(k_rope_norm, -1))),
        gmlp_v_norm=row(gmlp_v_norm), w_o_gmlp=w_o_gmlp.astype(_BF16))


def kernel(x_prompt, x_sample, cache_kv_latent, cache_k_rope, page_table, ffn1_norm, ffn1_w_gate, ffn1_w_up, ffn1_w_down, mix_norm, w_in, q_lora_norm, w_uq, kv_lora_norm, w_uk, w_uv, q_nope_norm, q_rope_norm, k_nope_norm, k_rope_norm, gmlp_v_norm, gmlp_w_s, gmlp_b_s, w_o_attn, w_o_gmlp, w_out, ffn2_norm, ffn2_w_gate, ffn2_w_up, ffn2_w_down):
    batch, seq, _ = x_prompt.shape
    n_seq, t_new, _ = x_sample.shape
    assert ffn1_norm.shape[0] == 1
    past = page_table.shape[1] * PAGE_SIZE
    n_pool = cache_kv_latent.shape[1]
    l = 0
    row = lambda a: a.reshape(1, -1).astype(_F32)
    w = _prep_weights(w_in[l], w_uq[l], w_uk[l], w_uv[l], w_o_gmlp[l], mix_norm[l], q_lora_norm[l],
                      kv_lora_norm[l], q_nope_norm[l], q_rope_norm[l], k_nope_norm[l], k_rope_norm[l],
                      gmlp_v_norm[l])
    ffn1 = (row(ffn1_norm[l]), ffn1_w_gate[l].astype(_BF16), ffn1_w_up[l].astype(_BF16),
            ffn1_w_down[l].astype(_BF16))
    ffn2 = (row(ffn2_norm[l]), ffn2_w_gate[l].astype(_BF16), ffn2_w_up[l].astype(_BF16),
            ffn2_w_down[l].astype(_BF16))
    woa = w_o_attn[l].astype(_BF16)
    wout = w_out[l].astype(_BF16)

    wp = dict(w, gmlp_w_s=gmlp_w_s[l].astype(_F32), gmlp_b_s=gmlp_b_s[l].reshape(GMLP_GROUPS, CHUNK, 1))
    cos_p, sin_p = _rope_tables(jnp.arange(seq))
    xp = _ffn(x_prompt.reshape(batch * seq, D_MODEL), *ffn1)
    q, k, v, ckv_p, kr_p, gate_p, gout_p = _proj(False, xp, wp, cos_p, sin_p, seq // PROJ_TM)
    attn_p = _prompt_attention(q, k, v, batch, seq)
    xp = _merge(False, attn_p, gate_p, gout_p, xp, w["w_uv"], woa, wout)
    xp = _ffn(xp, *ffn2)

    reps = CHUNK // t_new
    ws_s = jax.vmap(lambda m: jnp.kron(jnp.eye(reps, dtype=_F32), m))(gmlp_w_s[l][:, :t_new, :t_new])
    bs_s = jnp.tile(gmlp_b_s[l][:, :t_new], (1, reps)).reshape(GMLP_GROUPS, CHUNK, 1)
    ws = dict(w, gmlp_w_s=ws_s, gmlp_b_s=bs_s)
    cos_s, sin_s = _rope_tables(past + jnp.arange(PROJ_TM) % t_new)
    xs = _ffn(x_sample.reshape(n_seq * t_new, D_MODEL), *ffn1)
    qcat, ckv_s, kr_s, vn_s, gate_s, gout_s = _proj(True, xs, ws, cos_s, sin_s, 1)
    o_lat = _sample_attention(page_table, qcat, ckv_s, kr_s, w["w_uk"].T,
                              cache_kv_latent.reshape(n_pool, PAGE_SIZE, KV_LORA_RANK),
                              cache_k_rope.reshape(n_pool, PAGE_SIZE, QK_ROPE_DIM), t_new)
    xs = _merge(True, o_lat, gate_s, gout_s, xs, w["w_uv"], woa, wout)
    xs = _ffn(xs, *ffn2)

    return (xp.reshape(batch, seq, D_MODEL), xs.reshape(n_seq, t_new, D_MODEL),
            ckv_p.reshape(1, batch, seq, KV_LORA_RANK), kr_p.reshape(1, batch, seq, QK_ROPE_DIM),
            ckv_s.reshape(1, n_seq, t_new, KV_LORA_RANK), kr_s.reshape(1, n_seq, t_new, QK_ROPE_DIM),
            vn_s.reshape(1, n_seq, t_new, GMLP_WIDTH))
```

```python
import functools

import jax
import jax.numpy as jnp
import numpy as np
from jax import lax
from jax.experimental import pallas as pl
from jax.experimental.pallas import tpu as pltpu

D_MODEL = 1024
N_HEADS = 8
QK_NOPE_DIM = 128
QK_ROPE_DIM = 64
QK_HEAD_DIM = QK_NOPE_DIM + QK_ROPE_DIM
V_HEAD_DIM = 128
Q_LORA_RANK = 256
KV_LORA_RANK = 256
ROPE_BASE = 10000.0
ATTN_SCALE = QK_HEAD_DIM ** -0.5
GMLP_GROUPS = 4
GMLP_WIDTH = 1024
GMLP_GROUP_DIM = GMLP_WIDTH // GMLP_GROUPS
CHUNK = 128
PAGE_SIZE = 128
D_FF = 2816
EPS = 1e-6
NEG_INF = -1e30

LANE = 128
ROPE_SLOT = LANE

_OFF_CQ = 0
_OFF_CKV = _OFF_CQ + Q_LORA_RANK
_OFF_U = _OFF_CKV + KV_LORA_RANK
_OFF_V = _OFF_U + GMLP_WIDTH
_OFF_GA = _OFF_V + GMLP_WIDTH
_OFF_GG = _OFF_GA + D_MODEL
_OFF_KR = _OFF_GG + D_MODEL
_OFF_KRS = _OFF_KR + ROPE_SLOT
_D_IN_PADDED = _OFF_KRS + ROPE_SLOT

FFN_TM = 1024
FFN_TF = 256
PROJ_TM = 512
ATT_TQ = 512
ATT_TK = 512
ATT_HB = 2
MERGE_TM = 512
SATT_PAGES_PER_CHUNK = 16
SATT_LOOKAHEAD = 2
SATT_SLOTS = SATT_LOOKAHEAD + 2
VMEM_LIMIT = 56 * 1024 * 1024

_BF16 = jnp.bfloat16
_F32 = jnp.float32


def _dot(a, b):
    return jnp.dot(a, b, preferred_element_type=_F32)


def _dot_t(a, b):
    return lax.dot_general(a, b, (((1,), (1,)), ((), ())), preferred_element_type=_F32)


def _rms(x, n):
    return lax.rsqrt(jnp.sum(x * x, axis=-1, keepdims=True) * (1.0 / n) + EPS)


def _ffn_kernel(x_ref, g_ref, wg_ref, wu_ref, wd_ref, o_ref, h_scr, acc_scr):
    j = pl.program_id(1)

    @pl.when(j == 0)
    def _():
        x = x_ref[...]
        h_scr[...] = (x * _rms(x, D_MODEL) * g_ref[...]).astype(_BF16)
        acc_scr[...] = jnp.zeros_like(acc_scr)

    h = h_scr[...]
    gate = _dot(h, wg_ref[...])
    up = _dot(h, wu_ref[...])
    act = (gate * jax.nn.sigmoid(gate) * up).astype(_BF16)
    acc_scr[...] += _dot(act, wd_ref[...])

    @pl.when(j == pl.num_programs(1) - 1)
    def _():
        o_ref[...] = x_ref[...] + 0.5 * acc_scr[...]


def _ffn(x, norm_g, wg, wu, wd):
    n = x.shape[0]
    tm = min(FFN_TM, n)
    return pl.pallas_call(
        _ffn_kernel,
        grid=(n // tm, D_FF // FFN_TF),
        in_specs=[
            pl.BlockSpec((tm, D_MODEL), lambda i, j: (i, 0)),
            pl.BlockSpec((1, D_MODEL), lambda i, j: (0, 0)),
            pl.BlockSpec((D_MODEL, FFN_TF), lambda i, j: (0, j)),
            pl.BlockSpec((D_MODEL, FFN_TF), lambda i, j: (0, j)),
            pl.BlockSpec((FFN_TF, D_MODEL), lambda i, j: (j, 0)),
        ],
        out_specs=pl.BlockSpec((tm, D_MODEL), lambda i, j: (i, 0)),
        out_shape=jax.ShapeDtypeStruct((n, D_MODEL), _F32),
        scratch_shapes=[pltpu.VMEM((tm, D_MODEL), _BF16), pltpu.VMEM((tm, D_MODEL), _F32)],
        compiler_params=pltpu.CompilerParams(
            dimension_semantics=("parallel", "arbitrary"), vmem_limit_bytes=VMEM_LIMIT),
        name="ffn",
    )(x, norm_g, wg, wu, wd)


def _rope_slot(x, xs, g, gs, cos, sin):
    r = _rms(x, QK_ROPE_DIM)
    return (x * r * g) * cos + (xs * r * gs) * sin


def _gelu(x):
    return 0.5 * x * (1.0 + lax.erf(x * (2.0 ** -0.5)))


def _proj_kernel(sample, x_ref, mixg_ref, win_ref, qlg_ref, wuq_ref, kvg_ref, wuk_ref, wuv_ref,
                 qng_ref, qrg_ref, qrgs_ref, kng_ref, krg_ref, krgs_ref, vg_ref, ws_ref, bs_ref,
                 wog_ref, cos_ref, sin_ref, *rest):
    if sample:
        qcat_ref, ckv_ref, kr_ref, vn_ref, gate_ref, gout_ref, gm_scr = rest
    else:
        q_ref, k_ref, v_ref, ckv_ref, kr_ref, gate_ref, gout_ref, gm_scr = rest
    tm = x_ref.shape[0]
    x = x_ref[...]
    h = (x * _rms(x, D_MODEL) * mixg_ref[...]).astype(_BF16)
    cos = cos_ref[...]
    sin = sin_ref[...]

    def win(off, width):
        return _dot(h, win_ref[:, off:off + width])

    kr = _rope_slot(win(_OFF_KR, ROPE_SLOT), win(_OFF_KRS, ROPE_SLOT),
                    krg_ref[...], krgs_ref[...], cos, sin)[:, :QK_ROPE_DIM]
    kr_ref[...] = kr

    ckv = win(_OFF_CKV, KV_LORA_RANK)
    ckv = ckv * _rms(ckv, KV_LORA_RANK) * kvg_ref[...]
    ckv_ref[...] = ckv
    ckv_b = ckv.astype(_BF16)
    if not sample:
        kexp = _dot(ckv_b, wuk_ref[...])
        vals = _dot(ckv_b, wuv_ref[...])
        kr_b = kr.astype(_BF16)
        for hd in range(N_HEADS):
            sl = slice(hd * QK_NOPE_DIM, (hd + 1) * QK_NOPE_DIM)
            kh = kexp[:, sl]
            k_ref[hd, :, 0:QK_NOPE_DIM] = (kh * _rms(kh, QK_NOPE_DIM) * kng_ref[...]).astype(_BF16)
            k_ref[hd, :, QK_NOPE_DIM:QK_HEAD_DIM] = kr_b
            v_ref[hd] = vals[:, hd * V_HEAD_DIM:(hd + 1) * V_HEAD_DIM].astype(_BF16)

    cq = win(_OFF_CQ, Q_LORA_RANK)
    cq_b = (cq * _rms(cq, Q_LORA_RANK) * qlg_ref[...]).astype(_BF16)
    nq = N_HEADS * QK_NOPE_DIM
    nr = N_HEADS * ROPE_SLOT
    q_nope = _dot(cq_b, wuq_ref[:, 0:nq])
    q_rope = _dot(cq_b, wuq_ref[:, nq:nq + nr])
    q_rope_s = _dot(cq_b, wuq_ref[:, nq + nr:nq + 2 * nr])
    for hd in range(N_HEADS):
        sl = slice(hd * QK_NOPE_DIM, (hd + 1) * QK_NOPE_DIM)
        qh = q_nope[:, sl]
        qh = qh * _rms(qh, QK_NOPE_DIM) * qng_ref[...] * ATTN_SCALE
        rs = slice(hd * ROPE_SLOT, (hd + 1) * ROPE_SLOT)
        qr = _rope_slot(q_rope[:, rs], q_rope_s[:, rs], qrg_ref[...], qrgs_ref[...], cos, sin)
        qr = qr[:, :QK_ROPE_DIM] * ATTN_SCALE
        if sample:
            qa = _dot_t((qh * kng_ref[...]).astype(_BF16), wuk_ref[:, sl])
            qcat_ref[hd, :, 0:KV_LORA_RANK] = qa
            qcat_ref[hd, :, KV_LORA_RANK:KV_LORA_RANK + QK_ROPE_DIM] = qr
        else:
            q_ref[hd, :, 0:QK_NOPE_DIM] = qh.astype(_BF16)
            q_ref[hd, :, QK_NOPE_DIM:QK_HEAD_DIM] = qr.astype(_BF16)

    u = _gelu(win(_OFF_U, GMLP_WIDTH))
    v = _gelu(win(_OFF_V, GMLP_WIDTH))
    row = lax.broadcasted_iota(jnp.int32, (CHUNK, CHUNK), 0)
    col = lax.broadcasted_iota(jnp.int32, (CHUNK, CHUNK), 1)
    for g in range(GMLP_GROUPS):
        gs = slice(g * GMLP_GROUP_DIM, (g + 1) * GMLP_GROUP_DIM)
        vg = v[:, gs]
        vg = vg * _rms(vg, GMLP_GROUP_DIM) * vg_ref[:, gs]
        if sample:
            vn_ref[:, gs] = vg
        vg_b = vg.astype(_BF16)
        w = jnp.where(col <= row, ws_ref[g], 0.0).astype(_BF16)
        for c in range(tm // CHUNK):
            cs = slice(c * CHUNK, (c + 1) * CHUNK)
            mix = _dot(w, vg_b[cs]) + bs_ref[g]
            gm_scr[cs, gs] = (u[cs, gs] * mix).astype(_BF16)
    gout_ref[...] = jax.nn.sigmoid(win(_OFF_GG, D_MODEL)) * _dot(gm_scr[...], wog_ref[...])
    gate_ref[...] = jax.nn.sigmoid(win(_OFF_GA, D_MODEL)).astype(_BF16)


def _const_spec(shape):
    nd = len(shape)
    return pl.BlockSpec(shape, lambda i: (0,) * nd, pipeline_mode=pl.Buffered(1))


def _proj(sample, x, w, cos, sin, n_pos_tiles):
    n = x.shape[0]
    tm = PROJ_TM
    tok = lambda width: pl.BlockSpec((tm, width), lambda i: (i, 0))
    heads = lambda width: pl.BlockSpec((N_HEADS, tm, width), lambda i: (0, i, 0))
    weights = [w["mix_norm"], w["w_in"], w["q_lora_norm"], w["w_uq"], w["kv_lora_norm"], w["w_uk"],
               w["w_uv"], w["q_nope_norm"], w["q_rope_norm"], w["q_rope_norm_s"], w["k_nope_norm"],
               w["k_rope_norm"], w["k_rope_norm_s"], w["gmlp_v_norm"], w["gmlp_w_s"], w["gmlp_b_s"],
               w["w_o_gmlp"]]
    pos_spec = pl.BlockSpec((tm, ROPE_SLOT), lambda i: (i % n_pos_tiles, 0))
    in_specs = [tok(D_MODEL)] + [_const_spec(a.shape) for a in weights] + [pos_spec, pos_spec]
    sds = jax.ShapeDtypeStruct
    if sample:
        out_shape = [sds((N_HEADS, n, KV_LORA_RANK + QK_ROPE_DIM), _F32), sds((n, KV_LORA_RANK), _F32),
                     sds((n, QK_ROPE_DIM), _F32), sds((n, GMLP_WIDTH), _F32),
                     sds((n, D_MODEL), _BF16), sds((n, D_MODEL), _F32)]
        out_specs = [heads(KV_LORA_RANK + QK_ROPE_DIM), tok(KV_LORA_RANK), tok(QK_ROPE_DIM),
                     tok(GMLP_WIDTH), tok(D_MODEL), tok(D_MODEL)]
    else:
        out_shape = [sds((N_HEADS, n, QK_HEAD_DIM), _BF16), sds((N_HEADS, n, QK_HEAD_DIM), _BF16),
                     sds((N_HEADS, n, V_HEAD_DIM), _BF16), sds((n, KV_LORA_RANK), _F32),
                     sds((n, QK_ROPE_DIM), _F32), sds((n, D_MODEL), _BF16), sds((n, D_MODEL), _F32)]
        out_specs = [heads(QK_HEAD_DIM), heads(QK_HEAD_DIM), heads(V_HEAD_DIM), tok(KV_LORA_RANK),
                     tok(QK_ROPE_DIM), tok(D_MODEL), tok(D_MODEL)]
    return pl.pallas_call(
        functools.partial(_proj_kernel, sample),
        grid=(n // tm,),
        in_specs=in_specs,
        out_specs=out_specs,
        out_shape=out_shape,
        scratch_shapes=[pltpu.VMEM((tm, GMLP_WIDTH), _BF16)],
        compiler_params=pltpu.CompilerParams(
            dimension_semantics=("parallel",), vmem_limit_bytes=VMEM_LIMIT),
        name="proj_sample" if sample else "proj_prompt",
    )(x, *weights, cos, sin)


def _softmax_step(s, v_b, m_scr, l_scr, acc_scr):
    m_prev = m_scr[...]
    m_new = jnp.maximum(m_prev, jnp.max(s, axis=-1, keepdims=True))
    alpha = jnp.exp(m_prev - m_new)
    p = jnp.exp(s - m_new)
    l_scr[...] = alpha * l_scr[...] + jnp.sum(p, axis=-1, keepdims=True)
    acc_scr[...] = alpha * acc_scr[...] + _dot(p.astype(_BF16), v_b)
    m_scr[...] = m_new


def _pattn_kernel(q_ref, k_ref, v_ref, o_ref, s_scr, m_scr, acc_scr):
    qi = pl.program_id(2)
    m_scr[...] = jnp.full_like(m_scr, NEG_INF)
    acc_scr[...] = jnp.zeros_like(acc_scr)

    def stage_scores(kb, masked):
        off = pl.multiple_of(kb * ATT_TK, ATT_TK)
        for hh in range(ATT_HB):
            s = _dot_t(q_ref[hh], k_ref[hh, pl.ds(off, ATT_TK), :])
            if masked:
                row = lax.broadcasted_iota(jnp.int32, s.shape, 0)
                col = lax.broadcasted_iota(jnp.int32, s.shape, 1)
                s = jnp.where(col <= row, s, NEG_INF)
            s_scr[kb % 2, hh] = s

    def stage_update(kb):
        off = pl.multiple_of(kb * ATT_TK, ATT_TK)
        for hh in range(ATT_HB):
            s = s_scr[kb % 2, hh]
            v = v_ref[hh, pl.ds(off, ATT_TK), :]
            v1 = jnp.concatenate([v, jnp.ones_like(v)], axis=-1)
            m_prev = m_scr[hh]
            m_new = jnp.maximum(m_prev, jnp.max(s, axis=-1, keepdims=True))
            p = jnp.exp(s - m_new)
            acc_scr[hh] = jnp.exp(m_prev - m_new) * acc_scr[hh] + _dot(p.astype(_BF16), v1)
            m_scr[hh] = m_new

    @pl.when(qi == 0)
    def _():
        stage_scores(0, True)

    @pl.when(qi > 0)
    def _():
        stage_scores(0, False)

        def body(kb, carry):
            stage_update(kb - 1)
            stage_scores(kb, False)
            return carry

        lax.fori_loop(1, qi, body, 0)
        stage_update(qi - 1)
        stage_scores(qi, True)

    stage_update(qi)
    for hh in range(ATT_HB):
        acc = acc_scr[hh]
        o_ref[:, hh * V_HEAD_DIM:(hh + 1) * V_HEAD_DIM] = (
            acc[:, :V_HEAD_DIM] / acc[:, V_HEAD_DIM:]).astype(o_ref.dtype)


def _prompt_attention(q, k, v, batch, seq):
    nq = seq // ATT_TQ
    hb = ATT_HB
    return pl.pallas_call(
        _pattn_kernel,
        grid=(batch, N_HEADS // hb, nq),
        in_specs=[
            pl.BlockSpec((hb, ATT_TQ, QK_HEAD_DIM), lambda b, h, i: (h, b * nq + i, 0)),
            pl.BlockSpec((hb, seq, QK_HEAD_DIM), lambda b, h, i: (h, b, 0)),
            pl.BlockSpec((hb, seq, V_HEAD_DIM), lambda b, h, i: (h, b, 0)),
        ],
        out_specs=pl.BlockSpec((ATT_TQ, hb * V_HEAD_DIM), lambda b, h, i: (b * nq + i, h)),
        out_shape=jax.ShapeDtypeStruct((batch * seq, N_HEADS * V_HEAD_DIM), _BF16),
        scratch_shapes=[pltpu.VMEM((2, hb, ATT_TQ, ATT_TK), _F32),
                        pltpu.VMEM((hb, ATT_TQ, 1), _F32),
                        pltpu.VMEM((hb, ATT_TQ, 2 * V_HEAD_DIM), _F32)],
        compiler_params=pltpu.CompilerParams(
            dimension_semantics=("parallel", "parallel", "arbitrary"), vmem_limit_bytes=VMEM_LIMIT),
        name="prompt_attention",
    )(q, k, v)


def _sattn_kernel(n_pages, pt_ref, qcat_ref, cnew_ref, krnew_ref, wukt_ref, lat_hbm, ropet_hbm, o_ref,
                  a_scr, lat_buf, ropet_buf, sc_scr, sem_lat, sem_rope, m_scr, l_scr, acc_scr):
    pages = SATT_PAGES_PER_CHUNK
    n_chunks = n_pages // pages
    nk = N_HEADS * QK_NOPE_DIM
    nrow = N_HEADS * 8
    s = pl.program_id(0)
    total = pl.num_programs(0) * n_chunks
    base = s * n_chunks

    def copies(item, slot):
        out = []
        for g in range(pages):
            page = pt_ref[item * pages + g]
            dst = pl.ds(g * PAGE_SIZE, PAGE_SIZE)
            out.append(pltpu.make_async_copy(lat_hbm.at[page], lat_buf.at[slot, dst], sem_lat.at[slot]))
            out.append(pltpu.make_async_copy(ropet_hbm.at[page], ropet_buf.at[slot, :, dst], sem_rope.at[slot]))
        return out

    @pl.when(s == 0)
    def _():
        a_scr[0:nk, :] = wukt_ref[...]
        for t in range(SATT_LOOKAHEAD):
            for d in copies(t, t % SATT_SLOTS):
                d.start()

    q2 = qcat_ref[...].reshape(nrow, KV_LORA_RANK + QK_ROPE_DIM)
    a_scr[nk:nk + nrow, :] = q2[:, :KV_LORA_RANK].astype(_BF16)
    q_rope = q2[:, KV_LORA_RANK:].astype(_BF16)
    m_scr[...] = jnp.full_like(m_scr, NEG_INF)
    l_scr[...] = jnp.zeros_like(l_scr)
    acc_scr[...] = jnp.zeros_like(acc_scr)

    def scores(c_b, s_rope):
        big = _dot_t(a_scr[...], c_b)
        rows = []
        for hd in range(N_HEADS):
            kx = big[hd * QK_NOPE_DIM:(hd + 1) * QK_NOPE_DIM, :]
            r = lax.rsqrt(jnp.sum(kx * kx, axis=0, keepdims=True) * (1.0 / QK_NOPE_DIM) + EPS)
            rows.append(big[nk + 8 * hd:nk + 8 * hd + 8, :] * r + s_rope[8 * hd:8 * hd + 8, :])
        return jnp.concatenate(rows, axis=0)

    def slot_of(c):
        return lax.rem(base + c, SATT_SLOTS)

    def stage_dma(c):
        t = base + c

        @pl.when(t + SATT_LOOKAHEAD < total)
        def _():
            for d in copies(t + SATT_LOOKAHEAD, lax.rem(t + SATT_LOOKAHEAD, SATT_SLOTS)):
                d.start()

        for d in copies(t, slot_of(c)):
            d.wait()

    def stage_scores(c):
        slot = slot_of(c)
        sc_scr[c % 2] = scores(lat_buf[slot].astype(_BF16), _dot(q_rope, ropet_buf[slot].astype(_BF16)))

    def stage_update(c):
        _softmax_step(sc_scr[c % 2], lat_buf[slot_of(c)].astype(_BF16), m_scr, l_scr, acc_scr)

    stage_dma(0)
    stage_scores(0)

    def body(c, carry):
        stage_dma(c)
        stage_update(c - 1)
        stage_scores(c)
        return carry

    lax.fori_loop(1, n_chunks, body, 0)
    stage_update(n_chunks - 1)

    pad = PAGE_SIZE - cnew_ref.shape[0]
    c_new = jnp.concatenate([cnew_ref[...], jnp.zeros((pad, KV_LORA_RANK), _F32)], axis=0).astype(_BF16)
    kr_new = jnp.concatenate([krnew_ref[...], jnp.zeros((pad, QK_ROPE_DIM), _F32)], axis=0).astype(_BF16)
    row = lax.broadcasted_iota(jnp.int32, (nrow, PAGE_SIZE), 0)
    col = lax.broadcasted_iota(jnp.int32, (nrow, PAGE_SIZE), 1)
    sc = jnp.where(col <= (row & 7), scores(c_new, _dot_t(q_rope, kr_new)), NEG_INF)
    _softmax_step(sc, c_new, m_scr, l_scr, acc_scr)

    o_ref[...] = (acc_scr[...] / l_scr[...]).reshape(o_ref.shape)


def _sample_attention(page_table, qcat, c_new, kr_new, wukt, cache_lat, cache_rope_t, t_new):
    n_seq, n_pages = page_table.shape
    assert t_new == 8 and n_pages % SATT_PAGES_PER_CHUNK == 0
    assert n_seq * (n_pages // SATT_PAGES_PER_CHUNK) >= SATT_LOOKAHEAD
    tk = SATT_PAGES_PER_CHUNK * PAGE_SIZE
    nrow = N_HEADS * t_new
    dq = KV_LORA_RANK + QK_ROPE_DIM
    grid_spec = pltpu.PrefetchScalarGridSpec(
        num_scalar_prefetch=1,
        grid=(n_seq,),
        in_specs=[
            pl.BlockSpec((N_HEADS, t_new, dq), lambda s, pt: (0, s, 0)),
            pl.BlockSpec((t_new, KV_LORA_RANK), lambda s, pt: (s, 0)),
            pl.BlockSpec((t_new, QK_ROPE_DIM), lambda s, pt: (s, 0)),
            pl.BlockSpec((N_HEADS * QK_NOPE_DIM, KV_LORA_RANK), lambda s, pt: (0, 0)),
            pl.BlockSpec(memory_space=pl.ANY),
            pl.BlockSpec(memory_space=pl.ANY),
        ],
        out_specs=pl.BlockSpec((N_HEADS, t_new, KV_LORA_RANK), lambda s, pt: (0, s, 0)),
        scratch_shapes=[
            pltpu.VMEM((N_HEADS * QK_NOPE_DIM + nrow, KV_LORA_RANK), _BF16),
            pltpu.VMEM((SATT_SLOTS, tk, KV_LORA_RANK), _F32),
            pltpu.VMEM((SATT_SLOTS, QK_ROPE_DIM, tk), _F32),
            pltpu.VMEM((2, nrow, tk), _F32),
            pltpu.SemaphoreType.DMA((SATT_SLOTS,)),
            pltpu.SemaphoreType.DMA((SATT_SLOTS,)),
            pltpu.VMEM((nrow, 1), _F32),
            pltpu.VMEM((nrow, 1), _F32),
            pltpu.VMEM((nrow, KV_LORA_RANK), _F32),
        ],
    )
    return pl.pallas_call(
        functools.partial(_sattn_kernel, n_pages),
        grid_spec=grid_spec,
        out_shape=jax.ShapeDtypeStruct((N_HEADS, n_seq * t_new, KV_LORA_RANK), _F32),
        compiler_params=pltpu.CompilerParams(
            dimension_semantics=("arbitrary",), vmem_limit_bytes=VMEM_LIMIT),
        name="sample_attention",
    )(page_table.reshape(-1), qcat, c_new, kr_new, wukt, cache_lat, cache_rope_t)


def _merge_kernel(from_latent, a_ref, gate_ref, gout_ref, x_ref, wuv_ref, woa_ref, wout_ref, o_ref):
    if from_latent:
        heads = [_dot(a_ref[hd].astype(_BF16), wuv_ref[:, hd * V_HEAD_DIM:(hd + 1) * V_HEAD_DIM])
                 for hd in range(N_HEADS)]
        attn = jnp.concatenate(heads, axis=-1).astype(_BF16)
    else:
        attn = a_ref[...]
    merged = gate_ref[...].astype(_F32) * _dot(attn, woa_ref[...]) + gout_ref[...]
    o_ref[...] = x_ref[...] + _dot(merged.astype(_BF16), wout_ref[...])


def _merge(from_latent, attn, gate, gout, x, wuv, woa, wout):
    n = x.shape[0]
    tm = MERGE_TM
    tok = pl.BlockSpec((tm, D_MODEL), lambda i: (i, 0))
    if from_latent:
        a_spec = pl.BlockSpec((N_HEADS, tm, KV_LORA_RANK), lambda i: (0, i, 0))
    else:
        a_spec = tok
    return pl.pallas_call(
        functools.partial(_merge_kernel, from_latent),
        grid=(n // tm,),
        in_specs=[a_spec, tok, tok, tok, _const_spec(wuv.shape), _const_spec(woa.shape),
                  _const_spec(wout.shape)],
        out_specs=tok,
        out_shape=jax.ShapeDtypeStruct((n, D_MODEL), _F32),
        compiler_params=pltpu.CompilerParams(
            dimension_semantics=("parallel",), vmem_limit_bytes=VMEM_LIMIT),
        name="merge_sample" if from_latent else "merge_prompt",
    )(attn, gate, gout, x, wuv, woa, wout)


def _swap_halves(a, axis):
    lo, hi = jnp.split(a, 2, axis=axis)
    return jnp.concatenate([hi, lo], axis=axis)


def _slot(a):
    return jnp.concatenate([a, jnp.zeros_like(a)], axis=-1)


def _rope_tables(pos):
    half = QK_ROPE_DIM // 2
    inv = ROPE_BASE ** (-jnp.arange(half, dtype=_F32) / half)
    ang = pos.astype(_F32)[:, None] * inv[None, :]
    cos, sin = jnp.cos(ang), jnp.sin(ang)
    return _slot(jnp.concatenate([cos, cos], axis=-1)), _slot(jnp.concatenate([-sin, sin], axis=-1))


def _prep_weights(w_in, w_uq, w_uk, w_uv, w_o_gmlp, mix_norm, q_lora_norm, kv_lora_norm, q_nope_norm,
                  q_rope_norm, k_nope_norm, k_rope_norm, gmlp_v_norm):
    sizes = (Q_LORA_RANK, KV_LORA_RANK, QK_ROPE_DIM, GMLP_WIDTH, GMLP_WIDTH, D_MODEL, D_MODEL)
    c_q, c_kv, k_r, u, v, g_a, g_g = jnp.split(w_in, np.cumsum(sizes)[:-1].tolist(), axis=-1)
    w_in_p = jnp.concatenate([c_q, c_kv, u, v, g_a, g_g, _slot(k_r), _slot(_swap_halves(k_r, -1))], axis=-1)
    assert w_in_p.shape[-1] == _D_IN_PADDED
    wq = w_uq.reshape(Q_LORA_RANK, N_HEADS, QK_HEAD_DIM)
    wq_nope = wq[:, :, :QK_NOPE_DIM].reshape(Q_LORA_RANK, -1)
    wq_rope = wq[:, :, QK_NOPE_DIM:]
    w_uq_p = jnp.concatenate([wq_nope, _slot(wq_rope).reshape(Q_LORA_RANK, -1),
                              _slot(_swap_halves(wq_rope, -1)).reshape(Q_LORA_RANK, -1)], axis=-1)
    row = lambda a: a.reshape(1, -1).astype(_F32)
    return dict(
        mix_norm=row(mix_norm), w_in=w_in_p.astype(_BF16), q_lora_norm=row(q_lora_norm),
        w_uq=w_uq_p.astype(_BF16), kv_lora_norm=row(kv_lora_norm), w_uk=w_uk.astype(_BF16),
        w_uv=w_uv.astype(_BF16), q_nope_norm=row(q_nope_norm), q_rope_norm=row(_slot(q_rope_norm)),
        q_rope_norm_s=row(_slot(_swap_halves(q_rope_norm, -1))), k_nope_norm=row(k_nope_norm),
        k_rope_norm=row(_slot(k_rope_norm)), k_rope_norm_s=row(_slot(_swap_halves(k_rope_norm, -1))),
        gmlp_v_norm=row(gmlp_v_norm), w_o_gmlp=w_o_gmlp.astype(_BF16))


def kernel(x_prompt, x_sample, cache_kv_latent, cache_k_rope, page_table, ffn1_norm, ffn1_w_gate, ffn1_w_up, ffn1_w_down, mix_norm, w_in, q_lora_norm, w_uq, kv_lora_norm, w_uk, w_uv, q_nope_norm, q_rope_norm, k_nope_norm, k_rope_norm, gmlp_v_norm, gmlp_w_s, gmlp_b_s, w_o_attn, w_o_gmlp, w_out, ffn2_norm, ffn2_w_gate, ffn2_w_up, ffn2_w_down):
    batch, seq, _ = x_prompt.shape
    n_seq, t_new, _ = x_sample.shape
    assert ffn1_norm.shape[0] == 1
    past = page_table.shape[1] * PAGE_SIZE
    n_pool = cache_kv_latent.shape[1]
    l = 0
    row = lambda a: a.reshape(1, -1).astype(_F32)
    w = _prep_weights(w_in[l], w_uq[l], w_uk[l], w_uv[l], w_o_gmlp[l], mix_norm[l], q_lora_norm[l],
                      kv_lora_norm[l], q_nope_norm[l], q_rope_norm[l], k_nope_norm[l], k_rope_norm[l],
                      gmlp_v_norm[l])
    ffn1 = (row(ffn1_norm[l]), ffn1_w_gate[l].astype(_BF16), ffn1_w_up[l].astype(_BF16),
            ffn1_w_down[l].astype(_BF16))
    ffn2 = (row(ffn2_norm[l]), ffn2_w_gate[l].astype(_BF16), ffn2_w_up[l].astype(_BF16),
            ffn2_w_down[l].astype(_BF16))
    woa = w_o_attn[l].astype(_BF16)
    wout = w_out[l].astype(_BF16)

    wp = dict(w, gmlp_w_s=gmlp_w_s[l].astype(_F32), gmlp_b_s=gmlp_b_s[l].reshape(GMLP_GROUPS, CHUNK, 1))
    cos_p, sin_p = _rope_tables(jnp.arange(seq))
    xp = _ffn(x_prompt.reshape(batch * seq, D_MODEL), *ffn1)
    q, k, v, ckv_p, kr_p, gate_p, gout_p = _proj(False, xp, wp, cos_p, sin_p, seq // PROJ_TM)
    attn_p = _prompt_attention(q, k, v, batch, seq)
    xp = _merge(False, attn_p, gate_p, gout_p, xp, w["w_uv"], woa, wout)
    xp = _ffn(xp, *ffn2)

    reps = CHUNK // t_new
    ws_s = jax.vmap(lambda m: jnp.kron(jnp.eye(reps, dtype=_F32), m))(gmlp_w_s[l][:, :t_new, :t_new])
    bs_s = jnp.tile(gmlp_b_s[l][:, :t_new], (1, reps)).reshape(GMLP_GROUPS, CHUNK, 1)
    ws = dict(w, gmlp_w_s=ws_s, gmlp_b_s=bs_s)
    cos_s, sin_s = _rope_tables(past + jnp.arange(PROJ_TM) % t_new)
    xs = _ffn(x_sample.reshape(n_seq * t_new, D_MODEL), *ffn1)
    qcat, ckv_s, kr_s, vn_s, gate_s, gout_s = _proj(True, xs, ws, cos_s, sin_s, 1)
    o_lat = _sample_attention(page_table, qcat, ckv_s, kr_s, w["w_uk"].T,
                              cache_kv_latent.reshape(n_pool, PAGE_SIZE, KV_LORA_RANK),
                              jnp.swapaxes(cache_k_rope.reshape(n_pool, PAGE_SIZE, QK_ROPE_DIM), 1, 2),
                              t_new)
    xs = _merge(True, o_lat, gate_s, gout_s, xs, w["w_uv"], woa, wout)
    xs = _ffn(xs, *ffn2)

    return (xp.reshape(batch, seq, D_MODEL), xs.reshape(n_seq, t_new, D_MODEL),
            ckv_p.reshape(1, batch, seq, KV_LORA_RANK), kr_p.reshape(1, batch, seq, QK_ROPE_DIM),
            ckv_s.reshape(1, n_seq, t_new, KV_LORA_RANK), kr_s.reshape(1, n_seq, t_new, QK_ROPE_DIM),
            vn_s.reshape(1, n_seq, t_new, GMLP_WIDTH))
```

```python
import functools

import jax
import jax.numpy as jnp
import numpy as np
from jax import lax
from jax.experimental import pallas as pl
from jax.experimental.pallas import tpu as pltpu

D_MODEL = 1024
N_HEADS = 8
QK_NOPE_DIM = 128
QK_ROPE_DIM = 64
QK_HEAD_DIM = QK_NOPE_DIM + QK_ROPE_DIM
V_HEAD_DIM = 128
Q_LORA_RANK = 256
KV_LORA_RANK = 256
ROPE_BASE = 10000.0
ATTN_SCALE = QK_HEAD_DIM ** -0.5
LOG2E = 1.4426950408889634
GMLP_GROUPS = 4
GMLP_WIDTH = 1024
GMLP_GROUP_DIM = GMLP_WIDTH // GMLP_GROUPS
CHUNK = 128
PAGE_SIZE = 128
D_FF = 2816
EPS = 1e-6
NEG_INF = -1e30

LANE = 128
ROPE_SLOT = LANE

_OFF_CQ = 0
_OFF_CKV = _OFF_CQ + Q_LORA_RANK
_OFF_U = _OFF_CKV + KV_LORA_RANK
_OFF_V = _OFF_U + GMLP_WIDTH
_OFF_GA = _OFF_V + GMLP_WIDTH
_OFF_GG = _OFF_GA + D_MODEL
_OFF_KR = _OFF_GG + D_MODEL
_OFF_KRS = _OFF_KR + ROPE_SLOT
_D_IN_PADDED = _OFF_KRS + ROPE_SLOT

FFN_TM = 1024
FFN_TF = 256
PROJ_TM = 512
ATT_TQ = 512
ATT_TK = 512
ATT_HB = 2
ATT_ONES_ROWS = 16
MERGE_TM = 512
SATT_PAGES_PER_CHUNK = 16
SATT_LOOKAHEAD = 2
SATT_SLOTS = SATT_LOOKAHEAD + 2
VMEM_LIMIT = 56 * 1024 * 1024

_BF16 = jnp.bfloat16
_F32 = jnp.float32


def _dot(a, b):
    return jnp.dot(a, b, preferred_element_type=_F32)


def _dot_t(a, b):
    return lax.dot_general(a, b, (((1,), (1,)), ((), ())), preferred_element_type=_F32)


def _rms(x, n):
    return lax.rsqrt(jnp.sum(x * x, axis=-1, keepdims=True) * (1.0 / n) + EPS)


def _ffn_kernel(x_ref, g_ref, wg_ref, wu_ref, wd_ref, o_ref, h_scr, acc_scr):
    j = pl.program_id(1)

    @pl.when(j == 0)
    def _():
        x = x_ref[...]
        h_scr[...] = (x * _rms(x, D_MODEL) * g_ref[...]).astype(_BF16)
        acc_scr[...] = jnp.zeros_like(acc_scr)

    h = h_scr[...]
    gate = _dot(h, wg_ref[...])
    up = _dot(h, wu_ref[...])
    act = (gate * jax.nn.sigmoid(gate) * up).astype(_BF16)
    acc_scr[...] += _dot(act, wd_ref[...])

    @pl.when(j == pl.num_programs(1) - 1)
    def _():
        o_ref[...] = x_ref[...] + 0.5 * acc_scr[...]


def _ffn(x, norm_g, wg, wu, wd):
    n = x.shape[0]
    tm = min(FFN_TM, n)
    return pl.pallas_call(
        _ffn_kernel,
        grid=(n // tm, D_FF // FFN_TF),
        in_specs=[
            pl.BlockSpec((tm, D_MODEL), lambda i, j: (i, 0)),
            pl.BlockSpec((1, D_MODEL), lambda i, j: (0, 0)),
            pl.BlockSpec((D_MODEL, FFN_TF), lambda i, j: (0, j)),
            pl.BlockSpec((D_MODEL, FFN_TF), lambda i, j: (0, j)),
            pl.BlockSpec((FFN_TF, D_MODEL), lambda i, j: (j, 0)),
        ],
        out_specs=pl.BlockSpec((tm, D_MODEL), lambda i, j: (i, 0)),
        out_shape=jax.ShapeDtypeStruct((n, D_MODEL), _F32),
        scratch_shapes=[pltpu.VMEM((tm, D_MODEL), _BF16), pltpu.VMEM((tm, D_MODEL), _F32)],
        compiler_params=pltpu.CompilerParams(
            dimension_semantics=("parallel", "arbitrary"), vmem_limit_bytes=VMEM_LIMIT),
        name="ffn",
    )(x, norm_g, wg, wu, wd)


def _rope_slot(x, xs, g, gs, cos, sin):
    r = _rms(x, QK_ROPE_DIM)
    return (x * r * g) * cos + (xs * r * gs) * sin


def _gelu(x):
    return 0.5 * x * (1.0 + lax.erf(x * (2.0 ** -0.5)))


def _proj_kernel(sample, x_ref, mixg_ref, win_ref, qlg_ref, wuq_ref, kvg_ref, wuk_ref, wuv_ref,
                 qng_ref, qrg_ref, qrgs_ref, kng_ref, krg_ref, krgs_ref, vg_ref, ws_ref, bs_ref,
                 wog_ref, cos_ref, sin_ref, *rest):
    if sample:
        qcat_ref, ckv_ref, kr_ref, vn_ref, gate_ref, gout_ref, gm_scr = rest
    else:
        q_ref, k_ref, v_ref, ckv_ref, kr_ref, gate_ref, gout_ref, gm_scr = rest
    tm = x_ref.shape[0]
    x = x_ref[...]
    h = (x * _rms(x, D_MODEL) * mixg_ref[...]).astype(_BF16)
    cos = cos_ref[...]
    sin = sin_ref[...]

    def win(off, width):
        return _dot(h, win_ref[:, off:off + width])

    kr = _rope_slot(win(_OFF_KR, ROPE_SLOT), win(_OFF_KRS, ROPE_SLOT),
                    krg_ref[...], krgs_ref[...], cos, sin)[:, :QK_ROPE_DIM]
    kr_ref[...] = kr

    ckv = win(_OFF_CKV, KV_LORA_RANK)
    ckv = ckv * _rms(ckv, KV_LORA_RANK) * kvg_ref[...]
    ckv_ref[...] = ckv
    ckv_b = ckv.astype(_BF16)
    if not sample:
        kexp = _dot(ckv_b, wuk_ref[...])
        vals_t = _dot_t(wuv_ref[...], ckv_b)
        kr_b = kr.astype(_BF16)
        for hd in range(N_HEADS):
            sl = slice(hd * QK_NOPE_DIM, (hd + 1) * QK_NOPE_DIM)
            kh = kexp[:, sl]
            k_ref[hd, :, 0:QK_NOPE_DIM] = (kh * _rms(kh, QK_NOPE_DIM) * kng_ref[...]).astype(_BF16)
            k_ref[hd, :, QK_NOPE_DIM:QK_HEAD_DIM] = kr_b
            v_ref[hd] = vals_t[hd * V_HEAD_DIM:(hd + 1) * V_HEAD_DIM, :].astype(_BF16)

    cq = win(_OFF_CQ, Q_LORA_RANK)
    cq_b = (cq * _rms(cq, Q_LORA_RANK) * qlg_ref[...]).astype(_BF16)
    nq = N_HEADS * QK_NOPE_DIM
    nr = N_HEADS * ROPE_SLOT
    q_nope = _dot(cq_b, wuq_ref[:, 0:nq])
    q_rope = _dot(cq_b, wuq_ref[:, nq:nq + nr])
    q_rope_s = _dot(cq_b, wuq_ref[:, nq + nr:nq + 2 * nr])
    q_scale = ATTN_SCALE if sample else ATTN_SCALE * LOG2E
    for hd in range(N_HEADS):
        sl = slice(hd * QK_NOPE_DIM, (hd + 1) * QK_NOPE_DIM)
        qh = q_nope[:, sl]
        qh = qh * _rms(qh, QK_NOPE_DIM) * qng_ref[...] * q_scale
        rs = slice(hd * ROPE_SLOT, (hd + 1) * ROPE_SLOT)
        qr = _rope_slot(q_rope[:, rs], q_rope_s[:, rs], qrg_ref[...], qrgs_ref[...], cos, sin)
        qr = qr[:, :QK_ROPE_DIM] * q_scale
        if sample:
            qa = _dot_t((qh * kng_ref[...]).astype(_BF16), wuk_ref[:, sl])
            qcat_ref[hd, :, 0:KV_LORA_RANK] = qa
            qcat_ref[hd, :, KV_LORA_RANK:KV_LORA_RANK + QK_ROPE_DIM] = qr
        else:
            q_ref[hd, :, 0:QK_NOPE_DIM] = qh.astype(_BF16)
            q_ref[hd, :, QK_NOPE_DIM:QK_HEAD_DIM] = qr.astype(_BF16)

    u = _gelu(win(_OFF_U, GMLP_WIDTH))
    v = _gelu(win(_OFF_V, GMLP_WIDTH))
    row = lax.broadcasted_iota(jnp.int32, (CHUNK, CHUNK), 0)
    col = lax.broadcasted_iota(jnp.int32, (CHUNK, CHUNK), 1)
    for g in range(GMLP_GROUPS):
        gs = slice(g * GMLP_GROUP_DIM, (g + 1) * GMLP_GROUP_DIM)
        vg = v[:, gs]
        vg = vg * _rms(vg, GMLP_GROUP_DIM) * vg_ref[:, gs]
        if sample:
            vn_ref[:, gs] = vg
        vg_b = vg.astype(_BF16)
        w = jnp.where(col <= row, ws_ref[g], 0.0).astype(_BF16)
        for c in range(tm // CHUNK):
            cs = slice(c * CHUNK, (c + 1) * CHUNK)
            mix = _dot(w, vg_b[cs]) + bs_ref[g]
            gm_scr[cs, gs] = (u[cs, gs] * mix).astype(_BF16)
    gout_ref[...] = jax.nn.sigmoid(win(_OFF_GG, D_MODEL)) * _dot(gm_scr[...], wog_ref[...])
    gate_ref[...] = jax.nn.sigmoid(win(_OFF_GA, D_MODEL)).astype(_BF16)


def _const_spec(shape):
    nd = len(shape)
    return pl.BlockSpec(shape, lambda i: (0,) * nd, pipeline_mode=pl.Buffered(1))


def _proj(sample, x, w, cos, sin, n_pos_tiles):
    n = x.shape[0]
    tm = PROJ_TM
    tok = lambda width: pl.BlockSpec((tm, width), lambda i: (i, 0))
    heads = lambda width: pl.BlockSpec((N_HEADS, tm, width), lambda i: (0, i, 0))
    weights = [w["mix_norm"], w["w_in"], w["q_lora_norm"], w["w_uq"], w["kv_lora_norm"], w["w_uk"],
               w["w_uv_t"], w["q_nope_norm"], w["q_rope_norm"], w["q_rope_norm_s"], w["k_nope_norm"],
               w["k_rope_norm"], w["k_rope_norm_s"], w["gmlp_v_norm"], w["gmlp_w_s"], w["gmlp_b_s"],
               w["w_o_gmlp"]]
    pos_spec = pl.BlockSpec((tm, ROPE_SLOT), lambda i: (i % n_pos_tiles, 0))
    in_specs = [tok(D_MODEL)] + [_const_spec(a.shape) for a in weights] + [pos_spec, pos_spec]
    sds = jax.ShapeDtypeStruct
    if sample:
        out_shape = [sds((N_HEADS, n, KV_LORA_RANK + QK_ROPE_DIM), _F32), sds((n, KV_LORA_RANK), _F32),
                     sds((n, QK_ROPE_DIM), _F32), sds((n, GMLP_WIDTH), _F32),
                     sds((n, D_MODEL), _BF16), sds((n, D_MODEL), _F32)]
        out_specs = [heads(KV_LORA_RANK + QK_ROPE_DIM), tok(KV_LORA_RANK), tok(QK_ROPE_DIM),
                     tok(GMLP_WIDTH), tok(D_MODEL), tok(D_MODEL)]
    else:
        out_shape = [sds((N_HEADS, n, QK_HEAD_DIM), _BF16), sds((N_HEADS, n, QK_HEAD_DIM), _BF16),
                     sds((N_HEADS, V_HEAD_DIM, n), _BF16), sds((n, KV_LORA_RANK), _F32),
                     sds((n, QK_ROPE_DIM), _F32), sds((n, D_MODEL), _BF16), sds((n, D_MODEL), _F32)]
        vt_spec = pl.BlockSpec((N_HEADS, V_HEAD_DIM, tm), lambda i: (0, 0, i))
        out_specs = [heads(QK_HEAD_DIM), heads(QK_HEAD_DIM), vt_spec, tok(KV_LORA_RANK),
                     tok(QK_ROPE_DIM), tok(D_MODEL), tok(D_MODEL)]
    return pl.pallas_call(
        functools.partial(_proj_kernel, sample),
        grid=(n // tm,),
        in_specs=in_specs,
        out_specs=out_specs,
        out_shape=out_shape,
        scratch_shapes=[pltpu.VMEM((tm, GMLP_WIDTH), _BF16)],
        compiler_params=pltpu.CompilerParams(
            dimension_semantics=("parallel",), vmem_limit_bytes=VMEM_LIMIT),
        name="proj_sample" if sample else "proj_prompt",
    )(x, *weights, cos, sin)


def _softmax_step(s, v_b, m_scr, l_scr, acc_scr):
    m_prev = m_scr[...]
    m_new = jnp.maximum(m_prev, jnp.max(s, axis=-1, keepdims=True))
    alpha = jnp.exp(m_prev - m_new)
    p = jnp.exp(s - m_new)
    l_scr[...] = alpha * l_scr[...] + jnp.sum(p, axis=-1, keepdims=True)
    acc_scr[...] = alpha * acc_scr[...] + _dot(p.astype(_BF16), v_b)
    m_scr[...] = m_new


def _pattn_kernel(q_ref, k_ref, vt_ref, o_ref, sa_scr, sb_scr, m_scr, acc_scr):
    qi = pl.program_id(2)
    m_scr[...] = jnp.full_like(m_scr, NEG_INF)
    acc_scr[...] = jnp.zeros_like(acc_scr)

    def scores(kb, buf, masked):
        off = pl.multiple_of(kb * ATT_TK, ATT_TK)
        for hh in range(ATT_HB):
            s = _dot_t(k_ref[hh, pl.ds(off, ATT_TK), :], q_ref[hh])
            if masked:
                key = lax.broadcasted_iota(jnp.int32, s.shape, 0)
                qry = lax.broadcasted_iota(jnp.int32, s.shape, 1)
                s = jnp.where(key <= qry, s, NEG_INF)
            buf[hh] = s

    def update(kb, buf):
        off = pl.multiple_of(kb * ATT_TK, ATT_TK)
        for hh in range(ATT_HB):
            s = buf[hh]
            m_prev = m_scr[hh]
            m_new = jnp.maximum(m_prev, jnp.max(s, axis=0, keepdims=True))
            p = jnp.exp2(s - m_new).astype(_BF16)
            vt = vt_ref[hh, :, pl.ds(off, ATT_TK)]
            vt1 = jnp.concatenate([vt, jnp.ones((ATT_ONES_ROWS, ATT_TK), _BF16)], axis=0)
            acc_scr[hh] = jnp.exp2(m_prev - m_new) * acc_scr[hh] + _dot(vt1, p)
            m_scr[hh] = m_new

    def trip(t, ybuf, xbuf, masked):
        scores(t, xbuf, masked)
        update(t - 1, ybuf)

    @pl.when(qi == 0)
    def _():
        scores(0, sa_scr, True)
        update(0, sa_scr)

    @pl.when(qi > 0)
    def _():
        scores(0, sa_scr, False)

        def pair(j, carry):
            trip(2 * j + 1, sa_scr, sb_scr, False)
            trip(2 * j + 2, sb_scr, sa_scr, False)
            return carry

        lax.fori_loop(0, lax.shift_right_logical(qi - 1, 1), pair, 0)

        @pl.when((qi & 1) == 1)
        def _():
            trip(qi, sa_scr, sb_scr, True)
            update(qi, sb_scr)

        @pl.when((qi & 1) == 0)
        def _():
            trip(qi - 1, sa_scr, sb_scr, False)
            trip(qi, sb_scr, sa_scr, True)
            update(qi, sa_scr)

    for hh in range(ATT_HB):
        acc = acc_scr[hh]
        o_t = acc[:V_HEAD_DIM] / acc[V_HEAD_DIM:V_HEAD_DIM + 1]
        o_ref[:, hh * V_HEAD_DIM:(hh + 1) * V_HEAD_DIM] = o_t.T.astype(o_ref.dtype)


def _prompt_attention(q, k, vt, batch, seq):
    assert ATT_TQ == ATT_TK
    nq = seq // ATT_TQ
    hb = ATT_HB
    return pl.pallas_call(
        _pattn_kernel,
        grid=(batch, N_HEADS // hb, nq),
        in_specs=[
            pl.BlockSpec((hb, ATT_TQ, QK_HEAD_DIM), lambda b, h, i: (h, b * nq + i, 0)),
            pl.BlockSpec((hb, seq, QK_HEAD_DIM), lambda b, h, i: (h, b, 0)),
            pl.BlockSpec((hb, V_HEAD_DIM, seq), lambda b, h, i: (h, 0, b)),
        ],
        out_specs=pl.BlockSpec((ATT_TQ, hb * V_HEAD_DIM), lambda b, h, i: (b * nq + i, h)),
        out_shape=jax.ShapeDtypeStruct((batch * seq, N_HEADS * V_HEAD_DIM), _BF16),
        scratch_shapes=[pltpu.VMEM((hb, ATT_TK, ATT_TQ), _F32),
                        pltpu.VMEM((hb, ATT_TK, ATT_TQ), _F32),
                        pltpu.VMEM((hb, 1, ATT_TQ), _F32),
                        pltpu.VMEM((hb, V_HEAD_DIM + ATT_ONES_ROWS, ATT_TQ), _F32)],
        compiler_params=pltpu.CompilerParams(
            dimension_semantics=("parallel", "parallel", "arbitrary"), vmem_limit_bytes=VMEM_LIMIT),
        name="prompt_attention",
    )(q, k, vt)


def _sattn_kernel(n_pages, pt_ref, qcat_ref, cnew_ref, krnew_ref, wukt_ref, lat_hbm, ropet_hbm, o_ref,
                  a_scr, lat_buf, ropet_buf, sca_scr, scb_scr, scl_scr, sem_lat, sem_rope, m_scr, l_scr,
                  acc_scr):
    pages = SATT_PAGES_PER_CHUNK
    n_chunks = n_pages // pages
    nk = N_HEADS * QK_NOPE_DIM
    nrow = N_HEADS * 8
    s = pl.program_id(0)
    total = pl.num_programs(0) * n_chunks
    base = s * n_chunks

    def copies(item, slot):
        out = []
        for g in range(pages):
            page = pt_ref[item * pages + g]
            dst = pl.ds(g * PAGE_SIZE, PAGE_SIZE)
            out.append(pltpu.make_async_copy(lat_hbm.at[page], lat_buf.at[slot, dst], sem_lat.at[slot]))
            out.append(pltpu.make_async_copy(ropet_hbm.at[page], ropet_buf.at[slot, :, dst], sem_rope.at[slot]))
        return out

    @pl.when(s == 0)
    def _():
        a_scr[0:nk, :] = wukt_ref[...]
        for t in range(SATT_LOOKAHEAD):
            for d in copies(t, t % SATT_SLOTS):
                d.start()

    q2 = qcat_ref[...].reshape(nrow, KV_LORA_RANK + QK_ROPE_DIM)
    a_scr[nk:nk + nrow, :] = q2[:, :KV_LORA_RANK].astype(_BF16)
    q_rope = q2[:, KV_LORA_RANK:].astype(_BF16)
    m_scr[...] = jnp.full_like(m_scr, NEG_INF)
    l_scr[...] = jnp.zeros_like(l_scr)
    acc_scr[...] = jnp.zeros_like(acc_scr)

    def scores(c_b, s_rope):
        big = _dot_t(a_scr[...], c_b)
        rows = []
        for hd in range(N_HEADS):
            kx = big[hd * QK_NOPE_DIM:(hd + 1) * QK_NOPE_DIM, :]
            r = lax.rsqrt(jnp.sum(kx * kx, axis=0, keepdims=True) * (1.0 / QK_NOPE_DIM) + EPS)
            rows.append(big[nk + 8 * hd:nk + 8 * hd + 8, :] * r + s_rope[8 * hd:8 * hd + 8, :])
        return jnp.concatenate(rows, axis=0)

    def slot_of(c):
        return lax.rem(base + c, SATT_SLOTS)

    def stage_dma(c):
        t = base + c

        @pl.when(t + SATT_LOOKAHEAD < total)
        def _():
            for d in copies(t + SATT_LOOKAHEAD, lax.rem(t + SATT_LOOKAHEAD, SATT_SLOTS)):
                d.start()

        for d in copies(t, slot_of(c)):
            d.wait()

    sc_bufs = [sca_scr, scb_scr] * (n_chunks // 2 + 1)
    sc_bufs = sc_bufs[:n_chunks - 1] + [scl_scr]
    tk = pages * PAGE_SIZE
    pad = PAGE_SIZE - cnew_ref.shape[0]

    def latent(c):
        c_b = lat_buf[slot_of(c)].astype(_BF16)
        if c < n_chunks - 1:
            return c_b
        c_new = jnp.concatenate([cnew_ref[...], jnp.zeros((pad, KV_LORA_RANK), _F32)], axis=0)
        return jnp.concatenate([c_b, c_new.astype(_BF16)], axis=0)

    def stage_scores(c):
        s_rope = _dot(q_rope, ropet_buf[slot_of(c)].astype(_BF16))
        if c < n_chunks - 1:
            sc_bufs[c][...] = scores(latent(c), s_rope)
        else:
            kr_new = jnp.concatenate([krnew_ref[...], jnp.zeros((pad, QK_ROPE_DIM), _F32)], axis=0)
            s_rope = jnp.concatenate([s_rope, _dot_t(q_rope, kr_new.astype(_BF16))], axis=1)
            row = lax.broadcasted_iota(jnp.int32, (nrow, tk + PAGE_SIZE), 0)
            col = lax.broadcasted_iota(jnp.int32, (nrow, tk + PAGE_SIZE), 1)
            sc_bufs[c][...] = jnp.where(col - tk <= (row & 7), scores(latent(c), s_rope), NEG_INF)

    def stage_update(c):
        _softmax_step(sc_bufs[c][...], latent(c), m_scr, l_scr, acc_scr)

    stage_dma(0)
    stage_scores(0)
    for c in range(1, n_chunks):
        stage_dma(c)
        stage_scores(c)
        stage_update(c - 1)
    stage_update(n_chunks - 1)

    o_ref[...] = (acc_scr[...] / l_scr[...]).reshape(o_ref.shape)


def _sample_attention(page_table, qcat, c_new, kr_new, wukt, cache_lat, cache_rope_t, t_new):
    n_seq, n_pages = page_table.shape
    assert t_new == 8 and n_pages % SATT_PAGES_PER_CHUNK == 0
    assert n_seq * (n_pages // SATT_PAGES_PER_CHUNK) >= SATT_LOOKAHEAD
    tk = SATT_PAGES_PER_CHUNK * PAGE_SIZE
    nrow = N_HEADS * t_new
    dq = KV_LORA_RANK + QK_ROPE_DIM
    grid_spec = pltpu.PrefetchScalarGridSpec(
        num_scalar_prefetch=1,
        grid=(n_seq,),
        in_specs=[
            pl.BlockSpec((N_HEADS, t_new, dq), lambda s, pt: (0, s, 0)),
            pl.BlockSpec((t_new, KV_LORA_RANK), lambda s, pt: (s, 0)),
            pl.BlockSpec((t_new, QK_ROPE_DIM), lambda s, pt: (s, 0)),
            pl.BlockSpec((N_HEADS * QK_NOPE_DIM, KV_LORA_RANK), lambda s, pt: (0, 0)),
            pl.BlockSpec(memory_space=pl.ANY),
            pl.BlockSpec(memory_space=pl.ANY),
        ],
        out_specs=pl.BlockSpec((N_HEADS, t_new, KV_LORA_RANK), lambda s, pt: (0, s, 0)),
        scratch_shapes=[
            pltpu.VMEM((N_HEADS * QK_NOPE_DIM + nrow, KV_LORA_RANK), _BF16),
            pltpu.VMEM((SATT_SLOTS, tk, KV_LORA_RANK), _F32),
            pltpu.VMEM((SATT_SLOTS, QK_ROPE_DIM, tk), _F32),
            pltpu.VMEM((nrow, tk), _F32),
            pltpu.VMEM((nrow, tk), _F32),
            pltpu.VMEM((nrow, tk + PAGE_SIZE), _F32),
            pltpu.SemaphoreType.DMA((SATT_SLOTS,)),
            pltpu.SemaphoreType.DMA((SATT_SLOTS,)),
            pltpu.VMEM((nrow, 1), _F32),
            pltpu.VMEM((nrow, 1), _F32),
            pltpu.VMEM((nrow, KV_LORA_RANK), _F32),
        ],
    )
    return pl.pallas_call(
        functools.partial(_sattn_kernel, n_pages),
        grid_spec=grid_spec,
        out_shape=jax.ShapeDtypeStruct((N_HEADS, n_seq * t_new, KV_LORA_RANK), _F32),
        compiler_params=pltpu.CompilerParams(
            dimension_semantics=("arbitrary",), vmem_limit_bytes=VMEM_LIMIT),
        name="sample_attention",
    )(page_table.reshape(-1), qcat, c_new, kr_new, wukt, cache_lat, cache_rope_t)


def _merge_kernel(from_latent, a_ref, gate_ref, gout_ref, x_ref, wuv_ref, woa_ref, wout_ref, o_ref):
    if from_latent:
        heads = [_dot(a_ref[hd].astype(_BF16), wuv_ref[:, hd * V_HEAD_DIM:(hd + 1) * V_HEAD_DIM])
                 for hd in range(N_HEADS)]
        attn = jnp.concatenate(heads, axis=-1).astype(_BF16)
    else:
        attn = a_ref[...]
    merged = gate_ref[...].astype(_F32) * _dot(attn, woa_ref[...]) + gout_ref[...]
    o_ref[...] = x_ref[...] + _dot(merged.astype(_BF16), wout_ref[...])


def _merge(from_latent, attn, gate, gout, x, wuv, woa, wout):
    n = x.shape[0]
    tm = MERGE_TM
    tok = pl.BlockSpec((tm, D_MODEL), lambda i: (i, 0))
    if from_latent:
        a_spec = pl.BlockSpec((N_HEADS, tm, KV_LORA_RANK), lambda i: (0, i, 0))
    else:
        a_spec = tok
    return pl.pallas_call(
        functools.partial(_merge_kernel, from_latent),
        grid=(n // tm,),
        in_specs=[a_spec, tok, tok, tok, _const_spec(wuv.shape), _const_spec(woa.shape),
                  _const_spec(wout.shape)],
        out_specs=tok,
        out_shape=jax.ShapeDtypeStruct((n, D_MODEL), _F32),
        compiler_params=pltpu.CompilerParams(
            dimension_semantics=("parallel",), vmem_limit_bytes=VMEM_LIMIT),
        name="merge_sample" if from_latent else "merge_prompt",
    )(attn, gate, gout, x, wuv, woa, wout)


def _swap_halves(a, axis):
    lo, hi = jnp.split(a, 2, axis=axis)
    return jnp.concatenate([hi, lo], axis=axis)


def _slot(a):
    return jnp.concatenate([a, jnp.zeros_like(a)], axis=-1)


def _rope_tables(pos):
    half = QK_ROPE_DIM // 2
    inv = ROPE_BASE ** (-jnp.arange(half, dtype=_F32) / half)
    ang = pos.astype(_F32)[:, None] * inv[None, :]
    cos, sin = jnp.cos(ang), jnp.sin(ang)
    return _slot(jnp.concatenate([cos, cos], axis=-1)), _slot(jnp.concatenate([-sin, sin], axis=-1))


def _prep_weights(w_in, w_uq, w_uk, w_uv, w_o_gmlp, mix_norm, q_lora_norm, kv_lora_norm, q_nope_norm,
                  q_rope_norm, k_nope_norm, k_rope_norm, gmlp_v_norm):
    sizes = (Q_LORA_RANK, KV_LORA_RANK, QK_ROPE_DIM, GMLP_WIDTH, GMLP_WIDTH, D_MODEL, D_MODEL)
    c_q, c_kv, k_r, u, v, g_a, g_g = jnp.split(w_in, np.cumsum(sizes)[:-1].tolist(), axis=-1)
    w_in_p = jnp.concatenate([c_q, c_kv, u, v, g_a, g_g, _slot(k_r), _slot(_swap_halves(k_r, -1))], axis=-1)
    assert w_in_p.shape[-1] == _D_IN_PADDED
    wq = w_uq.reshape(Q_LORA_RANK, N_HEADS, QK_HEAD_DIM)
    wq_nope = wq[:, :, :QK_NOPE_DIM].reshape(Q_LORA_RANK, -1)
    wq_rope = wq[:, :, QK_NOPE_DIM:]
    w_uq_p = jnp.concatenate([wq_nope, _slot(wq_rope).reshape(Q_LORA_RANK, -1),
                              _slot(_swap_halves(wq_rope, -1)).reshape(Q_LORA_RANK, -1)], axis=-1)
    row = lambda a: a.reshape(1, -1).astype(_F32)
    return dict(
        mix_norm=row(mix_norm), w_in=w_in_p.astype(_BF16), q_lora_norm=row(q_lora_norm),
        w_uq=w_uq_p.astype(_BF16), kv_lora_norm=row(kv_lora_norm), w_uk=w_uk.astype(_BF16),
        w_uv=w_uv.astype(_BF16), w_uv_t=w_uv.T.astype(_BF16), q_nope_norm=row(q_nope_norm), q_rope_norm=row(_slot(q_rope_norm)),
        q_rope_norm_s=row(_slot(_swap_halves(q_rope_norm, -1))), k_nope_norm=row(k_nope_norm),
        k_rope_norm=row(_slot(k_rope_norm)), k_rope_norm_s=row(_slot(_swap_halves(k_rope_norm, -1))),
        gmlp_v_norm=row(gmlp_v_norm), w_o_gmlp=w_o_gmlp.astype(_BF16))


def kernel(x_prompt, x_sample, cache_kv_latent, cache_k_rope, page_table, ffn1_norm, ffn1_w_gate, ffn1_w_up, ffn1_w_down, mix_norm, w_in, q_lora_norm, w_uq, kv_lora_norm, w_uk, w_uv, q_nope_norm, q_rope_norm, k_nope_norm, k_rope_norm, gmlp_v_norm, gmlp_w_s, gmlp_b_s, w_o_attn, w_o_gmlp, w_out, ffn2_norm, ffn2_w_gate, ffn2_w_up, ffn2_w_down):
    batch, seq, _ = x_prompt.shape
    n_seq, t_new, _ = x_sample.shape
    assert ffn1_norm.shape[0] == 1
    past = page_table.shape[1] * PAGE_SIZE
    n_pool = cache_kv_latent.shape[1]
    l = 0
    row = lambda a: a.reshape(1, -1).astype(_F32)
    w = _prep_weights(w_in[l], w_uq[l], w_uk[l], w_uv[l], w_o_gmlp[l], mix_norm[l], q_lora_norm[l],
                      kv_lora_norm[l], q_nope_norm[l], q_rope_norm[l], k_nope_norm[l], k_rope_norm[l],
                      gmlp_v_norm[l])
    ffn1 = (row(ffn1_norm[l]), ffn1_w_gate[l].astype(_BF16), ffn1_w_up[l].astype(_BF16),
            ffn1_w_down[l].astype(_BF16))
    ffn2 = (row(ffn2_norm[l]), ffn2_w_gate[l].astype(_BF16), ffn2_w_up[l].astype(_BF16),
            ffn2_w_down[l].astype(_BF16))
    woa = w_o_attn[l].astype(_BF16)
    wout = w_out[l].astype(_BF16)

    wp = dict(w, gmlp_w_s=gmlp_w_s[l].astype(_F32), gmlp_b_s=gmlp_b_s[l].reshape(GMLP_GROUPS, CHUNK, 1))
    cos_p, sin_p = _rope_tables(jnp.arange(seq))
    xp = _ffn(x_prompt.reshape(batch * seq, D_MODEL), *ffn1)
    q, k, v, ckv_p, kr_p, gate_p, gout_p = _proj(False, xp, wp, cos_p, sin_p, seq // PROJ_TM)
    attn_p = _prompt_attention(q, k, v, batch, seq)
    xp = _merge(False, attn_p, gate_p, gout_p, xp, w["w_uv"], woa, wout)
    xp = _ffn(xp, *ffn2)

    reps = CHUNK // t_new
    ws_s = jax.vmap(lambda m: jnp.kron(jnp.eye(reps, dtype=_F32), m))(gmlp_w_s[l][:, :t_new, :t_new])
    bs_s = jnp.tile(gmlp_b_s[l][:, :t_new], (1, reps)).reshape(GMLP_GROUPS, CHUNK, 1)
    ws = dict(w, gmlp_w_s=ws_s, gmlp_b_s=bs_s)
    cos_s, sin_s = _rope_tables(past + jnp.arange(PROJ_TM) % t_new)
    xs = _ffn(x_sample.reshape(n_seq * t_new, D_MODEL), *ffn1)
    qcat, ckv_s, kr_s, vn_s, gate_s, gout_s = _proj(True, xs, ws, cos_s, sin_s, 1)
    o_lat = _sample_attention(page_table, qcat, ckv_s, kr_s, w["w_uk"].T,
                              cache_kv_latent.reshape(n_pool, PAGE_SIZE, KV_LORA_RANK),
                              jnp.swapaxes(cache_k_rope.reshape(n_pool, PAGE_SIZE, QK_ROPE_DIM), 1, 2),
                              t_new)
    xs = _merge(True, o_lat, gate_s, gout_s, xs, w["w_uv"], woa, wout)
    xs = _ffn(xs, *ffn2)

    return (xp.reshape(batch, seq, D_MODEL), xs.reshape(n_seq, t_new, D_MODEL),
            ckv_p.reshape(1, batch, seq, KV_LORA_RANK), kr_p.reshape(1, batch, seq, QK_ROPE_DIM),
            ckv_s.reshape(1, n_seq, t_new, KV_LORA_RANK), kr_s.reshape(1, n_seq, t_new, QK_ROPE_DIM),
            vn_s.reshape(1, n_seq, t_new, GMLP_WIDTH))
```

```python
import functools

import jax
import jax.numpy as jnp
import numpy as np
from jax import lax
from jax.experimental import pallas as pl
from jax.experimental.pallas import tpu as pltpu

D_MODEL = 1024
N_HEADS = 8
QK_NOPE_DIM = 128
QK_ROPE_DIM = 64
QK_HEAD_DIM = QK_NOPE_DIM + QK_ROPE_DIM
V_HEAD_DIM = 128
Q_LORA_RANK = 256
KV_LORA_RANK = 256
ROPE_BASE = 10000.0
ATTN_SCALE = QK_HEAD_DIM ** -0.5
LOG2E = 1.4426950408889634
GMLP_GROUPS = 4
GMLP_WIDTH = 1024
GMLP_GROUP_DIM = GMLP_WIDTH // GMLP_GROUPS
CHUNK = 128
PAGE_SIZE = 128
D_FF = 2816
EPS = 1e-6
NEG_INF = -1e30

LANE = 128
ROPE_SLOT = LANE

_OFF_CQ = 0
_OFF_CKV = _OFF_CQ + Q_LORA_RANK
_OFF_U = _OFF_CKV + KV_LORA_RANK
_OFF_V = _OFF_U + GMLP_WIDTH
_OFF_GA = _OFF_V + GMLP_WIDTH
_OFF_GG = _OFF_GA + D_MODEL
_OFF_KR = _OFF_GG + D_MODEL
_OFF_KRS = _OFF_KR + ROPE_SLOT
_D_IN_PADDED = _OFF_KRS + ROPE_SLOT

FFN_TM = 1024
FFN_TF = 256
PROJ_TM = 512
ATT_TQ = 512
ATT_TK = 512
ATT_HB = 2
ATT_ONES_ROWS = 16
MERGE_TM = 512
SATT_PAGES_PER_CHUNK = 16
VMEM_LIMIT = 56 * 1024 * 1024

_BF16 = jnp.bfloat16
_F32 = jnp.float32


def _dot(a, b):
    return jnp.dot(a, b, preferred_element_type=_F32)


def _dot_t(a, b):
    return lax.dot_general(a, b, (((1,), (1,)), ((), ())), preferred_element_type=_F32)


def _rms(x, n):
    return lax.rsqrt(jnp.sum(x * x, axis=-1, keepdims=True) * (1.0 / n) + EPS)


def _ffn_kernel(x_ref, g_ref, wg_ref, wu_ref, wd_ref, o_ref, h_scr, acc_scr):
    j = pl.program_id(1)

    @pl.when(j == 0)
    def _():
        x = x_ref[...]
        h_scr[...] = (x * _rms(x, D_MODEL) * g_ref[...]).astype(_BF16)
        acc_scr[...] = jnp.zeros_like(acc_scr)

    h = h_scr[...]
    gate = _dot(h, wg_ref[...])
    up = _dot(h, wu_ref[...])
    act = (gate * jax.nn.sigmoid(gate) * up).astype(_BF16)
    acc_scr[...] += _dot(act, wd_ref[...])

    @pl.when(j == pl.num_programs(1) - 1)
    def _():
        o_ref[...] = x_ref[...] + 0.5 * acc_scr[...]


def _ffn(x, norm_g, wg, wu, wd):
    n = x.shape[0]
    tm = min(FFN_TM, n)
    return pl.pallas_call(
        _ffn_kernel,
        grid=(n // tm, D_FF // FFN_TF),
        in_specs=[
            pl.BlockSpec((tm, D_MODEL), lambda i, j: (i, 0)),
            pl.BlockSpec((1, D_MODEL), lambda i, j: (0, 0)),
            pl.BlockSpec((D_MODEL, FFN_TF), lambda i, j: (0, j)),
            pl.BlockSpec((D_MODEL, FFN_TF), lambda i, j: (0, j)),
            pl.BlockSpec((FFN_TF, D_MODEL), lambda i, j: (j, 0)),
        ],
        out_specs=pl.BlockSpec((tm, D_MODEL), lambda i, j: (i, 0)),
        out_shape=jax.ShapeDtypeStruct((n, D_MODEL), _F32),
        scratch_shapes=[pltpu.VMEM((tm, D_MODEL), _BF16), pltpu.VMEM((tm, D_MODEL), _F32)],
        compiler_params=pltpu.CompilerParams(
            dimension_semantics=("parallel", "arbitrary"), vmem_limit_bytes=VMEM_LIMIT),
        name="ffn",
    )(x, norm_g, wg, wu, wd)


def _rope_slot(x, xs, g, gs, cos, sin):
    r = _rms(x, QK_ROPE_DIM)
    return (x * r * g) * cos + (xs * r * gs) * sin


def _gelu(x):
    return 0.5 * x * (1.0 + lax.erf(x * (2.0 ** -0.5)))


def _proj_kernel(sample, x_ref, mixg_ref, win_ref, qlg_ref, wuq_ref, kvg_ref, wuk_ref, wuv_ref,
                 qng_ref, qrg_ref, qrgs_ref, kng_ref, krg_ref, krgs_ref, vg_ref, ws_ref, bs_ref,
                 wog_ref, cos_ref, sin_ref, *rest):
    if sample:
        qcat_ref, ckv_ref, kr_ref, vn_ref, gate_ref, gout_ref, gm_scr = rest
    else:
        q_ref, k_ref, v_ref, ckv_ref, kr_ref, gate_ref, gout_ref, gm_scr = rest
    tm = x_ref.shape[0]
    x = x_ref[...]
    h = (x * _rms(x, D_MODEL) * mixg_ref[...]).astype(_BF16)
    cos = cos_ref[...]
    sin = sin_ref[...]

    def win(off, width):
        return _dot(h, win_ref[:, off:off + width])

    kr = _rope_slot(win(_OFF_KR, ROPE_SLOT), win(_OFF_KRS, ROPE_SLOT),
                    krg_ref[...], krgs_ref[...], cos, sin)[:, :QK_ROPE_DIM]
    kr_ref[...] = kr

    ckv = win(_OFF_CKV, KV_LORA_RANK)
    ckv = ckv * _rms(ckv, KV_LORA_RANK) * kvg_ref[...]
    ckv_ref[...] = ckv
    ckv_b = ckv.astype(_BF16)
    if not sample:
        kexp = _dot(ckv_b, wuk_ref[...])
        vals_t = _dot_t(wuv_ref[...], ckv_b)
        kr_b = kr.astype(_BF16)
        for hd in range(N_HEADS):
            sl = slice(hd * QK_NOPE_DIM, (hd + 1) * QK_NOPE_DIM)
            kh = kexp[:, sl]
            k_ref[hd, :, 0:QK_NOPE_DIM] = (kh * _rms(kh, QK_NOPE_DIM) * kng_ref[...]).astype(_BF16)
            k_ref[hd, :, QK_NOPE_DIM:QK_HEAD_DIM] = kr_b
            v_ref[hd] = vals_t[hd * V_HEAD_DIM:(hd + 1) * V_HEAD_DIM, :].astype(_BF16)

    cq = win(_OFF_CQ, Q_LORA_RANK)
    cq_b = (cq * _rms(cq, Q_LORA_RANK) * qlg_ref[...]).astype(_BF16)
    nq = N_HEADS * QK_NOPE_DIM
    nr = N_HEADS * ROPE_SLOT
    q_nope = _dot(cq_b, wuq_ref[:, 0:nq])
    q_rope = _dot(cq_b, wuq_ref[:, nq:nq + nr])
    q_rope_s = _dot(cq_b, wuq_ref[:, nq + nr:nq + 2 * nr])
    q_scale = ATTN_SCALE if sample else ATTN_SCALE * LOG2E
    for hd in range(N_HEADS):
        sl = slice(hd * QK_NOPE_DIM, (hd + 1) * QK_NOPE_DIM)
        qh = q_nope[:, sl]
        qh = qh * _rms(qh, QK_NOPE_DIM) * qng_ref[...] * q_scale
        rs = slice(hd * ROPE_SLOT, (hd + 1) * ROPE_SLOT)
        qr = _rope_slot(q_rope[:, rs], q_rope_s[:, rs], qrg_ref[...], qrgs_ref[...], cos, sin)
        qr = qr[:, :QK_ROPE_DIM] * q_scale
        if sample:
            qa = _dot_t((qh * kng_ref[...]).astype(_BF16), wuk_ref[:, sl])
            qcat_ref[hd, :, 0:KV_LORA_RANK] = qa
            qcat_ref[hd, :, KV_LORA_RANK:KV_LORA_RANK + QK_ROPE_DIM] = qr
        else:
            q_ref[hd, :, 0:QK_NOPE_DIM] = qh.astype(_BF16)
            q_ref[hd, :, QK_NOPE_DIM:QK_HEAD_DIM] = qr.astype(_BF16)

    u = _gelu(win(_OFF_U, GMLP_WIDTH))
    v = _gelu(win(_OFF_V, GMLP_WIDTH))
    row = lax.broadcasted_iota(jnp.int32, (CHUNK, CHUNK), 0)
    col = lax.broadcasted_iota(jnp.int32, (CHUNK, CHUNK), 1)
    for g in range(GMLP_GROUPS):
        gs = slice(g * GMLP_GROUP_DIM, (g + 1) * GMLP_GROUP_DIM)
        vg = v[:, gs]
        vg = vg * _rms(vg, GMLP_GROUP_DIM) * vg_ref[:, gs]
        if sample:
            vn_ref[:, gs] = vg
        vg_b = vg.astype(_BF16)
        w = jnp.where(col <= row, ws_ref[g], 0.0).astype(_BF16)
        for c in range(tm // CHUNK):
            cs = slice(c * CHUNK, (c + 1) * CHUNK)
            mix = _dot(w, vg_b[cs]) + bs_ref[g]
            gm_scr[cs, gs] = (u[cs, gs] * mix).astype(_BF16)
    gout_ref[...] = jax.nn.sigmoid(win(_OFF_GG, D_MODEL)) * _dot(gm_scr[...], wog_ref[...])
    gate_ref[...] = jax.nn.sigmoid(win(_OFF_GA, D_MODEL)).astype(_BF16)


def _const_spec(shape):
    nd = len(shape)
    return pl.BlockSpec(shape, lambda i: (0,) * nd, pipeline_mode=pl.Buffered(1))


def _proj(sample, x, w, cos, sin, n_pos_tiles):
    n = x.shape[0]
    tm = PROJ_TM
    tok = lambda width: pl.BlockSpec((tm, width), lambda i: (i, 0))
    heads = lambda width: pl.BlockSpec((N_HEADS, tm, width), lambda i: (0, i, 0))
    weights = [w["mix_norm"], w["w_in"], w["q_lora_norm"], w["w_uq"], w["kv_lora_norm"], w["w_uk"],
               w["w_uv_t"], w["q_nope_norm"], w["q_rope_norm"], w["q_rope_norm_s"], w["k_nope_norm"],
               w["k_rope_norm"], w["k_rope_norm_s"], w["gmlp_v_norm"], w["gmlp_w_s"], w["gmlp_b_s"],
               w["w_o_gmlp"]]
    pos_spec = pl.BlockSpec((tm, ROPE_SLOT), lambda i: (i % n_pos_tiles, 0))
    in_specs = [tok(D_MODEL)] + [_const_spec(a.shape) for a in weights] + [pos_spec, pos_spec]
    sds = jax.ShapeDtypeStruct
    if sample:
        out_shape = [sds((N_HEADS, n, KV_LORA_RANK + QK_ROPE_DIM), _F32), sds((n, KV_LORA_RANK), _F32),
                     sds((n, QK_ROPE_DIM), _F32), sds((n, GMLP_WIDTH), _F32),
                     sds((n, D_MODEL), _BF16), sds((n, D_MODEL), _F32)]
        out_specs = [heads(KV_LORA_RANK + QK_ROPE_DIM), tok(KV_LORA_RANK), tok(QK_ROPE_DIM),
                     tok(GMLP_WIDTH), tok(D_MODEL), tok(D_MODEL)]
    else:
        out_shape = [sds((N_HEADS, n, QK_HEAD_DIM), _BF16), sds((N_HEADS, n, QK_HEAD_DIM), _BF16),
                     sds((N_HEADS, V_HEAD_DIM, n), _BF16), sds((n, KV_LORA_RANK), _F32),
                     sds((n, QK_ROPE_DIM), _F32), sds((n, D_MODEL), _BF16), sds((n, D_MODEL), _F32)]
        vt_spec = pl.BlockSpec((N_HEADS, V_HEAD_DIM, tm), lambda i: (0, 0, i))
        out_specs = [heads(QK_HEAD_DIM), heads(QK_HEAD_DIM), vt_spec, tok(KV_LORA_RANK),
                     tok(QK_ROPE_DIM), tok(D_MODEL), tok(D_MODEL)]
    return pl.pallas_call(
        functools.partial(_proj_kernel, sample),
        grid=(n // tm,),
        in_specs=in_specs,
        out_specs=out_specs,
        out_shape=out_shape,
        scratch_shapes=[pltpu.VMEM((tm, GMLP_WIDTH), _BF16)],
        compiler_params=pltpu.CompilerParams(
            dimension_semantics=("parallel",), vmem_limit_bytes=VMEM_LIMIT),
        name="proj_sample" if sample else "proj_prompt",
    )(x, *weights, cos, sin)


def _softmax_step(s, v_b, m_scr, l_scr, acc_scr):
    m_prev = m_scr[...]
    m_new = jnp.maximum(m_prev, jnp.max(s, axis=-1, keepdims=True))
    alpha = jnp.exp(m_prev - m_new)
    p = jnp.exp(s - m_new)
    l_scr[...] = alpha * l_scr[...] + jnp.sum(p, axis=-1, keepdims=True)
    acc_scr[...] = alpha * acc_scr[...] + _dot(p.astype(_BF16), v_b)
    m_scr[...] = m_new


def _pattn_kernel(q_ref, k_ref, vt_ref, o_ref, sa_scr, sb_scr, m_scr, acc_scr):
    qi = pl.program_id(2)
    m_scr[...] = jnp.full_like(m_scr, NEG_INF)
    acc_scr[...] = jnp.zeros_like(acc_scr)

    def scores(kb, buf, masked):
        off = pl.multiple_of(kb * ATT_TK, ATT_TK)
        for hh in range(ATT_HB):
            s = _dot_t(k_ref[hh, pl.ds(off, ATT_TK), :], q_ref[hh])
            if masked:
                key = lax.broadcasted_iota(jnp.int32, s.shape, 0)
                qry = lax.broadcasted_iota(jnp.int32, s.shape, 1)
                s = jnp.where(key <= qry, s, NEG_INF)
            buf[hh] = s

    def update(kb, buf):
        off = pl.multiple_of(kb * ATT_TK, ATT_TK)
        for hh in range(ATT_HB):
            s = buf[hh]
            m_prev = m_scr[hh]
            m_new = jnp.maximum(m_prev, jnp.max(s, axis=0, keepdims=True))
            p = jnp.exp2(s - m_new).astype(_BF16)
            vt = vt_ref[hh, :, pl.ds(off, ATT_TK)]
            vt1 = jnp.concatenate([vt, jnp.ones((ATT_ONES_ROWS, ATT_TK), _BF16)], axis=0)
            acc_scr[hh] = jnp.exp2(m_prev - m_new) * acc_scr[hh] + _dot(vt1, p)
            m_scr[hh] = m_new

    def trip(t, ybuf, xbuf, masked):
        scores(t, xbuf, masked)
        update(t - 1, ybuf)

    @pl.when(qi == 0)
    def _():
        scores(0, sa_scr, True)
        update(0, sa_scr)

    @pl.when(qi > 0)
    def _():
        scores(0, sa_scr, False)

        def pair(j, carry):
            trip(2 * j + 1, sa_scr, sb_scr, False)
            trip(2 * j + 2, sb_scr, sa_scr, False)
            return carry

        lax.fori_loop(0, lax.shift_right_logical(qi - 1, 1), pair, 0)

        @pl.when((qi & 1) == 1)
        def _():
            trip(qi, sa_scr, sb_scr, True)
            update(qi, sb_scr)

        @pl.when((qi & 1) == 0)
        def _():
            trip(qi - 1, sa_scr, sb_scr, False)
            trip(qi, sb_scr, sa_scr, True)
            update(qi, sa_scr)

    for hh in range(ATT_HB):
        acc = acc_scr[hh]
        o_t = acc[:V_HEAD_DIM] / acc[V_HEAD_DIM:V_HEAD_DIM + 1]
        o_ref[:, hh * V_HEAD_DIM:(hh + 1) * V_HEAD_DIM] = o_t.T.astype(o_ref.dtype)


def _prompt_attention(q, k, vt, batch, seq):
    assert ATT_TQ == ATT_TK
    nq = seq // ATT_TQ
    hb = ATT_HB
    return pl.pallas_call(
        _pattn_kernel,
        grid=(batch, N_HEADS // hb, nq),
        in_specs=[
            pl.BlockSpec((hb, ATT_TQ, QK_HEAD_DIM), lambda b, h, i: (h, b * nq + i, 0)),
            pl.BlockSpec((hb, seq, QK_HEAD_DIM), lambda b, h, i: (h, b, 0)),
            pl.BlockSpec((hb, V_HEAD_DIM, seq), lambda b, h, i: (h, 0, b)),
        ],
        out_specs=pl.BlockSpec((ATT_TQ, hb * V_HEAD_DIM), lambda b, h, i: (b * nq + i, h)),
        out_shape=jax.ShapeDtypeStruct((batch * seq, N_HEADS * V_HEAD_DIM), _BF16),
        scratch_shapes=[pltpu.VMEM((hb, ATT_TK, ATT_TQ), _F32),
                        pltpu.VMEM((hb, ATT_TK, ATT_TQ), _F32),
                        pltpu.VMEM((hb, 1, ATT_TQ), _F32),
                        pltpu.VMEM((hb, V_HEAD_DIM + ATT_ONES_ROWS, ATT_TQ), _F32)],
        compiler_params=pltpu.CompilerParams(
            dimension_semantics=("parallel", "parallel", "arbitrary"), vmem_limit_bytes=VMEM_LIMIT),
        name="prompt_attention",
    )(q, k, vt)


def _sattn_kernel(n_pages, pt_ref, qcat_ref, cnew_ref, krnew_ref, wukt_ref, lat_hbm, ropet_hbm, o_ref,
                  a_scr, lat_buf, ropet_buf, sca_scr, scb_scr, scl_scr, sem_lat, sem_rope, m_scr, l_scr,
                  acc_scr):
    pages = SATT_PAGES_PER_CHUNK
    n_chunks = n_pages // pages
    tk = pages * PAGE_SIZE
    nk = N_HEADS * QK_NOPE_DIM
    nrow = N_HEADS * 8
    s = pl.program_id(0)
    slot = lax.rem(s, 2)

    def copies(seq, dst_slot):
        out = []
        for g in range(n_pages):
            page = 0 if seq is None else pt_ref[seq * n_pages + g]
            dst = pl.ds(g * PAGE_SIZE, PAGE_SIZE)
            out.append(pltpu.make_async_copy(lat_hbm.at[page], lat_buf.at[dst_slot, dst], sem_lat.at[dst_slot]))
            out.append(pltpu.make_async_copy(ropet_hbm.at[page], ropet_buf.at[dst_slot, :, dst],
                                             sem_rope.at[dst_slot]))
        return out

    @pl.when(s == 0)
    def _():
        a_scr[0:nk, :] = wukt_ref[...]
        for d in copies(0, 0):
            d.start()

    @pl.when(s + 1 < pl.num_programs(0))
    def _():
        for d in copies(s + 1, 1 - slot):
            d.start()

    for d in copies(None, slot):
        d.wait()

    q2 = qcat_ref[...].reshape(nrow, KV_LORA_RANK + QK_ROPE_DIM)
    a_scr[nk:nk + nrow, :] = q2[:, :KV_LORA_RANK].astype(_BF16)
    q_rope = q2[:, KV_LORA_RANK:].astype(_BF16)
    m_scr[...] = jnp.full_like(m_scr, NEG_INF)
    l_scr[...] = jnp.zeros_like(l_scr)
    acc_scr[...] = jnp.zeros_like(acc_scr)

    def scores(c_b, s_rope):
        big = _dot_t(a_scr[...], c_b)
        rows = []
        for hd in range(N_HEADS):
            kx = big[hd * QK_NOPE_DIM:(hd + 1) * QK_NOPE_DIM, :]
            r = lax.rsqrt(jnp.sum(kx * kx, axis=0, keepdims=True) * (1.0 / QK_NOPE_DIM) + EPS)
            rows.append(big[nk + 8 * hd:nk + 8 * hd + 8, :] * r + s_rope[8 * hd:8 * hd + 8, :])
        return jnp.concatenate(rows, axis=0)

    sc_bufs = [sca_scr, scb_scr] * (n_chunks // 2 + 1)
    sc_bufs = sc_bufs[:n_chunks - 1] + [scl_scr]
    pad = PAGE_SIZE - cnew_ref.shape[0]

    def latent(c):
        c_b = lat_buf[slot, pl.ds(c * tk, tk), :].astype(_BF16)
        if c < n_chunks - 1:
            return c_b
        c_new = jnp.concatenate([cnew_ref[...], jnp.zeros((pad, KV_LORA_RANK), _F32)], axis=0)
        return jnp.concatenate([c_b, c_new.astype(_BF16)], axis=0)

    def stage_scores(c):
        s_rope = _dot(q_rope, ropet_buf[slot, :, pl.ds(c * tk, tk)].astype(_BF16))
        if c < n_chunks - 1:
            sc_bufs[c][...] = scores(latent(c), s_rope)
        else:
            kr_new = jnp.concatenate([krnew_ref[...], jnp.zeros((pad, QK_ROPE_DIM), _F32)], axis=0)
            s_rope = jnp.concatenate([s_rope, _dot_t(q_rope, kr_new.astype(_BF16))], axis=1)
            row = lax.broadcasted_iota(jnp.int32, (nrow, tk + PAGE_SIZE), 0)
            col = lax.broadcasted_iota(jnp.int32, (nrow, tk + PAGE_SIZE), 1)
            sc_bufs[c][...] = jnp.where(col - tk <= (row & 7), scores(latent(c), s_rope), NEG_INF)

    def stage_update(c):
        _softmax_step(sc_bufs[c][...], latent(c), m_scr, l_scr, acc_scr)

    stage_scores(0)
    for c in range(1, n_chunks):
        stage_scores(c)
        stage_update(c - 1)
    stage_update(n_chunks - 1)

    o_ref[...] = (acc_scr[...] / l_scr[...]).reshape(o_ref.shape)


def _sample_attention(page_table, qcat, c_new, kr_new, wukt, cache_lat, cache_rope_t, t_new):
    n_seq, n_pages = page_table.shape
    assert t_new == 8 and n_pages % SATT_PAGES_PER_CHUNK == 0
    tk = SATT_PAGES_PER_CHUNK * PAGE_SIZE
    past = n_pages * PAGE_SIZE
    nrow = N_HEADS * t_new
    dq = KV_LORA_RANK + QK_ROPE_DIM
    grid_spec = pltpu.PrefetchScalarGridSpec(
        num_scalar_prefetch=1,
        grid=(n_seq,),
        in_specs=[
            pl.BlockSpec((N_HEADS, t_new, dq), lambda s, pt: (0, s, 0)),
            pl.BlockSpec((t_new, KV_LORA_RANK), lambda s, pt: (s, 0)),
            pl.BlockSpec((t_new, QK_ROPE_DIM), lambda s, pt: (s, 0)),
            pl.BlockSpec((N_HEADS * QK_NOPE_DIM, KV_LORA_RANK), lambda s, pt: (0, 0)),
            pl.BlockSpec(memory_space=pl.ANY),
            pl.BlockSpec(memory_space=pl.ANY),
        ],
        out_specs=pl.BlockSpec((N_HEADS, t_new, KV_LORA_RANK), lambda s, pt: (0, s, 0)),
        scratch_shapes=[
            pltpu.VMEM((N_HEADS * QK_NOPE_DIM + nrow, KV_LORA_RANK), _BF16),
            pltpu.VMEM((2, past, KV_LORA_RANK), _F32),
            pltpu.VMEM((2, QK_ROPE_DIM, past), _F32),
            pltpu.VMEM((nrow, tk), _F32),
            pltpu.VMEM((nrow, tk), _F32),
            pltpu.VMEM((nrow, tk + PAGE_SIZE), _F32),
            pltpu.SemaphoreType.DMA((2,)),
            pltpu.SemaphoreType.DMA((2,)),
            pltpu.VMEM((nrow, 1), _F32),
            pltpu.VMEM((nrow, 1), _F32),
            pltpu.VMEM((nrow, KV_LORA_RANK), _F32),
        ],
    )
    return pl.pallas_call(
        functools.partial(_sattn_kernel, n_pages),
        grid_spec=grid_spec,
        out_shape=jax.ShapeDtypeStruct((N_HEADS, n_seq * t_new, KV_LORA_RANK), _F32),
        compiler_params=pltpu.CompilerParams(
            dimension_semantics=("arbitrary",), vmem_limit_bytes=VMEM_LIMIT),
        name="sample_attention",
    )(page_table.reshape(-1), qcat, c_new, kr_new, wukt, cache_lat, cache_rope_t)


def _merge_kernel(from_latent, a_ref, gate_ref, gout_ref, x_ref, wuv_ref, woa_ref, wout_ref, o_ref):
    if from_latent:
        heads = [_dot(a_ref[hd].astype(_BF16), wuv_ref[:, hd * V_HEAD_DIM:(hd + 1) * V_HEAD_DIM])
                 for hd in range(N_HEADS)]
        attn = jnp.concatenate(heads, axis=-1).astype(_BF16)
    else:
        attn = a_ref[...]
    merged = gate_ref[...].astype(_F32) * _dot(attn, woa_ref[...]) + gout_ref[...]
    o_ref[...] = x_ref[...] + _dot(merged.astype(_BF16), wout_ref[...])


def _merge(from_latent, attn, gate, gout, x, wuv, woa, wout):
    n = x.shape[0]
    tm = MERGE_TM
    tok = pl.BlockSpec((tm, D_MODEL), lambda i: (i, 0))
    if from_latent:
        a_spec = pl.BlockSpec((N_HEADS, tm, KV_LORA_RANK), lambda i: (0, i, 0))
    else:
        a_spec = tok
    return pl.pallas_call(
        functools.partial(_merge_kernel, from_latent),
        grid=(n // tm,),
        in_specs=[a_spec, tok, tok, tok, _const_spec(wuv.shape), _const_spec(woa.shape),
                  _const_spec(wout.shape)],
        out_specs=tok,
        out_shape=jax.ShapeDtypeStruct((n, D_MODEL), _F32),
        compiler_params=pltpu.CompilerParams(
            dimension_semantics=("parallel",), vmem_limit_bytes=VMEM_LIMIT),
        name="merge_sample" if from_latent else "merge_prompt",
    )(attn, gate, gout, x, wuv, woa, wout)


def _swap_halves(a, axis):
    lo, hi = jnp.split(a, 2, axis=axis)
    return jnp.concatenate([hi, lo], axis=axis)


def _slot(a):
    return jnp.concatenate([a, jnp.zeros_like(a)], axis=-1)


def _rope_tables(pos):
    half = QK_ROPE_DIM // 2
    inv = ROPE_BASE ** (-jnp.arange(half, dtype=_F32) / half)
    ang = pos.astype(_F32)[:, None] * inv[None, :]
    cos, sin = jnp.cos(ang), jnp.sin(ang)
    return _slot(jnp.concatenate([cos, cos], axis=-1)), _slot(jnp.concatenate([-sin, sin], axis=-1))


def _prep_weights(w_in, w_uq, w_uk, w_uv, w_o_gmlp, mix_norm, q_lora_norm, kv_lora_norm, q_nope_norm,
                  q_rope_norm, k_nope_norm, k_rope_norm, gmlp_v_norm):
    w_in = w_in.astype(_BF16)
    w_uq = w_uq.astype(_BF16)
    off_kr = Q_LORA_RANK + KV_LORA_RANK
    k_r = w_in[:, off_kr:off_kr + QK_ROPE_DIM]
    w_in_p = jnp.concatenate([w_in[:, :off_kr], w_in[:, off_kr + QK_ROPE_DIM:], _slot(k_r),
                              _slot(_swap_halves(k_r, -1))], axis=-1)
    assert w_in_p.shape[-1] == _D_IN_PADDED
    wq = w_uq.reshape(Q_LORA_RANK, N_HEADS, QK_HEAD_DIM)
    wq_nope = wq[:, :, :QK_NOPE_DIM].reshape(Q_LORA_RANK, -1)
    wq_rope = wq[:, :, QK_NOPE_DIM:]
    w_uq_p = jnp.concatenate([wq_nope, _slot(wq_rope).reshape(Q_LORA_RANK, -1),
                              _slot(_swap_halves(wq_rope, -1)).reshape(Q_LORA_RANK, -1)], axis=-1)
    row = lambda a: a.reshape(1, -1).astype(_F32)
    return dict(
        mix_norm=row(mix_norm), w_in=w_in_p.astype(_BF16), q_lora_norm=row(q_lora_norm),
        w_uq=w_uq_p.astype(_BF16), kv_lora_norm=row(kv_lora_norm), w_uk=w_uk.astype(_BF16),
        w_uv=w_uv.astype(_BF16), w_uv_t=w_uv.T.astype(_BF16), q_nope_norm=row(q_nope_norm), q_rope_norm=row(_slot(q_rope_norm)),
        q_rope_norm_s=row(_slot(_swap_halves(q_rope_norm, -1))), k_nope_norm=row(k_nope_norm),
        k_rope_norm=row(_slot(k_rope_norm)), k_rope_norm_s=row(_slot(_swap_halves(k_rope_norm, -1))),
        gmlp_v_norm=row(gmlp_v_norm), w_o_gmlp=w_o_gmlp.astype(_BF16))


def kernel(x_prompt, x_sample, cache_kv_latent, cache_k_rope, page_table, ffn1_norm, ffn1_w_gate, ffn1_w_up, ffn1_w_down, mix_norm, w_in, q_lora_norm, w_uq, kv_lora_norm, w_uk, w_uv, q_nope_norm, q_rope_norm, k_nope_norm, k_rope_norm, gmlp_v_norm, gmlp_w_s, gmlp_b_s, w_o_attn, w_o_gmlp, w_out, ffn2_norm, ffn2_w_gate, ffn2_w_up, ffn2_w_down):
    batch, seq, _ = x_prompt.shape
    n_seq, t_new, _ = x_sample.shape
    assert ffn1_norm.shape[0] == 1
    past = page_table.shape[1] * PAGE_SIZE
    n_pool = cache_kv_latent.shape[1]
    l = 0
    row = lambda a: a.reshape(1, -1).astype(_F32)
    w = _prep_weights(w_in[l], w_uq[l], w_uk[l], w_uv[l], w_o_gmlp[l], mix_norm[l], q_lora_norm[l],
                      kv_lora_norm[l], q_nope_norm[l], q_rope_norm[l], k_nope_norm[l], k_rope_norm[l],
                      gmlp_v_norm[l])
    ffn1 = (row(ffn1_norm[l]), ffn1_w_gate[l].astype(_BF16), ffn1_w_up[l].astype(_BF16),
            ffn1_w_down[l].astype(_BF16))
    ffn2 = (row(ffn2_norm[l]), ffn2_w_gate[l].astype(_BF16), ffn2_w_up[l].astype(_BF16),
            ffn2_w_down[l].astype(_BF16))
    woa = w_o_attn[l].astype(_BF16)
    wout = w_out[l].astype(_BF16)

    wp = dict(w, gmlp_w_s=gmlp_w_s[l].astype(_F32), gmlp_b_s=gmlp_b_s[l].reshape(GMLP_GROUPS, CHUNK, 1))
    cos_p, sin_p = _rope_tables(jnp.arange(seq))
    xp = _ffn(x_prompt.reshape(batch * seq, D_MODEL), *ffn1)
    q, k, v, ckv_p, kr_p, gate_p, gout_p = _proj(False, xp, wp, cos_p, sin_p, seq // PROJ_TM)
    attn_p = _prompt_attention(q, k, v, batch, seq)
    xp = _merge(False, attn_p, gate_p, gout_p, xp, w["w_uv"], woa, wout)
    xp = _ffn(xp, *ffn2)

    reps = CHUNK // t_new
    ws_s = jax.vmap(lambda m: jnp.kron(jnp.eye(reps, dtype=_F32), m))(gmlp_w_s[l][:, :t_new, :t_new])
    bs_s = jnp.tile(gmlp_b_s[l][:, :t_new], (1, reps)).reshape(GMLP_GROUPS, CHUNK, 1)
    ws = dict(w, gmlp_w_s=ws_s, gmlp_b_s=bs_s)
    cos_s, sin_s = _rope_tables(past + jnp.arange(PROJ_TM) % t_new)
    xs = _ffn(x_sample.reshape(n_seq * t_new, D_MODEL), *ffn1)
    qcat, ckv_s, kr_s, vn_s, gate_s, gout_s = _proj(True, xs, ws, cos_s, sin_s, 1)
    o_lat = _sample_attention(page_table, qcat, ckv_s, kr_s, w["w_uk"].T,
                              cache_kv_latent.reshape(n_pool, PAGE_SIZE, KV_LORA_RANK),
                              jnp.swapaxes(cache_k_rope.reshape(n_pool, PAGE_SIZE, QK_ROPE_DIM), 1, 2),
                              t_new)
    xs = _merge(True, o_lat, gate_s, gout_s, xs, w["w_uv"], woa, wout)
    xs = _ffn(xs, *ffn2)

    return (xp.reshape(batch, seq, D_MODEL), xs.reshape(n_seq, t_new, D_MODEL),
            ckv_p.reshape(1, batch, seq, KV_LORA_RANK), kr_p.reshape(1, batch, seq, QK_ROPE_DIM),
            ckv_s.reshape(1, n_seq, t_new, KV_LORA_RANK), kr_s.reshape(1, n_seq, t_new, QK_ROPE_DIM),
            vn_s.reshape(1, n_seq, t_new, GMLP_WIDTH))
```

```python
import functools

import jax
import jax.numpy as jnp
import numpy as np
from jax import lax
from jax.experimental import pallas as pl
from jax.experimental.pallas import tpu as pltpu

D_MODEL = 1024
N_HEADS = 8
QK_NOPE_DIM = 128
QK_ROPE_DIM = 64
QK_HEAD_DIM = QK_NOPE_DIM + QK_ROPE_DIM
V_HEAD_DIM = 128
Q_LORA_RANK = 256
KV_LORA_RANK = 256
ROPE_BASE = 10000.0
ATTN_SCALE = QK_HEAD_DIM ** -0.5
LOG2E = 1.4426950408889634
GMLP_GROUPS = 4
GMLP_WIDTH = 1024
GMLP_GROUP_DIM = GMLP_WIDTH // GMLP_GROUPS
CHUNK = 128
PAGE_SIZE = 128
D_FF = 2816
EPS = 1e-6
NEG_INF = -1e30

LANE = 128
ROPE_SLOT = LANE

_OFF_CQ = 0
_OFF_CKV = _OFF_CQ + Q_LORA_RANK
_OFF_U = _OFF_CKV + KV_LORA_RANK
_OFF_V = _OFF_U + GMLP_WIDTH
_OFF_GA = _OFF_V + GMLP_WIDTH
_OFF_GG = _OFF_GA + D_MODEL
_OFF_KR = _OFF_GG + D_MODEL
_OFF_KRS = _OFF_KR + ROPE_SLOT
_D_IN_PADDED = _OFF_KRS + ROPE_SLOT

FFN_TM = 1024
FFN_TF = 256
PROJ_TM = 512
ATT_TQ = 512
ATT_TK = 512
ATT_HB = 2
ATT_ONES_ROWS = 16
MERGE_TM = 512
SATT_PAGES_PER_CHUNK = 16
VMEM_LIMIT = 56 * 1024 * 1024

_BF16 = jnp.bfloat16
_F32 = jnp.float32


def _dot(a, b):
    return jnp.dot(a, b, preferred_element_type=_F32)


def _dot_t(a, b):
    return lax.dot_general(a, b, (((1,), (1,)), ((), ())), preferred_element_type=_F32)


def _rms(x, n):
    return lax.rsqrt(jnp.sum(x * x, axis=-1, keepdims=True) * (1.0 / n) + EPS)


def _ffn_kernel(x_ref, g_ref, wg_ref, wu_ref, wd_ref, o_ref, act_scr):
    x = x_ref[...]
    h = (x * _rms(x, D_MODEL) * g_ref[...]).astype(_BF16)
    for j in range(D_FF // FFN_TF):
        sl = slice(j * FFN_TF, (j + 1) * FFN_TF)
        gate = _dot(h, wg_ref[:, sl])
        up = _dot(h, wu_ref[:, sl])
        act_scr[:, sl] = (gate * jax.nn.sigmoid(gate) * up).astype(_BF16)
    o_ref[...] = x_ref[...] + 0.5 * _dot(act_scr[...], wd_ref[...])


def _ffn(x, norm_g, wg, wu, wd):
    n = x.shape[0]
    tm = min(FFN_TM, n)
    return pl.pallas_call(
        _ffn_kernel,
        grid=(n // tm,),
        in_specs=[
            pl.BlockSpec((tm, D_MODEL), lambda i: (i, 0)),
            _const_spec(norm_g.shape), _const_spec(wg.shape), _const_spec(wu.shape), _const_spec(wd.shape),
        ],
        out_specs=pl.BlockSpec((tm, D_MODEL), lambda i: (i, 0)),
        out_shape=jax.ShapeDtypeStruct((n, D_MODEL), _F32),
        scratch_shapes=[pltpu.VMEM((tm, D_FF), _BF16)],
        compiler_params=pltpu.CompilerParams(
            dimension_semantics=("parallel",), vmem_limit_bytes=VMEM_LIMIT),
        name="ffn",
    )(x, norm_g, wg, wu, wd)


def _rope_slot(x, xs, g, gs, cos, sin):
    r = _rms(x, QK_ROPE_DIM)
    return (x * r * g) * cos + (xs * r * gs) * sin


def _gelu(x):
    return 0.5 * x * (1.0 + lax.erf(x * (2.0 ** -0.5)))


def _proj_kernel(sample, x_ref, mixg_ref, win_ref, qlg_ref, wuq_ref, kvg_ref, wuk_ref, wuv_ref,
                 qng_ref, qrg_ref, qrgs_ref, kng_ref, krg_ref, krgs_ref, vg_ref, ws_ref, bs_ref,
                 wog_ref, cos_ref, sin_ref, *rest):
    if sample:
        qcat_ref, ckv_ref, kr_ref, vn_ref, gate_ref, gout_ref, gm_scr = rest
    else:
        q_ref, k_ref, v_ref, ckv_ref, kr_ref, gate_ref, gout_ref, gm_scr = rest
    tm = x_ref.shape[0]
    x = x_ref[...]
    h = (x * _rms(x, D_MODEL) * mixg_ref[...]).astype(_BF16)
    cos = cos_ref[...]
    sin = sin_ref[...]

    def win(off, width):
        return _dot(h, win_ref[:, off:off + width])

    kr_raw = win(_OFF_KR, 2 * ROPE_SLOT)
    ckv = win(_OFF_CKV, KV_LORA_RANK)
    cq = win(_OFF_CQ, Q_LORA_RANK)
    u = win(_OFF_U, GMLP_WIDTH)
    v = win(_OFF_V, GMLP_WIDTH)

    kr = _rope_slot(kr_raw[:, :ROPE_SLOT], kr_raw[:, ROPE_SLOT:], krg_ref[...], krgs_ref[...],
                    cos, sin)[:, :QK_ROPE_DIM]
    kr_ref[...] = kr
    ckv = ckv * _rms(ckv, KV_LORA_RANK) * kvg_ref[...]
    ckv_ref[...] = ckv
    ckv_b = ckv.astype(_BF16)
    cq_b = (cq * _rms(cq, Q_LORA_RANK) * qlg_ref[...]).astype(_BF16)

    nq = N_HEADS * QK_NOPE_DIM
    nr = N_HEADS * ROPE_SLOT
    if not sample:
        kexp = _dot(ckv_b, wuk_ref[...])
        vals_t = _dot_t(wuv_ref[...], ckv_b)
    q_nope = _dot(cq_b, wuq_ref[:, 0:nq])
    q_rope = _dot(cq_b, wuq_ref[:, nq:nq + nr])
    q_rope_s = _dot(cq_b, wuq_ref[:, nq + nr:nq + 2 * nr])
    gate_g = win(_OFF_GG, D_MODEL)
    gate_a = win(_OFF_GA, D_MODEL)

    if not sample:
        kr_b = kr.astype(_BF16)
        for hd in range(N_HEADS):
            sl = slice(hd * QK_NOPE_DIM, (hd + 1) * QK_NOPE_DIM)
            kh = kexp[:, sl]
            k_ref[hd, :, 0:QK_NOPE_DIM] = (kh * _rms(kh, QK_NOPE_DIM) * kng_ref[...]).astype(_BF16)
            k_ref[hd, :, QK_NOPE_DIM:QK_HEAD_DIM] = kr_b
            v_ref[hd] = vals_t[hd * V_HEAD_DIM:(hd + 1) * V_HEAD_DIM, :].astype(_BF16)

    q_scale = ATTN_SCALE if sample else ATTN_SCALE * LOG2E
    for hd in range(N_HEADS):
        sl = slice(hd * QK_NOPE_DIM, (hd + 1) * QK_NOPE_DIM)
        qh = q_nope[:, sl]
        qh = qh * _rms(qh, QK_NOPE_DIM) * qng_ref[...] * q_scale
        rs = slice(hd * ROPE_SLOT, (hd + 1) * ROPE_SLOT)
        qr = _rope_slot(q_rope[:, rs], q_rope_s[:, rs], qrg_ref[...], qrgs_ref[...], cos, sin)
        qr = qr[:, :QK_ROPE_DIM] * q_scale
        if sample:
            qa = _dot_t((qh * kng_ref[...]).astype(_BF16), wuk_ref[:, sl])
            qcat_ref[hd, :, 0:KV_LORA_RANK] = qa
            qcat_ref[hd, :, KV_LORA_RANK:KV_LORA_RANK + QK_ROPE_DIM] = qr
        else:
            q_ref[hd, :, 0:QK_NOPE_DIM] = qh.astype(_BF16)
            q_ref[hd, :, QK_NOPE_DIM:QK_HEAD_DIM] = qr.astype(_BF16)

    u = _gelu(u)
    v = _gelu(v)
    row = lax.broadcasted_iota(jnp.int32, (CHUNK, CHUNK), 0)
    col = lax.broadcasted_iota(jnp.int32, (CHUNK, CHUNK), 1)
    for g in range(GMLP_GROUPS):
        gs = slice(g * GMLP_GROUP_DIM, (g + 1) * GMLP_GROUP_DIM)
        vg = v[:, gs]
        vg = vg * _rms(vg, GMLP_GROUP_DIM) * vg_ref[:, gs]
        if sample:
            vn_ref[:, gs] = vg
        vg_b = vg.astype(_BF16)
        w = jnp.where(col <= row, ws_ref[g], 0.0).astype(_BF16)
        for c in range(tm // CHUNK):
            cs = slice(c * CHUNK, (c + 1) * CHUNK)
            mix = _dot(w, vg_b[cs]) + bs_ref[g]
            gm_scr[cs, gs] = (u[cs, gs] * mix).astype(_BF16)
    gout_ref[...] = (jax.nn.sigmoid(gate_g) * _dot(gm_scr[...], wog_ref[...])).astype(gout_ref.dtype)
    gate_ref[...] = jax.nn.sigmoid(gate_a).astype(gate_ref.dtype)


def _const_spec(shape):
    nd = len(shape)
    return pl.BlockSpec(shape, lambda i: (0,) * nd, pipeline_mode=pl.Buffered(1))


def _proj(sample, x, w, cos, sin, n_pos_tiles):
    n = x.shape[0]
    tm = PROJ_TM
    tok = lambda width: pl.BlockSpec((tm, width), lambda i: (i, 0))
    heads = lambda width: pl.BlockSpec((N_HEADS, tm, width), lambda i: (0, i, 0))
    weights = [w["mix_norm"], w["w_in"], w["q_lora_norm"], w["w_uq"], w["kv_lora_norm"], w["w_uk"],
               w["w_uv_t"], w["q_nope_norm"], w["q_rope_norm"], w["q_rope_norm_s"], w["k_nope_norm"],
               w["k_rope_norm"], w["k_rope_norm_s"], w["gmlp_v_norm"], w["gmlp_w_s"], w["gmlp_b_s"],
               w["w_o_gmlp"]]
    pos_spec = pl.BlockSpec((tm, ROPE_SLOT), lambda i: (i % n_pos_tiles, 0))
    in_specs = [tok(D_MODEL)] + [_const_spec(a.shape) for a in weights] + [pos_spec, pos_spec]
    sds = jax.ShapeDtypeStruct
    if sample:
        out_shape = [sds((N_HEADS, n, KV_LORA_RANK + QK_ROPE_DIM), _F32), sds((n, KV_LORA_RANK), _F32),
                     sds((n, QK_ROPE_DIM), _F32), sds((n, GMLP_WIDTH), _F32),
                     sds((n, D_MODEL), _BF16), sds((n, D_MODEL), _BF16)]
        out_specs = [heads(KV_LORA_RANK + QK_ROPE_DIM), tok(KV_LORA_RANK), tok(QK_ROPE_DIM),
                     tok(GMLP_WIDTH), tok(D_MODEL), tok(D_MODEL)]
    else:
        out_shape = [sds((N_HEADS, n, QK_HEAD_DIM), _BF16), sds((N_HEADS, n, QK_HEAD_DIM), _BF16),
                     sds((N_HEADS, V_HEAD_DIM, n), _BF16), sds((n, KV_LORA_RANK), _F32),
                     sds((n, QK_ROPE_DIM), _F32), sds((n, D_MODEL), _BF16), sds((n, D_MODEL), _BF16)]
        vt_spec = pl.BlockSpec((N_HEADS, V_HEAD_DIM, tm), lambda i: (0, 0, i))
        out_specs = [heads(QK_HEAD_DIM), heads(QK_HEAD_DIM), vt_spec, tok(KV_LORA_RANK),
                     tok(QK_ROPE_DIM), tok(D_MODEL), tok(D_MODEL)]
    return pl.pallas_call(
        functools.partial(_proj_kernel, sample),
        grid=(n // tm,),
        in_specs=in_specs,
        out_specs=out_specs,
        out_shape=out_shape,
        scratch_shapes=[pltpu.VMEM((tm, GMLP_WIDTH), _BF16)],
        compiler_params=pltpu.CompilerParams(
            dimension_semantics=("parallel",), vmem_limit_bytes=VMEM_LIMIT),
        name="proj_sample" if sample else "proj_prompt",
    )(x, *weights, cos, sin)


def _softmax_step(s, v_b, m_scr, l_scr, acc_scr):
    m_prev = m_scr[...]
    m_new = jnp.maximum(m_prev, jnp.max(s, axis=-1, keepdims=True))
    alpha = jnp.exp(m_prev - m_new)
    p = jnp.exp(s - m_new)
    l_scr[...] = alpha * l_scr[...] + jnp.sum(p, axis=-1, keepdims=True)
    acc_scr[...] = alpha * acc_scr[...] + _dot(p.astype(_BF16), v_b)
    m_scr[...] = m_new


def _pattn_kernel(q_ref, k_ref, vt_ref, o_ref, sa_scr, sb_scr, m_scr, acc_scr):
    qi = pl.program_id(2)
    m_scr[...] = jnp.full_like(m_scr, NEG_INF)
    acc_scr[...] = jnp.zeros_like(acc_scr)

    def scores(kb, buf, masked):
        off = pl.multiple_of(kb * ATT_TK, ATT_TK)
        for hh in range(ATT_HB):
            s = _dot_t(k_ref[hh, pl.ds(off, ATT_TK), :], q_ref[hh])
            if masked:
                key = lax.broadcasted_iota(jnp.int32, s.shape, 0)
                qry = lax.broadcasted_iota(jnp.int32, s.shape, 1)
                s = jnp.where(key <= qry, s, NEG_INF)
            buf[hh] = s

    def update(kb, buf):
        off = pl.multiple_of(kb * ATT_TK, ATT_TK)
        for hh in range(ATT_HB):
            s = buf[hh]
            m_prev = m_scr[hh]
            m_new = jnp.maximum(m_prev, jnp.max(s, axis=0, keepdims=True))
            p = jnp.exp2(s - m_new).astype(_BF16)
            vt = vt_ref[hh, :, pl.ds(off, ATT_TK)]
            vt1 = jnp.concatenate([vt, jnp.ones((ATT_ONES_ROWS, ATT_TK), _BF16)], axis=0)
            acc_scr[hh] = jnp.exp2(m_prev - m_new) * acc_scr[hh] + _dot(vt1, p)
            m_scr[hh] = m_new

    def trip(t, ybuf, xbuf, masked):
        scores(t, xbuf, masked)
        update(t - 1, ybuf)

    @pl.when(qi == 0)
    def _():
        scores(0, sa_scr, True)
        update(0, sa_scr)

    @pl.when(qi > 0)
    def _():
        scores(0, sa_scr, False)

        def pair(j, carry):
            trip(2 * j + 1, sa_scr, sb_scr, False)
            trip(2 * j + 2, sb_scr, sa_scr, False)
            return carry

        lax.fori_loop(0, lax.shift_right_logical(qi - 1, 1), pair, 0)

        @pl.when((qi & 1) == 1)
        def _():
            trip(qi, sa_scr, sb_scr, True)
            update(qi, sb_scr)

        @pl.when((qi & 1) == 0)
        def _():
            trip(qi - 1, sa_scr, sb_scr, False)
            trip(qi, sb_scr, sa_scr, True)
            update(qi, sa_scr)

    for hh in range(ATT_HB):
        acc = acc_scr[hh]
        o_t = acc[:V_HEAD_DIM] / acc[V_HEAD_DIM:V_HEAD_DIM + 1]
        o_ref[:, hh * V_HEAD_DIM:(hh + 1) * V_HEAD_DIM] = o_t.T.astype(o_ref.dtype)


def _prompt_attention(q, k, vt, batch, seq):
    assert ATT_TQ == ATT_TK
    nq = seq // ATT_TQ
    hb = ATT_HB
    return pl.pallas_call(
        _pattn_kernel,
        grid=(batch, N_HEADS // hb, nq),
        in_specs=[
            pl.BlockSpec((hb, ATT_TQ, QK_HEAD_DIM), lambda b, h, i: (h, b * nq + i, 0)),
            pl.BlockSpec((hb, seq, QK_HEAD_DIM), lambda b, h, i: (h, b, 0)),
            pl.BlockSpec((hb, V_HEAD_DIM, seq), lambda b, h, i: (h, 0, b)),
        ],
        out_specs=pl.BlockSpec((ATT_TQ, hb * V_HEAD_DIM), lambda b, h, i: (b * nq + i, h)),
        out_shape=jax.ShapeDtypeStruct((batch * seq, N_HEADS * V_HEAD_DIM), _BF16),
        scratch_shapes=[pltpu.VMEM((hb, ATT_TK, ATT_TQ), _F32),
                        pltpu.VMEM((hb, ATT_TK, ATT_TQ), _F32),
                        pltpu.VMEM((hb, 1, ATT_TQ), _F32),
                        pltpu.VMEM((hb, V_HEAD_DIM + ATT_ONES_ROWS, ATT_TQ), _F32)],
        compiler_params=pltpu.CompilerParams(
            dimension_semantics=("parallel", "parallel", "arbitrary"), vmem_limit_bytes=VMEM_LIMIT),
        name="prompt_attention",
    )(q, k, vt)


def _sattn_kernel(n_pages, pt_ref, qcat_ref, cnew_ref, krnew_ref, wukt_ref, lat_hbm, ropet_hbm, o_ref,
                  a_scr, lat_buf, ropet_buf, sca_scr, scb_scr, scl_scr, sem_lat, sem_rope, m_scr, l_scr,
                  acc_scr):
    pages = SATT_PAGES_PER_CHUNK
    n_chunks = n_pages // pages
    tk = pages * PAGE_SIZE
    nk = N_HEADS * QK_NOPE_DIM
    nrow = N_HEADS * 8
    s = pl.program_id(0)
    slot = lax.rem(s, 2)

    def copies(seq, dst_slot):
        out = []
        for g in range(n_pages):
            page = 0 if seq is None else pt_ref[seq * n_pages + g]
            dst = pl.ds(g * PAGE_SIZE, PAGE_SIZE)
            out.append(pltpu.make_async_copy(lat_hbm.at[page], lat_buf.at[dst_slot, dst], sem_lat.at[dst_slot]))
            out.append(pltpu.make_async_copy(ropet_hbm.at[page], ropet_buf.at[dst_slot, :, dst],
                                             sem_rope.at[dst_slot]))
        return out

    @pl.when(s == 0)
    def _():
        a_scr[0:nk, :] = wukt_ref[...]
        for d in copies(0, 0):
            d.start()

    @pl.when(s + 1 < pl.num_programs(0))
    def _():
        for d in copies(s + 1, 1 - slot):
            d.start()

    for d in copies(None, slot):
        d.wait()

    q2 = qcat_ref[...].reshape(nrow, KV_LORA_RANK + QK_ROPE_DIM)
    a_scr[nk:nk + nrow, :] = q2[:, :KV_LORA_RANK].astype(_BF16)
    q_rope = q2[:, KV_LORA_RANK:].astype(_BF16)
    m_scr[...] = jnp.full_like(m_scr, NEG_INF)
    l_scr[...] = jnp.zeros_like(l_scr)
    acc_scr[...] = jnp.zeros_like(acc_scr)

    def scores(c_b, s_rope):
        big = _dot_t(a_scr[...], c_b)
        rows = []
        for hd in range(N_HEADS):
            kx = big[hd * QK_NOPE_DIM:(hd + 1) * QK_NOPE_DIM, :]
            r = lax.rsqrt(jnp.sum(kx * kx, axis=0, keepdims=True) * (1.0 / QK_NOPE_DIM) + EPS)
            rows.append(big[nk + 8 * hd:nk + 8 * hd + 8, :] * r + s_rope[8 * hd:8 * hd + 8, :])
        return jnp.concatenate(rows, axis=0)

    sc_bufs = [sca_scr, scb_scr] * (n_chunks // 2 + 1)
    sc_bufs = sc_bufs[:n_chunks - 1] + [scl_scr]
    pad = PAGE_SIZE - cnew_ref.shape[0]

    def latent(c):
        c_b = lat_buf[slot, pl.ds(c * tk, tk), :].astype(_BF16)
        if c < n_chunks - 1:
            return c_b
        c_new = jnp.concatenate([cnew_ref[...], jnp.zeros((pad, KV_LORA_RANK), _F32)], axis=0)
        return jnp.concatenate([c_b, c_new.astype(_BF16)], axis=0)

    def stage_scores(c):
        s_rope = _dot(q_rope, ropet_buf[slot, :, pl.ds(c * tk, tk)].astype(_BF16))
        if c < n_chunks - 1:
            sc_bufs[c][...] = scores(latent(c), s_rope)
        else:
            kr_new = jnp.concatenate([krnew_ref[...], jnp.zeros((pad, QK_ROPE_DIM), _F32)], axis=0)
            s_rope = jnp.concatenate([s_rope, _dot_t(q_rope, kr_new.astype(_BF16))], axis=1)
            row = lax.broadcasted_iota(jnp.int32, (nrow, tk + PAGE_SIZE), 0)
            col = lax.broadcasted_iota(jnp.int32, (nrow, tk + PAGE_SIZE), 1)
            sc_bufs[c][...] = jnp.where(col - tk <= (row & 7), scores(latent(c), s_rope), NEG_INF)

    def stage_update(c):
        _softmax_step(sc_bufs[c][...], latent(c), m_scr, l_scr, acc_scr)

    stage_scores(0)
    for c in range(1, n_chunks):
        stage_scores(c)
        stage_update(c - 1)
    stage_update(n_chunks - 1)

    o_ref[...] = (acc_scr[...] / l_scr[...]).reshape(o_ref.shape)


def _sample_attention(page_table, qcat, c_new, kr_new, wukt, cache_lat, cache_rope_t, t_new):
    n_seq, n_pages = page_table.shape
    assert t_new == 8 and n_pages % SATT_PAGES_PER_CHUNK == 0
    tk = SATT_PAGES_PER_CHUNK * PAGE_SIZE
    past = n_pages * PAGE_SIZE
    nrow = N_HEADS * t_new
    dq = KV_LORA_RANK + QK_ROPE_DIM
    grid_spec = pltpu.PrefetchScalarGridSpec(
        num_scalar_prefetch=1,
        grid=(n_seq,),
        in_specs=[
            pl.BlockSpec((N_HEADS, t_new, dq), lambda s, pt: (0, s, 0)),
            pl.BlockSpec((t_new, KV_LORA_RANK), lambda s, pt: (s, 0)),
            pl.BlockSpec((t_new, QK_ROPE_DIM), lambda s, pt: (s, 0)),
            pl.BlockSpec((N_HEADS * QK_NOPE_DIM, KV_LORA_RANK), lambda s, pt: (0, 0)),
            pl.BlockSpec(memory_space=pl.ANY),
            pl.BlockSpec(memory_space=pl.ANY),
        ],
        out_specs=pl.BlockSpec((N_HEADS, t_new, KV_LORA_RANK), lambda s, pt: (0, s, 0)),
        scratch_shapes=[
            pltpu.VMEM((N_HEADS * QK_NOPE_DIM + nrow, KV_LORA_RANK), _BF16),
            pltpu.VMEM((2, past, KV_LORA_RANK), _F32),
            pltpu.VMEM((2, QK_ROPE_DIM, past), _F32),
            pltpu.VMEM((nrow, tk), _F32),
            pltpu.VMEM((nrow, tk), _F32),
            pltpu.VMEM((nrow, tk + PAGE_SIZE), _F32),
            pltpu.SemaphoreType.DMA((2,)),
            pltpu.SemaphoreType.DMA((2,)),
            pltpu.VMEM((nrow, 1), _F32),
            pltpu.VMEM((nrow, 1), _F32),
            pltpu.VMEM((nrow, KV_LORA_RANK), _F32),
        ],
    )
    return pl.pallas_call(
        functools.partial(_sattn_kernel, n_pages),
        grid_spec=grid_spec,
        out_shape=jax.ShapeDtypeStruct((N_HEADS, n_seq * t_new, KV_LORA_RANK), _F32),
        compiler_params=pltpu.CompilerParams(
            dimension_semantics=("arbitrary",), vmem_limit_bytes=VMEM_LIMIT),
        name="sample_attention",
    )(page_table.reshape(-1), qcat, c_new, kr_new, wukt, cache_lat, cache_rope_t)


def _merge_kernel(from_latent, a_ref, gate_ref, gout_ref, x_ref, wuv_ref, woa_ref, wout_ref, o_ref):
    if from_latent:
        heads = [_dot(a_ref[hd].astype(_BF16), wuv_ref[:, hd * V_HEAD_DIM:(hd + 1) * V_HEAD_DIM])
                 for hd in range(N_HEADS)]
        attn = jnp.concatenate(heads, axis=-1).astype(_BF16)
    else:
        attn = a_ref[...]
    merged = gate_ref[...].astype(_F32) * _dot(attn, woa_ref[...]) + gout_ref[...]
    o_ref[...] = x_ref[...] + _dot(merged.astype(_BF16), wout_ref[...])


def _merge(from_latent, attn, gate, gout, x, wuv, woa, wout):
    n = x.shape[0]
    tm = MERGE_TM
    tok = pl.BlockSpec((tm, D_MODEL), lambda i: (i, 0))
    if from_latent:
        a_spec = pl.BlockSpec((N_HEADS, tm, KV_LORA_RANK), lambda i: (0, i, 0))
    else:
        a_spec = tok
    return pl.pallas_call(
        functools.partial(_merge_kernel, from_latent),
        grid=(n // tm,),
        in_specs=[a_spec, tok, tok, tok, _const_spec(wuv.shape), _const_spec(woa.shape),
                  _const_spec(wout.shape)],
        out_specs=tok,
        out_shape=jax.ShapeDtypeStruct((n, D_MODEL), _F32),
        compiler_params=pltpu.CompilerParams(
            dimension_semantics=("parallel",), vmem_limit_bytes=VMEM_LIMIT),
        name="merge_sample" if from_latent else "merge_prompt",
    )(attn, gate, gout, x, wuv, woa, wout)


def _swap_halves(a, axis):
    lo, hi = jnp.split(a, 2, axis=axis)
    return jnp.concatenate([hi, lo], axis=axis)


def _slot(a):
    return jnp.concatenate([a, jnp.zeros_like(a)], axis=-1)


def _rope_tables(pos):
    half = QK_ROPE_DIM // 2
    inv = ROPE_BASE ** (-jnp.arange(half, dtype=_F32) / half)
    ang = pos.astype(_F32)[:, None] * inv[None, :]
    cos, sin = jnp.cos(ang), jnp.sin(ang)
    return _slot(jnp.concatenate([cos, cos], axis=-1)), _slot(jnp.concatenate([-sin, sin], axis=-1))


def _prep_weights(w_in, w_uq, w_uk, w_uv, w_o_gmlp, mix_norm, q_lora_norm, kv_lora_norm, q_nope_norm,
                  q_rope_norm, k_nope_norm, k_rope_norm, gmlp_v_norm):
    w_in = w_in.astype(_BF16)
    w_uq = w_uq.astype(_BF16)
    off_kr = Q_LORA_RANK + KV_LORA_RANK
    k_r = w_in[:, off_kr:off_kr + QK_ROPE_DIM]
    w_in_p = jnp.concatenate([w_in[:, :off_kr], w_in[:, off_kr + QK_ROPE_DIM:], _slot(k_r),
                              _slot(_swap_halves(k_r, -1))], axis=-1)
    assert w_in_p.shape[-1] == _D_IN_PADDED
    wq = w_uq.reshape(Q_LORA_RANK, N_HEADS, QK_HEAD_DIM)
    wq_nope = wq[:, :, :QK_NOPE_DIM].reshape(Q_LORA_RANK, -1)
    wq_rope = wq[:, :, QK_NOPE_DIM:]
    w_uq_p = jnp.concatenate([wq_nope, _slot(wq_rope).reshape(Q_LORA_RANK, -1),
                              _slot(_swap_halves(wq_rope, -1)).reshape(Q_LORA_RANK, -1)], axis=-1)
    row = lambda a: a.reshape(1, -1).astype(_F32)
    return dict(
        mix_norm=row(mix_norm), w_in=w_in_p.astype(_BF16), q_lora_norm=row(q_lora_norm),
        w_uq=w_uq_p.astype(_BF16), kv_lora_norm=row(kv_lora_norm), w_uk=w_uk.astype(_BF16),
        w_uv=w_uv.astype(_BF16), w_uv_t=w_uv.T.astype(_BF16), q_nope_norm=row(q_nope_norm), q_rope_norm=row(_slot(q_rope_norm)),
        q_rope_norm_s=row(_slot(_swap_halves(q_rope_norm, -1))), k_nope_norm=row(k_nope_norm),
        k_rope_norm=row(_slot(k_rope_norm)), k_rope_norm_s=row(_slot(_swap_halves(k_rope_norm, -1))),
        gmlp_v_norm=row(gmlp_v_norm), w_o_gmlp=w_o_gmlp.astype(_BF16))


def kernel(x_prompt, x_sample, cache_kv_latent, cache_k_rope, page_table, ffn1_norm, ffn1_w_gate, ffn1_w_up, ffn1_w_down, mix_norm, w_in, q_lora_norm, w_uq, kv_lora_norm, w_uk, w_uv, q_nope_norm, q_rope_norm, k_nope_norm, k_rope_norm, gmlp_v_norm, gmlp_w_s, gmlp_b_s, w_o_attn, w_o_gmlp, w_out, ffn2_norm, ffn2_w_gate, ffn2_w_up, ffn2_w_down):
    batch, seq, _ = x_prompt.shape
    n_seq, t_new, _ = x_sample.shape
    assert ffn1_norm.shape[0] == 1
    past = page_table.shape[1] * PAGE_SIZE
    n_pool = cache_kv_latent.shape[1]
    l = 0
    row = lambda a: a.reshape(1, -1).astype(_F32)
    w = _prep_weights(w_in[l], w_uq[l], w_uk[l], w_uv[l], w_o_gmlp[l], mix_norm[l], q_lora_norm[l],
                      kv_lora_norm[l], q_nope_norm[l], q_rope_norm[l], k_nope_norm[l], k_rope_norm[l],
                      gmlp_v_norm[l])
    ffn1 = (row(ffn1_norm[l]), ffn1_w_gate[l].astype(_BF16), ffn1_w_up[l].astype(_BF16),
            ffn1_w_down[l].astype(_BF16))
    ffn2 = (row(ffn2_norm[l]), ffn2_w_gate[l].astype(_BF16), ffn2_w_up[l].astype(_BF16),
            ffn2_w_down[l].astype(_BF16))
    woa = w_o_attn[l].astype(_BF16)
    wout = w_out[l].astype(_BF16)

    wp = dict(w, gmlp_w_s=gmlp_w_s[l].astype(_F32), gmlp_b_s=gmlp_b_s[l].reshape(GMLP_GROUPS, CHUNK, 1))
    cos_p, sin_p = _rope_tables(jnp.arange(seq))
    xp = _ffn(x_prompt.reshape(batch * seq, D_MODEL), *ffn1)
    q, k, v, ckv_p, kr_p, gate_p, gout_p = _proj(False, xp, wp, cos_p, sin_p, seq // PROJ_TM)
    attn_p = _prompt_attention(q, k, v, batch, seq)
    xp = _merge(False, attn_p, gate_p, gout_p, xp, w["w_uv"], woa, wout)
    xp = _ffn(xp, *ffn2)

    reps = CHUNK // t_new
    ws_s = jax.vmap(lambda m: jnp.kron(jnp.eye(reps, dtype=_F32), m))(gmlp_w_s[l][:, :t_new, :t_new])
    bs_s = jnp.tile(gmlp_b_s[l][:, :t_new], (1, reps)).reshape(GMLP_GROUPS, CHUNK, 1)
    ws = dict(w, gmlp_w_s=ws_s, gmlp_b_s=bs_s)
    cos_s, sin_s = _rope_tables(past + jnp.arange(PROJ_TM) % t_new)
    xs = _ffn(x_sample.reshape(n_seq * t_new, D_MODEL), *ffn1)
    qcat, ckv_s, kr_s, vn_s, gate_s, gout_s = _proj(True, xs, ws, cos_s, sin_s, 1)
    o_lat = _sample_attention(page_table, qcat, ckv_s, kr_s, w["w_uk"].T,
                              cache_kv_latent.reshape(n_pool, PAGE_SIZE, KV_LORA_RANK),
                              jnp.swapaxes(cache_k_rope.reshape(n_pool, PAGE_SIZE, QK_ROPE_DIM), 1, 2),
                              t_new)
    xs = _merge(True, o_lat, gate_s, gout_s, xs, w["w_uv"], woa, wout)
    xs = _ffn(xs, *ffn2)

    return (xp.reshape(batch, seq, D_MODEL), xs.reshape(n_seq, t_new, D_MODEL),
            ckv_p.reshape(1, batch, seq, KV_LORA_RANK), kr_p.reshape(1, batch, seq, QK_ROPE_DIM),
            ckv_s.reshape(1, n_seq, t_new, KV_LORA_RANK), kr_s.reshape(1, n_seq, t_new, QK_ROPE_DIM),
            vn_s.reshape(1, n_seq, t_new, GMLP_WIDTH))
```

```python
import functools

import jax
import jax.numpy as jnp
import numpy as np
from jax import lax
from jax.experimental import pallas as pl
from jax.experimental.pallas import tpu as pltpu

D_MODEL = 1024
N_HEADS = 8
QK_NOPE_DIM = 128
QK_ROPE_DIM = 64
QK_HEAD_DIM = QK_NOPE_DIM + QK_ROPE_DIM
V_HEAD_DIM = 128
Q_LORA_RANK = 256
KV_LORA_RANK = 256
ROPE_BASE = 10000.0
ATTN_SCALE = QK_HEAD_DIM ** -0.5
LOG2E = 1.4426950408889634
GMLP_GROUPS = 4
GMLP_WIDTH = 1024
GMLP_GROUP_DIM = GMLP_WIDTH // GMLP_GROUPS
CHUNK = 128
PAGE_SIZE = 128
D_FF = 2816
EPS = 1e-6
NEG_INF = -1e30

LANE = 128
ROPE_SLOT = LANE

_OFF_CQ = 0
_OFF_CKV = _OFF_CQ + Q_LORA_RANK
_OFF_U = _OFF_CKV + KV_LORA_RANK
_OFF_V = _OFF_U + GMLP_WIDTH
_OFF_GA = _OFF_V + GMLP_WIDTH
_OFF_GG = _OFF_GA + D_MODEL
_OFF_KR = _OFF_GG + D_MODEL
_OFF_KRS = _OFF_KR + ROPE_SLOT
_D_IN_PADDED = _OFF_KRS + ROPE_SLOT

FFN_TM = 1024
FFN_TF = 256
PROJ_TM = 512
ATT_TQ = 512
ATT_TK = 512
ATT_HB = 2
ATT_ONES_ROWS = 16
MERGE_TM = 512
SATT_PAGES_PER_CHUNK = 16
VMEM_LIMIT = 56 * 1024 * 1024

_BF16 = jnp.bfloat16
_F32 = jnp.float32


def _dot(a, b):
    return jnp.dot(a, b, preferred_element_type=_F32)


def _dot_t(a, b):
    return lax.dot_general(a, b, (((1,), (1,)), ((), ())), preferred_element_type=_F32)


def _rms(x, n):
    return lax.rsqrt(jnp.sum(x * x, axis=-1, keepdims=True) * (1.0 / n) + EPS)


def _ffn_kernel(x_ref, g_ref, wg_ref, wu_ref, wd_ref, o_ref, act_scr):
    x = x_ref[...]
    h = (x * _rms(x, D_MODEL) * g_ref[...]).astype(_BF16)
    for j in range(D_FF // FFN_TF):
        sl = slice(j * FFN_TF, (j + 1) * FFN_TF)
        gate = _dot(h, wg_ref[:, sl])
        up = _dot(h, wu_ref[:, sl])
        act_scr[:, sl] = (gate * jax.nn.sigmoid(gate) * up).astype(_BF16)
    o_ref[...] = x_ref[...] + 0.5 * _dot(act_scr[...], wd_ref[...])


def _ffn(x, norm_g, wg, wu, wd):
    n = x.shape[0]
    tm = min(FFN_TM, n)
    return pl.pallas_call(
        _ffn_kernel,
        grid=(n // tm,),
        in_specs=[
            pl.BlockSpec((tm, D_MODEL), lambda i: (i, 0)),
            _const_spec(norm_g.shape), _const_spec(wg.shape), _const_spec(wu.shape), _const_spec(wd.shape),
        ],
        out_specs=pl.BlockSpec((tm, D_MODEL), lambda i: (i, 0)),
        out_shape=jax.ShapeDtypeStruct((n, D_MODEL), _F32),
        scratch_shapes=[pltpu.VMEM((tm, D_FF), _BF16)],
        compiler_params=pltpu.CompilerParams(
            dimension_semantics=("parallel",), vmem_limit_bytes=VMEM_LIMIT),
        name="ffn",
    )(x, norm_g, wg, wu, wd)


def _rope_slot(x, xs, g, gs, cos, sin):
    r = _rms(x, QK_ROPE_DIM)
    return (x * r * g) * cos + (xs * r * gs) * sin


def _gelu(x):
    return 0.5 * x * (1.0 + lax.erf(x * (2.0 ** -0.5)))


def _proj_kernel(sample, x_ref, mixg_ref, win_ref, qlg_ref, wuq_ref, kvg_ref, wuk_ref, wuv_ref,
                 qng_ref, qrg_ref, qrgs_ref, kng_ref, krg_ref, krgs_ref, vg_ref, ws_ref, bs_ref,
                 wog_ref, cos_ref, sin_ref, *rest):
    if sample:
        qcat_ref, ckv_ref, kr_ref, vn_ref, gate_ref, gout_ref, gm_scr = rest
    else:
        q_ref, k_ref, v_ref, ckv_ref, kr_ref, gate_ref, gout_ref, gm_scr = rest
    tm = x_ref.shape[0]
    x = x_ref[...]
    h = (x * _rms(x, D_MODEL) * mixg_ref[...]).astype(_BF16)
    cos = cos_ref[...]
    sin = sin_ref[...]

    def win(off, width):
        return _dot(h, win_ref[:, off:off + width])

    kr_raw = win(_OFF_KR, 2 * ROPE_SLOT)
    ckv = win(_OFF_CKV, KV_LORA_RANK)
    cq = win(_OFF_CQ, Q_LORA_RANK)
    u = win(_OFF_U, GMLP_WIDTH)
    v = win(_OFF_V, GMLP_WIDTH)

    kr = _rope_slot(kr_raw[:, :ROPE_SLOT], kr_raw[:, ROPE_SLOT:], krg_ref[...], krgs_ref[...],
                    cos, sin)[:, :QK_ROPE_DIM]
    kr_ref[...] = kr
    ckv = ckv * _rms(ckv, KV_LORA_RANK) * kvg_ref[...]
    ckv_ref[...] = ckv
    ckv_b = ckv.astype(_BF16)
    cq_b = (cq * _rms(cq, Q_LORA_RANK) * qlg_ref[...]).astype(_BF16)

    nq = N_HEADS * QK_NOPE_DIM
    nr = N_HEADS * ROPE_SLOT
    if not sample:
        kexp = _dot(ckv_b, wuk_ref[...])
        vals_t = _dot_t(wuv_ref[...], ckv_b)
    q_nope = _dot(cq_b, wuq_ref[:, 0:nq])
    q_rope = _dot(cq_b, wuq_ref[:, nq:nq + nr])
    q_rope_s = _dot(cq_b, wuq_ref[:, nq + nr:nq + 2 * nr])
    gate_g = win(_OFF_GG, D_MODEL)
    gate_a = win(_OFF_GA, D_MODEL)

    if not sample:
        kr_b = kr.astype(_BF16)
        for hd in range(N_HEADS):
            sl = slice(hd * QK_NOPE_DIM, (hd + 1) * QK_NOPE_DIM)
            kh = kexp[:, sl]
            k_ref[hd, :, 0:QK_NOPE_DIM] = (kh * _rms(kh, QK_NOPE_DIM) * kng_ref[...]).astype(_BF16)
            k_ref[hd, :, QK_NOPE_DIM:QK_HEAD_DIM] = kr_b
            v_ref[hd] = vals_t[hd * V_HEAD_DIM:(hd + 1) * V_HEAD_DIM, :].astype(_BF16)

    q_scale = ATTN_SCALE if sample else ATTN_SCALE * LOG2E
    for hd in range(N_HEADS):
        sl = slice(hd * QK_NOPE_DIM, (hd + 1) * QK_NOPE_DIM)
        qh = q_nope[:, sl]
        qh = qh * _rms(qh, QK_NOPE_DIM) * qng_ref[...] * q_scale
        rs = slice(hd * ROPE_SLOT, (hd + 1) * ROPE_SLOT)
        qr = _rope_slot(q_rope[:, rs], q_rope_s[:, rs], qrg_ref[...], qrgs_ref[...], cos, sin)
        qr = qr[:, :QK_ROPE_DIM] * q_scale
        if sample:
            qa = _dot_t((qh * kng_ref[...]).astype(_BF16), wuk_ref[:, sl])
            qcat_ref[hd, :, 0:KV_LORA_RANK] = qa
            qcat_ref[hd, :, KV_LORA_RANK:KV_LORA_RANK + QK_ROPE_DIM] = qr
        else:
            q_ref[hd, :, 0:QK_NOPE_DIM] = qh.astype(_BF16)
            q_ref[hd, :, QK_NOPE_DIM:QK_HEAD_DIM] = qr.astype(_BF16)

    u = _gelu(u)
    v = _gelu(v)
    row = lax.broadcasted_iota(jnp.int32, (CHUNK, CHUNK), 0)
    col = lax.broadcasted_iota(jnp.int32, (CHUNK, CHUNK), 1)
    for g in range(GMLP_GROUPS):
        gs = slice(g * GMLP_GROUP_DIM, (g + 1) * GMLP_GROUP_DIM)
        vg = v[:, gs]
        vg = vg * _rms(vg, GMLP_GROUP_DIM) * vg_ref[:, gs]
        if sample:
            vn_ref[:, gs] = vg
        vg_b = vg.astype(_BF16)
        w = jnp.where(col <= row, ws_ref[g], 0.0).astype(_BF16)
        for c in range(tm // CHUNK):
            cs = slice(c * CHUNK, (c + 1) * CHUNK)
            mix = _dot(w, vg_b[cs]) + bs_ref[g]
            gm_scr[cs, gs] = (u[cs, gs] * mix).astype(_BF16)
    gout_ref[...] = (jax.nn.sigmoid(gate_g) * _dot(gm_scr[...], wog_ref[...])).astype(gout_ref.dtype)
    gate_ref[...] = jax.nn.sigmoid(gate_a).astype(gate_ref.dtype)


def _const_spec(shape):
    nd = len(shape)
    return pl.BlockSpec(shape, lambda i: (0,) * nd, pipeline_mode=pl.Buffered(1))


def _proj(sample, x, w, cos, sin, n_pos_tiles):
    n = x.shape[0]
    tm = PROJ_TM
    tok = lambda width: pl.BlockSpec((tm, width), lambda i: (i, 0))
    heads = lambda width: pl.BlockSpec((N_HEADS, tm, width), lambda i: (0, i, 0))
    weights = [w["mix_norm"], w["w_in"], w["q_lora_norm"], w["w_uq"], w["kv_lora_norm"], w["w_uk"],
               w["w_uv_t"], w["q_nope_norm"], w["q_rope_norm"], w["q_rope_norm_s"], w["k_nope_norm"],
               w["k_rope_norm"], w["k_rope_norm_s"], w["gmlp_v_norm"], w["gmlp_w_s"], w["gmlp_b_s"],
               w["w_o_gmlp"]]
    pos_spec = pl.BlockSpec((tm, ROPE_SLOT), lambda i: (i % n_pos_tiles, 0))
    in_specs = [tok(D_MODEL)] + [_const_spec(a.shape) for a in weights] + [pos_spec, pos_spec]
    sds = jax.ShapeDtypeStruct
    if sample:
        out_shape = [sds((N_HEADS, n, KV_LORA_RANK + QK_ROPE_DIM), _F32), sds((n, KV_LORA_RANK), _F32),
                     sds((n, QK_ROPE_DIM), _F32), sds((n, GMLP_WIDTH), _F32),
                     sds((n, D_MODEL), _BF16), sds((n, D_MODEL), _BF16)]
        out_specs = [heads(KV_LORA_RANK + QK_ROPE_DIM), tok(KV_LORA_RANK), tok(QK_ROPE_DIM),
                     tok(GMLP_WIDTH), tok(D_MODEL), tok(D_MODEL)]
    else:
        out_shape = [sds((N_HEADS, n, QK_HEAD_DIM), _BF16), sds((N_HEADS, n, QK_HEAD_DIM), _BF16),
                     sds((N_HEADS, V_HEAD_DIM, n), _BF16), sds((n, KV_LORA_RANK), _F32),
                     sds((n, QK_ROPE_DIM), _F32), sds((n, D_MODEL), _BF16), sds((n, D_MODEL), _BF16)]
        vt_spec = pl.BlockSpec((N_HEADS, V_HEAD_DIM, tm), lambda i: (0, 0, i))
        out_specs = [heads(QK_HEAD_DIM), heads(QK_HEAD_DIM), vt_spec, tok(KV_LORA_RANK),
                     tok(QK_ROPE_DIM), tok(D_MODEL), tok(D_MODEL)]
    return pl.pallas_call(
        functools.partial(_proj_kernel, sample),
        grid=(n // tm,),
        in_specs=in_specs,
        out_specs=out_specs,
        out_shape=out_shape,
        scratch_shapes=[pltpu.VMEM((tm, GMLP_WIDTH), _BF16)],
        compiler_params=pltpu.CompilerParams(
            dimension_semantics=("parallel",), vmem_limit_bytes=VMEM_LIMIT),
        name="proj_sample" if sample else "proj_prompt",
    )(x, *weights, cos, sin)


def _softmax_step(s, v_b, m_scr, l_scr, acc_scr):
    m_prev = m_scr[...]
    m_new = jnp.maximum(m_prev, jnp.max(s, axis=-1, keepdims=True))
    alpha = jnp.exp(m_prev - m_new)
    p = jnp.exp(s - m_new)
    l_scr[...] = alpha * l_scr[...] + jnp.sum(p, axis=-1, keepdims=True)
    acc_scr[...] = alpha * acc_scr[...] + _dot(p.astype(_BF16), v_b)
    m_scr[...] = m_new


def _pattn_kernel(q_ref, k_ref, vt_ref, o_ref, sa_scr, sb_scr, m_scr, acc_scr):
    qi = pl.program_id(2)
    m_scr[...] = jnp.full_like(m_scr, NEG_INF)
    acc_scr[...] = jnp.zeros_like(acc_scr)

    def scores(kb, buf, masked):
        off = pl.multiple_of(kb * ATT_TK, ATT_TK)
        for hh in range(ATT_HB):
            s = _dot_t(k_ref[hh, pl.ds(off, ATT_TK), :], q_ref[hh])
            if masked:
                key = lax.broadcasted_iota(jnp.int32, s.shape, 0)
                qry = lax.broadcasted_iota(jnp.int32, s.shape, 1)
                s = jnp.where(key <= qry, s, NEG_INF)
            buf[hh] = s

    def update(kb, buf):
        off = pl.multiple_of(kb * ATT_TK, ATT_TK)
        for hh in range(ATT_HB):
            s = buf[hh]
            m_prev = m_scr[hh]
            m_new = jnp.maximum(m_prev, jnp.max(s, axis=0, keepdims=True))
            p = jnp.exp2(s - m_new).astype(_BF16)
            vt = vt_ref[hh, :, pl.ds(off, ATT_TK)]
            vt1 = jnp.concatenate([vt, jnp.ones((ATT_ONES_ROWS, ATT_TK), _BF16)], axis=0)
            acc_scr[hh] = jnp.exp2(m_prev - m_new) * acc_scr[hh] + _dot(vt1, p)
            m_scr[hh] = m_new

    def trip(t, ybuf, xbuf, masked):
        scores(t, xbuf, masked)
        update(t - 1, ybuf)

    @pl.when(qi == 0)
    def _():
        scores(0, sa_scr, True)
        update(0, sa_scr)

    @pl.when(qi > 0)
    def _():
        scores(0, sa_scr, False)

        def pair(j, carry):
            trip(2 * j + 1, sa_scr, sb_scr, False)
            trip(2 * j + 2, sb_scr, sa_scr, False)
            return carry

        lax.fori_loop(0, lax.shift_right_logical(qi - 1, 1), pair, 0)

        @pl.when((qi & 1) == 1)
        def _():
            trip(qi, sa_scr, sb_scr, True)
            update(qi, sb_scr)

        @pl.when((qi & 1) == 0)
        def _():
            trip(qi - 1, sa_scr, sb_scr, False)
            trip(qi, sb_scr, sa_scr, True)
            update(qi, sa_scr)

    for hh in range(ATT_HB):
        acc = acc_scr[hh]
        o_t = acc[:V_HEAD_DIM] / acc[V_HEAD_DIM:V_HEAD_DIM + 1]
        o_ref[:, hh * V_HEAD_DIM:(hh + 1) * V_HEAD_DIM] = o_t.T.astype(o_ref.dtype)


def _prompt_attention(q, k, vt, batch, seq):
    assert ATT_TQ == ATT_TK
    nq = seq // ATT_TQ
    hb = ATT_HB
    return pl.pallas_call(
        _pattn_kernel,
        grid=(batch, N_HEADS // hb, nq),
        in_specs=[
            pl.BlockSpec((hb, ATT_TQ, QK_HEAD_DIM), lambda b, h, i: (h, b * nq + i, 0)),
            pl.BlockSpec((hb, seq, QK_HEAD_DIM), lambda b, h, i: (h, b, 0)),
            pl.BlockSpec((hb, V_HEAD_DIM, seq), lambda b, h, i: (h, 0, b)),
        ],
        out_specs=pl.BlockSpec((ATT_TQ, hb * V_HEAD_DIM), lambda b, h, i: (b * nq + i, h)),
        out_shape=jax.ShapeDtypeStruct((batch * seq, N_HEADS * V_HEAD_DIM), _BF16),
        scratch_shapes=[pltpu.VMEM((hb, ATT_TK, ATT_TQ), _F32),
                        pltpu.VMEM((hb, ATT_TK, ATT_TQ), _F32),
                        pltpu.VMEM((hb, 1, ATT_TQ), _F32),
                        pltpu.VMEM((hb, V_HEAD_DIM + ATT_ONES_ROWS, ATT_TQ), _F32)],
        compiler_params=pltpu.CompilerParams(
            dimension_semantics=("parallel", "parallel", "arbitrary"), vmem_limit_bytes=VMEM_LIMIT),
        name="prompt_attention",
    )(q, k, vt)


def _sattn_kernel(n_pages, pt_ref, qcat_ref, cnew_ref, krnew_ref, wukt_ref, lat_hbm, ropet_hbm, o_ref,
                  a_scr, lat_buf, ropet_buf, sca_scr, scb_scr, scl_scr, cbl_scr, sem_lat, sem_rope,
                  m_scr, l_scr, acc_scr):
    pages = SATT_PAGES_PER_CHUNK
    n_chunks = n_pages // pages
    tk = pages * PAGE_SIZE
    nk = N_HEADS * QK_NOPE_DIM
    nrow = N_HEADS * 8
    s = pl.program_id(0)
    n_seq = pl.num_programs(0) - 1
    slot = lax.rem(s, 2)

    def copies(seq, dst_slot):
        out = []
        for g in range(n_pages):
            page = 0 if seq is None else pt_ref[seq * n_pages + g]
            dst = pl.ds(g * PAGE_SIZE, PAGE_SIZE)
            out.append(pltpu.make_async_copy(lat_hbm.at[page], lat_buf.at[dst_slot, dst], sem_lat.at[dst_slot]))
            out.append(pltpu.make_async_copy(ropet_hbm.at[page], ropet_buf.at[dst_slot, :, dst],
                                             sem_rope.at[dst_slot]))
        return out

    @pl.when(s == 0)
    def _():
        a_scr[0:nk, :] = wukt_ref[...]
        scl_scr[...] = jnp.zeros_like(scl_scr)
        cbl_scr[...] = jnp.zeros_like(cbl_scr)
        m_scr[...] = jnp.zeros_like(m_scr)
        l_scr[...] = jnp.ones_like(l_scr)
        acc_scr[...] = jnp.zeros_like(acc_scr)
        for d in copies(0, 0):
            d.start()

    @pl.when(s < n_seq)
    def _():
        for d in copies(jnp.minimum(s + 1, n_seq - 1), 1 - slot):
            d.start()

    for d in copies(None, slot):
        d.wait()

    q2 = qcat_ref[...].reshape(nrow, KV_LORA_RANK + QK_ROPE_DIM)
    a_scr[nk:nk + nrow, :] = q2[:, :KV_LORA_RANK].astype(_BF16)
    q_rope = q2[:, KV_LORA_RANK:].astype(_BF16)

    def scores(c_b, s_rope):
        big = _dot_t(a_scr[...], c_b)
        rows = []
        for hd in range(N_HEADS):
            kx = big[hd * QK_NOPE_DIM:(hd + 1) * QK_NOPE_DIM, :]
            r = lax.rsqrt(jnp.sum(kx * kx, axis=0, keepdims=True) * (1.0 / QK_NOPE_DIM) + EPS)
            rows.append(big[nk + 8 * hd:nk + 8 * hd + 8, :] * r + s_rope[8 * hd:8 * hd + 8, :])
        return jnp.concatenate(rows, axis=0)

    sc_bufs = [sca_scr, scb_scr] * (n_chunks // 2 + 1)
    sc_bufs = sc_bufs[:n_chunks - 1] + [scl_scr]
    pad = PAGE_SIZE - cnew_ref.shape[0]

    def latent(c):
        return lat_buf[slot, pl.ds(c * tk, tk), :].astype(_BF16)

    def stage_scores(c):
        s_rope = _dot(q_rope, ropet_buf[slot, :, pl.ds(c * tk, tk)].astype(_BF16))
        if c < n_chunks - 1:
            sc_bufs[c][...] = scores(latent(c), s_rope)
        else:
            c_new = jnp.concatenate([cnew_ref[...], jnp.zeros((pad, KV_LORA_RANK), _F32)], axis=0)
            c_b = jnp.concatenate([latent(c), c_new.astype(_BF16)], axis=0)
            cbl_scr[...] = c_b
            kr_new = jnp.concatenate([krnew_ref[...], jnp.zeros((pad, QK_ROPE_DIM), _F32)], axis=0)
            s_rope = jnp.concatenate([s_rope, _dot_t(q_rope, kr_new.astype(_BF16))], axis=1)
            row = lax.broadcasted_iota(jnp.int32, (nrow, tk + PAGE_SIZE), 0)
            col = lax.broadcasted_iota(jnp.int32, (nrow, tk + PAGE_SIZE), 1)
            sc_bufs[c][...] = jnp.where(col - tk <= (row & 7), scores(c_b, s_rope), NEG_INF)

    def stage_update(c):
        _softmax_step(sc_bufs[c][...], latent(c), m_scr, l_scr, acc_scr)

    stage_scores(0)

    _softmax_step(scl_scr[...], cbl_scr[...], m_scr, l_scr, acc_scr)
    o_ref[...] = (acc_scr[...] / l_scr[...]).reshape(o_ref.shape)
    m_scr[...] = jnp.full_like(m_scr, NEG_INF)
    l_scr[...] = jnp.zeros_like(l_scr)
    acc_scr[...] = jnp.zeros_like(acc_scr)

    for c in range(1, n_chunks):
        stage_scores(c)
        stage_update(c - 1)


def _sample_attention(page_table, qcat, c_new, kr_new, wukt, cache_lat, cache_rope_t, t_new):
    n_seq, n_pages = page_table.shape
    assert t_new == 8 and n_pages % SATT_PAGES_PER_CHUNK == 0 and n_pages // SATT_PAGES_PER_CHUNK >= 2
    tk = SATT_PAGES_PER_CHUNK * PAGE_SIZE
    past = n_pages * PAGE_SIZE
    nrow = N_HEADS * t_new
    dq = KV_LORA_RANK + QK_ROPE_DIM
    cur = lambda s: jnp.minimum(s, n_seq - 1)
    prev = lambda s: jnp.maximum(s - 1, 0)
    grid_spec = pltpu.PrefetchScalarGridSpec(
        num_scalar_prefetch=1,
        grid=(n_seq + 1,),
        in_specs=[
            pl.BlockSpec((N_HEADS, t_new, dq), lambda s, pt: (0, cur(s), 0)),
            pl.BlockSpec((t_new, KV_LORA_RANK), lambda s, pt: (cur(s), 0)),
            pl.BlockSpec((t_new, QK_ROPE_DIM), lambda s, pt: (cur(s), 0)),
            pl.BlockSpec((N_HEADS * QK_NOPE_DIM, KV_LORA_RANK), lambda s, pt: (0, 0)),
            pl.BlockSpec(memory_space=pl.ANY),
            pl.BlockSpec(memory_space=pl.ANY),
        ],
        out_specs=pl.BlockSpec((N_HEADS, t_new, KV_LORA_RANK), lambda s, pt: (0, prev(s), 0)),
        scratch_shapes=[
            pltpu.VMEM((N_HEADS * QK_NOPE_DIM + nrow, KV_LORA_RANK), _BF16),
            pltpu.VMEM((2, past, KV_LORA_RANK), _F32),
            pltpu.VMEM((2, QK_ROPE_DIM, past), _F32),
            pltpu.VMEM((nrow, tk), _F32),
            pltpu.VMEM((nrow, tk), _F32),
            pltpu.VMEM((nrow, tk + PAGE_SIZE), _F32),
            pltpu.VMEM((tk + PAGE_SIZE, KV_LORA_RANK), _BF16),
            pltpu.SemaphoreType.DMA((2,)),
            pltpu.SemaphoreType.DMA((2,)),
            pltpu.VMEM((nrow, 1), _F32),
            pltpu.VMEM((nrow, 1), _F32),
            pltpu.VMEM((nrow, KV_LORA_RANK), _F32),
        ],
    )
    return pl.pallas_call(
        functools.partial(_sattn_kernel, n_pages),
        grid_spec=grid_spec,
        out_shape=jax.ShapeDtypeStruct((N_HEADS, n_seq * t_new, KV_LORA_RANK), _F32),
        compiler_params=pltpu.CompilerParams(
            dimension_semantics=("arbitrary",), vmem_limit_bytes=VMEM_LIMIT),
        name="sample_attention",
    )(page_table.reshape(-1), qcat, c_new, kr_new, wukt, cache_lat, cache_rope_t)


def _merge_kernel(from_latent, a_ref, gate_ref, gout_ref, x_ref, wuv_ref, woa_ref, wout_ref, o_ref):
    if from_latent:
        heads = [_dot(a_ref[hd].astype(_BF16), wuv_ref[:, hd * V_HEAD_DIM:(hd + 1) * V_HEAD_DIM])
                 for hd in range(N_HEADS)]
        attn = jnp.concatenate(heads, axis=-1).astype(_BF16)
    else:
        attn = a_ref[...]
    merged = gate_ref[...].astype(_F32) * _dot(attn, woa_ref[...]) + gout_ref[...]
    o_ref[...] = x_ref[...] + _dot(merged.astype(_BF16), wout_ref[...])


def _merge(from_latent, attn, gate, gout, x, wuv, woa, wout):
    n = x.shape[0]
    tm = MERGE_TM
    tok = pl.BlockSpec((tm, D_MODEL), lambda i: (i, 0))
    if from_latent:
        a_spec = pl.BlockSpec((N_HEADS, tm, KV_LORA_RANK), lambda i: (0, i, 0))
    else:
        a_spec = tok
    return pl.pallas_call(
        functools.partial(_merge_kernel, from_latent),
        grid=(n // tm,),
        in_specs=[a_spec, tok, tok, tok, _const_spec(wuv.shape), _const_spec(woa.shape),
                  _const_spec(wout.shape)],
        out_specs=tok,
        out_shape=jax.ShapeDtypeStruct((n, D_MODEL), _F32),
        compiler_params=pltpu.CompilerParams(
            dimension_semantics=("parallel",), vmem_limit_bytes=VMEM_LIMIT),
        name="merge_sample" if from_latent else "merge_prompt",
    )(attn, gate, gout, x, wuv, woa, wout)


def _swap_halves(a, axis):
    lo, hi = jnp.split(a, 2, axis=axis)
    return jnp.concatenate([hi, lo], axis=axis)


def _slot(a):
    return jnp.concatenate([a, jnp.zeros_like(a)], axis=-1)


def _rope_tables(pos):
    half = QK_ROPE_DIM // 2
    inv = ROPE_BASE ** (-jnp.arange(half, dtype=_F32) / half)
    ang = pos.astype(_F32)[:, None] * inv[None, :]
    cos, sin = jnp.cos(ang), jnp.sin(ang)
    return _slot(jnp.concatenate([cos, cos], axis=-1)), _slot(jnp.concatenate([-sin, sin], axis=-1))


def _prep_weights(w_in, w_uq, w_uk, w_uv, w_o_gmlp, mix_norm, q_lora_norm, kv_lora_norm, q_nope_norm,
                  q_rope_norm, k_nope_norm, k_rope_norm, gmlp_v_norm):
    w_in = w_in.astype(_BF16)
    w_uq = w_uq.astype(_BF16)
    off_kr = Q_LORA_RANK + KV_LORA_RANK
    k_r = w_in[:, off_kr:off_kr + QK_ROPE_DIM]
    w_in_p = jnp.concatenate([w_in[:, :off_kr], w_in[:, off_kr + QK_ROPE_DIM:], _slot(k_r),
                              _slot(_swap_halves(k_r, -1))], axis=-1)
    assert w_in_p.shape[-1] == _D_IN_PADDED
    wq = w_uq.reshape(Q_LORA_RANK, N_HEADS, QK_HEAD_DIM)
    wq_nope = wq[:, :, :QK_NOPE_DIM].reshape(Q_LORA_RANK, -1)
    wq_rope = wq[:, :, QK_NOPE_DIM:]
    w_uq_p = jnp.concatenate([wq_nope, _slot(wq_rope).reshape(Q_LORA_RANK, -1),
                              _slot(_swap_halves(wq_rope, -1)).reshape(Q_LORA_RANK, -1)], axis=-1)
    row = lambda a: a.reshape(1, -1).astype(_F32)
    return dict(
        mix_norm=row(mix_norm), w_in=w_in_p.astype(_BF16), q_lora_norm=row(q_lora_norm),
        w_uq=w_uq_p.astype(_BF16), kv_lora_norm=row(kv_lora_norm), w_uk=w_uk.astype(_BF16),
        w_uv=w_uv.astype(_BF16), w_uv_t=w_uv.T.astype(_BF16), q_nope_norm=row(q_nope_norm), q_rope_norm=row(_slot(q_rope_norm)),
        q_rope_norm_s=row(_slot(_swap_halves(q_rope_norm, -1))), k_nope_norm=row(k_nope_norm),
        k_rope_norm=row(_slot(k_rope_norm)), k_rope_norm_s=row(_slot(_swap_halves(k_rope_norm, -1))),
        gmlp_v_norm=row(gmlp_v_norm), w_o_gmlp=w_o_gmlp.astype(_BF16))


def kernel(x_prompt, x_sample, cache_kv_latent, cache_k_rope, page_table, ffn1_norm, ffn1_w_gate, ffn1_w_up, ffn1_w_down, mix_norm, w_in, q_lora_norm, w_uq, kv_lora_norm, w_uk, w_uv, q_nope_norm, q_rope_norm, k_nope_norm, k_rope_norm, gmlp_v_norm, gmlp_w_s, gmlp_b_s, w_o_attn, w_o_gmlp, w_out, ffn2_norm, ffn2_w_gate, ffn2_w_up, ffn2_w_down):
    batch, seq, _ = x_prompt.shape
    n_seq, t_new, _ = x_sample.shape
    assert ffn1_norm.shape[0] == 1
    past = page_table.shape[1] * PAGE_SIZE
    n_pool = cache_kv_latent.shape[1]
    l = 0
    row = lambda a: a.reshape(1, -1).astype(_F32)
    w = _prep_weights(w_in[l], w_uq[l], w_uk[l], w_uv[l], w_o_gmlp[l], mix_norm[l], q_lora_norm[l],
                      kv_lora_norm[l], q_nope_norm[l], q_rope_norm[l], k_nope_norm[l], k_rope_norm[l],
                      gmlp_v_norm[l])
    ffn1 = (row(ffn1_norm[l]), ffn1_w_gate[l].astype(_BF16), ffn1_w_up[l].astype(_BF16),
            ffn1_w_down[l].astype(_BF16))
    ffn2 = (row(ffn2_norm[l]), ffn2_w_gate[l].astype(_BF16), ffn2_w_up[l].astype(_BF16),
            ffn2_w_down[l].astype(_BF16))
    woa = w_o_attn[l].astype(_BF16)
    wout = w_out[l].astype(_BF16)

    wp = dict(w, gmlp_w_s=gmlp_w_s[l].astype(_F32), gmlp_b_s=gmlp_b_s[l].reshape(GMLP_GROUPS, CHUNK, 1))
    cos_p, sin_p = _rope_tables(jnp.arange(seq))
    xp = _ffn(x_prompt.reshape(batch * seq, D_MODEL), *ffn1)
    q, k, v, ckv_p, kr_p, gate_p, gout_p = _proj(False, xp, wp, cos_p, sin_p, seq // PROJ_TM)
    attn_p = _prompt_attention(q, k, v, batch, seq)
    xp = _merge(False, attn_p, gate_p, gout_p, xp, w["w_uv"], woa, wout)
    xp = _ffn(xp, *ffn2)

    reps = CHUNK // t_new
    ws_s = jax.vmap(lambda m: jnp.kron(jnp.eye(reps, dtype=_F32), m))(gmlp_w_s[l][:, :t_new, :t_new])
    bs_s = jnp.tile(gmlp_b_s[l][:, :t_new], (1, reps)).reshape(GMLP_GROUPS, CHUNK, 1)
    ws = dict(w, gmlp_w_s=ws_s, gmlp_b_s=bs_s)
    cos_s, sin_s = _rope_tables(past + jnp.arange(PROJ_TM) % t_new)
    xs = _ffn(x_sample.reshape(n_seq * t_new, D_MODEL), *ffn1)
    qcat, ckv_s, kr_s, vn_s, gate_s, gout_s = _proj(True, xs, ws, cos_s, sin_s, 1)
    o_lat = _sample_attention(page_table, qcat, ckv_s, kr_s, w["w_uk"].T,
                              cache_kv_latent.reshape(n_pool, PAGE_SIZE, KV_LORA_RANK),
                              jnp.swapaxes(cache_k_rope.reshape(n_pool, PAGE_SIZE, QK_ROPE_DIM), 1, 2),
                              t_new)
    xs = _merge(True, o_lat, gate_s, gout_s, xs, w["w_uv"], woa, wout)
    xs = _ffn(xs, *ffn2)

    return (xp.reshape(batch, seq, D_MODEL), xs.reshape(n_seq, t_new, D_MODEL),
            ckv_p.reshape(1, batch, seq, KV_LORA_RANK), kr_p.reshape(1, batch, seq, QK_ROPE_DIM),
            ckv_s.reshape(1, n_seq, t_new, KV_LORA_RANK), kr_s.reshape(1, n_seq, t_new, QK_ROPE_DIM),
            vn_s.reshape(1, n_seq, t_new, GMLP_WIDTH))
```

```python
import functools

import jax
import jax.numpy as jnp
import numpy as np
from jax import lax
from jax.experimental import pallas as pl
from jax.experimental.pallas import tpu as pltpu

D_MODEL = 1024
N_HEADS = 8
QK_NOPE_DIM = 128
QK_ROPE_DIM = 64
QK_HEAD_DIM = QK_NOPE_DIM + QK_ROPE_DIM
V_HEAD_DIM = 128
Q_LORA_RANK = 256
KV_LORA_RANK = 256
ROPE_BASE = 10000.0
ATTN_SCALE = QK_HEAD_DIM ** -0.5
LOG2E = 1.4426950408889634
GMLP_GROUPS = 4
GMLP_WIDTH = 1024
GMLP_GROUP_DIM = GMLP_WIDTH // GMLP_GROUPS
CHUNK = 128
PAGE_SIZE = 128
D_FF = 2816
EPS = 1e-6
NEG_INF = -1e30

LANE = 128
ROPE_SLOT = LANE

_OFF_CQ = 0
_OFF_CKV = _OFF_CQ + Q_LORA_RANK
_OFF_U = _OFF_CKV + KV_LORA_RANK
_OFF_V = _OFF_U + GMLP_WIDTH
_OFF_GA = _OFF_V + GMLP_WIDTH
_OFF_GG = _OFF_GA + D_MODEL
_OFF_KR = _OFF_GG + D_MODEL
_OFF_KRS = _OFF_KR + ROPE_SLOT
_D_IN_PADDED = _OFF_KRS + ROPE_SLOT

FFN_TM = 1024
FFN_TF = 256
PROJ_TM = 512
ATT_TQ = 512
ATT_TK = 512
ATT_HB = 4
ATT_ONES_ROWS = 16
MERGE_TM = 512
SATT_PAGES_PER_CHUNK = 16
VMEM_LIMIT = 56 * 1024 * 1024

_BF16 = jnp.bfloat16
_F32 = jnp.float32


def _dot(a, b):
    return jnp.dot(a, b, preferred_element_type=_F32)


def _dot_t(a, b):
    return lax.dot_general(a, b, (((1,), (1,)), ((), ())), preferred_element_type=_F32)


def _rms(x, n):
    return lax.rsqrt(jnp.sum(x * x, axis=-1, keepdims=True) * (1.0 / n) + EPS)


def _ffn_kernel(x_ref, g_ref, wg_ref, wu_ref, wd_ref, o_ref, act_scr):
    x = x_ref[...]
    h = (x * _rms(x, D_MODEL) * g_ref[...]).astype(_BF16)
    for j in range(D_FF // FFN_TF):
        sl = slice(j * FFN_TF, (j + 1) * FFN_TF)
        gate = _dot(h, wg_ref[:, sl])
        up = _dot(h, wu_ref[:, sl])
        act_scr[:, sl] = (gate * jax.nn.sigmoid(gate) * up).astype(_BF16)
    o_ref[...] = x_ref[...] + 0.5 * _dot(act_scr[...], wd_ref[...])


def _ffn(x, norm_g, wg, wu, wd):
    n = x.shape[0]
    tm = min(FFN_TM, n)
    return pl.pallas_call(
        _ffn_kernel,
        grid=(n // tm,),
        in_specs=[
            pl.BlockSpec((tm, D_MODEL), lambda i: (i, 0)),
            _const_spec(norm_g.shape), _const_spec(wg.shape), _const_spec(wu.shape), _const_spec(wd.shape),
        ],
        out_specs=pl.BlockSpec((tm, D_MODEL), lambda i: (i, 0)),
        out_shape=jax.ShapeDtypeStruct((n, D_MODEL), _F32),
        scratch_shapes=[pltpu.VMEM((tm, D_FF), _BF16)],
        compiler_params=pltpu.CompilerParams(
            dimension_semantics=("parallel",), vmem_limit_bytes=VMEM_LIMIT),
        name="ffn",
    )(x, norm_g, wg, wu, wd)


def _rope_slot(x, xs, g, gs, cos, sin):
    r = _rms(x, QK_ROPE_DIM)
    return (x * r * g) * cos + (xs * r * gs) * sin


def _gelu(x):
    return 0.5 * x * (1.0 + lax.erf(x * (2.0 ** -0.5)))


def _proj_kernel(sample, x_ref, mixg_ref, wa_ref, wb_ref, wkr_ref, qlg_ref, wuq_ref, kvg_ref, wuk_ref, wuv_ref,
                 qng_ref, qrg_ref, qrgs_ref, kng_ref, krg_ref, krgs_ref, vg_ref, ws_ref, bs_ref,
                 wog_ref, cos_ref, sin_ref, *rest):
    if sample:
        qcat_ref, ckv_ref, kr_ref, vn_ref, gate_ref, gout_ref, gm_scr = rest
    else:
        q_ref, k_ref, v_ref, ckv_ref, kr_ref, gate_ref, gout_ref, gm_scr = rest
    tm = x_ref.shape[0]
    x = x_ref[...]
    h = (x * _rms(x, D_MODEL) * mixg_ref[...]).astype(_BF16)
    cos = cos_ref[...]
    sin = sin_ref[...]

    def win(off, width):
        ref, base = (wa_ref, _OFF_CQ) if off < _OFF_U else (wb_ref, _OFF_U) if off < _OFF_KR else (wkr_ref, _OFF_KR)
        return _dot(h, ref[:, off - base:off - base + width])

    kr_raw = win(_OFF_KR, 2 * ROPE_SLOT)
    ckv = win(_OFF_CKV, KV_LORA_RANK)
    cq = win(_OFF_CQ, Q_LORA_RANK)
    u = win(_OFF_U, GMLP_WIDTH)
    v = win(_OFF_V, GMLP_WIDTH)

    kr = _rope_slot(kr_raw[:, :ROPE_SLOT], kr_raw[:, ROPE_SLOT:], krg_ref[...], krgs_ref[...],
                    cos, sin)[:, :QK_ROPE_DIM]
    kr_ref[...] = kr
    ckv = ckv * _rms(ckv, KV_LORA_RANK) * kvg_ref[...]
    ckv_ref[...] = ckv
    ckv_b = ckv.astype(_BF16)
    cq_b = (cq * _rms(cq, Q_LORA_RANK) * qlg_ref[...]).astype(_BF16)

    nq = N_HEADS * QK_NOPE_DIM
    nr = N_HEADS * ROPE_SLOT
    if not sample:
        kexp = _dot(ckv_b, wuk_ref[...])
        vals_t = _dot_t(wuv_ref[...], ckv_b)
    q_nope = _dot(cq_b, wuq_ref[:, 0:nq])
    q_rope = _dot(cq_b, wuq_ref[:, nq:nq + nr])
    q_rope_s = _dot(cq_b, wuq_ref[:, nq + nr:nq + 2 * nr])
    gate_g = win(_OFF_GG, D_MODEL)
    gate_a = win(_OFF_GA, D_MODEL)

    if not sample:
        kr_b = kr.astype(_BF16)
        for hd in range(N_HEADS):
            sl = slice(hd * QK_NOPE_DIM, (hd + 1) * QK_NOPE_DIM)
            kh = kexp[:, sl]
            k_ref[hd, :, 0:QK_NOPE_DIM] = (kh * _rms(kh, QK_NOPE_DIM) * kng_ref[...]).astype(_BF16)
            k_ref[hd, :, QK_NOPE_DIM:QK_HEAD_DIM] = kr_b
            v_ref[hd] = vals_t[hd * V_HEAD_DIM:(hd + 1) * V_HEAD_DIM, :].astype(_BF16)

    q_scale = ATTN_SCALE if sample else ATTN_SCALE * LOG2E
    for hd in range(N_HEADS):
        sl = slice(hd * QK_NOPE_DIM, (hd + 1) * QK_NOPE_DIM)
        qh = q_nope[:, sl]
        qh = qh * _rms(qh, QK_NOPE_DIM) * qng_ref[...] * q_scale
        rs = slice(hd * ROPE_SLOT, (hd + 1) * ROPE_SLOT)
        qr = _rope_slot(q_rope[:, rs], q_rope_s[:, rs], qrg_ref[...], qrgs_ref[...], cos, sin)
        qr = qr[:, :QK_ROPE_DIM] * q_scale
        if sample:
            qa = _dot_t((qh * kng_ref[...]).astype(_BF16), wuk_ref[:, sl])
            qcat_ref[hd, :, 0:KV_LORA_RANK] = qa
            qcat_ref[hd, :, KV_LORA_RANK:KV_LORA_RANK + QK_ROPE_DIM] = qr
        else:
            q_ref[hd, :, 0:QK_NOPE_DIM] = qh.astype(_BF16)
            q_ref[hd, :, QK_NOPE_DIM:QK_HEAD_DIM] = qr.astype(_BF16)

    u = _gelu(u)
    v = _gelu(v)
    row = lax.broadcasted_iota(jnp.int32, (CHUNK, CHUNK), 0)
    col = lax.broadcasted_iota(jnp.int32, (CHUNK, CHUNK), 1)
    for g in range(GMLP_GROUPS):
        gs = slice(g * GMLP_GROUP_DIM, (g + 1) * GMLP_GROUP_DIM)
        vg = v[:, gs]
        vg = vg * _rms(vg, GMLP_GROUP_DIM) * vg_ref[:, gs]
        if sample:
            vn_ref[:, gs] = vg
        vg_b = vg.astype(_BF16)
        w = jnp.where(col <= row, ws_ref[g], 0.0).astype(_BF16)
        for c in range(tm // CHUNK):
            cs = slice(c * CHUNK, (c + 1) * CHUNK)
            mix = _dot(w, vg_b[cs]) + bs_ref[g]
            gm_scr[cs, gs] = (u[cs, gs] * mix).astype(_BF16)
    gout_ref[...] = (jax.nn.sigmoid(gate_g) * _dot(gm_scr[...], wog_ref[...])).astype(gout_ref.dtype)
    gate_ref[...] = jax.nn.sigmoid(gate_a).astype(gate_ref.dtype)


def _const_spec(shape):
    nd = len(shape)
    return pl.BlockSpec(shape, lambda i: (0,) * nd, pipeline_mode=pl.Buffered(1))


def _proj(sample, x, w, cos, sin, n_pos_tiles):
    n = x.shape[0]
    tm = PROJ_TM
    tok = lambda width: pl.BlockSpec((tm, width), lambda i: (i, 0))
    heads = lambda width: pl.BlockSpec((N_HEADS, tm, width), lambda i: (0, i, 0))
    weights = [w["mix_norm"], w["w_in_a"], w["w_in_b"], w["w_in_kr"], w["q_lora_norm"], w["w_uq"],
               w["kv_lora_norm"], w["w_uk"],
               w["w_uv_t"], w["q_nope_norm"], w["q_rope_norm"], w["q_rope_norm_s"], w["k_nope_norm"],
               w["k_rope_norm"], w["k_rope_norm_s"], w["gmlp_v_norm"], w["gmlp_w_s"], w["gmlp_b_s"],
               w["w_o_gmlp"]]
    pos_spec = pl.BlockSpec((tm, ROPE_SLOT), lambda i: (i % n_pos_tiles, 0))
    in_specs = [tok(D_MODEL)] + [_const_spec(a.shape) for a in weights] + [pos_spec, pos_spec]
    sds = jax.ShapeDtypeStruct
    if sample:
        out_shape = [sds((N_HEADS, n, KV_LORA_RANK + QK_ROPE_DIM), _F32), sds((n, KV_LORA_RANK), _F32),
                     sds((n, QK_ROPE_DIM), _F32), sds((n, GMLP_WIDTH), _F32),
                     sds((n, D_MODEL), _BF16), sds((n, D_MODEL), _BF16)]
        out_specs = [heads(KV_LORA_RANK + QK_ROPE_DIM), tok(KV_LORA_RANK), tok(QK_ROPE_DIM),
                     tok(GMLP_WIDTH), tok(D_MODEL), tok(D_MODEL)]
    else:
        out_shape = [sds((N_HEADS, n, QK_HEAD_DIM), _BF16), sds((N_HEADS, n, QK_HEAD_DIM), _BF16),
                     sds((N_HEADS, V_HEAD_DIM, n), _BF16), sds((n, KV_LORA_RANK), _F32),
                     sds((n, QK_ROPE_DIM), _F32), sds((n, D_MODEL), _BF16), sds((n, D_MODEL), _BF16)]
        vt_spec = pl.BlockSpec((N_HEADS, V_HEAD_DIM, tm), lambda i: (0, 0, i))
        out_specs = [heads(QK_HEAD_DIM), heads(QK_HEAD_DIM), vt_spec, tok(KV_LORA_RANK),
                     tok(QK_ROPE_DIM), tok(D_MODEL), tok(D_MODEL)]
    return pl.pallas_call(
        functools.partial(_proj_kernel, sample),
        grid=(n // tm,),
        in_specs=in_specs,
        out_specs=out_specs,
        out_shape=out_shape,
        scratch_shapes=[pltpu.VMEM((tm, GMLP_WIDTH), _BF16)],
        compiler_params=pltpu.CompilerParams(
            dimension_semantics=("parallel",), vmem_limit_bytes=VMEM_LIMIT),
        name="proj_sample" if sample else "proj_prompt",
    )(x, *weights, cos, sin)


def _softmax_step(s, v_b, m_scr, l_scr, acc_scr):
    m_prev = m_scr[...]
    m_new = jnp.maximum(m_prev, jnp.max(s, axis=-1, keepdims=True))
    alpha = jnp.exp(m_prev - m_new)
    p = jnp.exp(s - m_new)
    l_scr[...] = alpha * l_scr[...] + jnp.sum(p, axis=-1, keepdims=True)
    acc_scr[...] = alpha * acc_scr[...] + _dot(p.astype(_BF16), v_b)
    m_scr[...] = m_new


def _pattn_kernel(q_ref, k_ref, vt_ref, o_ref, sa_scr, sb_scr, m_scr, acc_scr):
    qi = pl.program_id(2)
    m_scr[...] = jnp.full_like(m_scr, NEG_INF)
    acc_scr[...] = jnp.zeros_like(acc_scr)

    def scores(kb, buf, masked):
        off = pl.multiple_of(kb * ATT_TK, ATT_TK)
        for hh in range(ATT_HB):
            s = _dot_t(k_ref[hh, pl.ds(off, ATT_TK), :], q_ref[hh])
            if masked:
                key = lax.broadcasted_iota(jnp.int32, s.shape, 0)
                qry = lax.broadcasted_iota(jnp.int32, s.shape, 1)
                s = jnp.where(key <= qry, s, NEG_INF)
            buf[hh] = s

    def update(kb, buf):
        off = pl.multiple_of(kb * ATT_TK, ATT_TK)
        for hh in range(ATT_HB):
            s = buf[hh]
            m_prev = m_scr[hh]
            m_new = jnp.maximum(m_prev, jnp.max(s, axis=0, keepdims=True))
            p = jnp.exp2(s - m_new).astype(_BF16)
            vt = vt_ref[hh, :, pl.ds(off, ATT_TK)]
            vt1 = jnp.concatenate([vt, jnp.ones((ATT_ONES_ROWS, ATT_TK), _BF16)], axis=0)
            acc_scr[hh] = jnp.exp2(m_prev - m_new) * acc_scr[hh] + _dot(vt1, p)
            m_scr[hh] = m_new

    def trip(t, ybuf, xbuf, masked):
        scores(t, xbuf, masked)
        update(t - 1, ybuf)

    @pl.when(qi == 0)
    def _():
        scores(0, sa_scr, True)
        update(0, sa_scr)

    @pl.when(qi > 0)
    def _():
        scores(0, sa_scr, False)

        def pair(j, carry):
            trip(2 * j + 1, sa_scr, sb_scr, False)
            trip(2 * j + 2, sb_scr, sa_scr, False)
            return carry

        lax.fori_loop(0, lax.shift_right_logical(qi - 1, 1), pair, 0)

        @pl.when((qi & 1) == 1)
        def _():
            trip(qi, sa_scr, sb_scr, True)
            update(qi, sb_scr)

        @pl.when((qi & 1) == 0)
        def _():
            trip(qi - 1, sa_scr, sb_scr, False)
            trip(qi, sb_scr, sa_scr, True)
            update(qi, sa_scr)

    for hh in range(ATT_HB):
        acc = acc_scr[hh]
        o_t = acc[:V_HEAD_DIM] / acc[V_HEAD_DIM:V_HEAD_DIM + 1]
        o_ref[:, hh * V_HEAD_DIM:(hh + 1) * V_HEAD_DIM] = o_t.T.astype(o_ref.dtype)


def _prompt_attention(q, k, vt, batch, seq):
    assert ATT_TQ == ATT_TK
    nq = seq // ATT_TQ
    hb = ATT_HB
    return pl.pallas_call(
        _pattn_kernel,
        grid=(batch, N_HEADS // hb, nq),
        in_specs=[
            pl.BlockSpec((hb, ATT_TQ, QK_HEAD_DIM), lambda b, h, i: (h, b * nq + i, 0)),
            pl.BlockSpec((hb, seq, QK_HEAD_DIM), lambda b, h, i: (h, b, 0)),
            pl.BlockSpec((hb, V_HEAD_DIM, seq), lambda b, h, i: (h, 0, b)),
        ],
        out_specs=pl.BlockSpec((ATT_TQ, hb * V_HEAD_DIM), lambda b, h, i: (b * nq + i, h)),
        out_shape=jax.ShapeDtypeStruct((batch * seq, N_HEADS * V_HEAD_DIM), _BF16),
        scratch_shapes=[pltpu.VMEM((hb, ATT_TK, ATT_TQ), _F32),
                        pltpu.VMEM((hb, ATT_TK, ATT_TQ), _F32),
                        pltpu.VMEM((hb, 1, ATT_TQ), _F32),
                        pltpu.VMEM((hb, V_HEAD_DIM + ATT_ONES_ROWS, ATT_TQ), _F32)],
        compiler_params=pltpu.CompilerParams(
            dimension_semantics=("parallel", "parallel", "arbitrary"), vmem_limit_bytes=VMEM_LIMIT),
        name="prompt_attention",
    )(q, k, vt)


def _sattn_kernel(n_pages, pt_ref, qcat_ref, cnew_ref, krnew_ref, wukt_ref, lat_hbm, ropet_hbm, o_ref,
                  a_scr, lat_buf, ropet_buf, sca_scr, scb_scr, scl_scr, cbl_scr, sem_lat, sem_rope,
                  m_scr, l_scr, acc_scr):
    pages = SATT_PAGES_PER_CHUNK
    n_chunks = n_pages // pages
    tk = pages * PAGE_SIZE
    nk = N_HEADS * QK_NOPE_DIM
    nrow = N_HEADS * 8
    s = pl.program_id(0)
    n_seq = pl.num_programs(0) - 1
    slot = lax.rem(s, 2)

    def copies(seq, dst_slot):
        out = []
        for g in range(n_pages):
            page = 0 if seq is None else pt_ref[seq * n_pages + g]
            dst = pl.ds(g * PAGE_SIZE, PAGE_SIZE)
            out.append(pltpu.make_async_copy(lat_hbm.at[page], lat_buf.at[dst_slot, dst], sem_lat.at[dst_slot]))
            out.append(pltpu.make_async_copy(ropet_hbm.at[page], ropet_buf.at[dst_slot, :, dst],
                                             sem_rope.at[dst_slot]))
        return out

    @pl.when(s == 0)
    def _():
        a_scr[0:nk, :] = wukt_ref[...]
        scl_scr[...] = jnp.zeros_like(scl_scr)
        cbl_scr[...] = jnp.zeros_like(cbl_scr)
        m_scr[...] = jnp.zeros_like(m_scr)
        l_scr[...] = jnp.ones_like(l_scr)
        acc_scr[...] = jnp.zeros_like(acc_scr)
        for d in copies(0, 0):
            d.start()

    @pl.when(s < n_seq)
    def _():
        for d in copies(jnp.minimum(s + 1, n_seq - 1), 1 - slot):
            d.start()

    for d in copies(None, slot):
        d.wait()

    q2 = qcat_ref[...].reshape(nrow, KV_LORA_RANK + QK_ROPE_DIM)
    a_scr[nk:nk + nrow, :] = q2[:, :KV_LORA_RANK].astype(_BF16)
    q_rope = q2[:, KV_LORA_RANK:].astype(_BF16)

    def scores(c_b, s_rope):
        big = _dot_t(a_scr[...], c_b)
        rows = []
        for hd in range(N_HEADS):
            kx = big[hd * QK_NOPE_DIM:(hd + 1) * QK_NOPE_DIM, :]
            r = lax.rsqrt(jnp.sum(kx * kx, axis=0, keepdims=True) * (1.0 / QK_NOPE_DIM) + EPS)
            rows.append(big[nk + 8 * hd:nk + 8 * hd + 8, :] * r + s_rope[8 * hd:8 * hd + 8, :])
        return jnp.concatenate(rows, axis=0)

    sc_bufs = [sca_scr, scb_scr] * (n_chunks // 2 + 1)
    sc_bufs = sc_bufs[:n_chunks - 1] + [scl_scr]
    pad = PAGE_SIZE - cnew_ref.shape[0]

    def latent(c):
        return lat_buf[slot, pl.ds(c * tk, tk), :].astype(_BF16)

    def stage_scores(c):
        s_rope = _dot(q_rope, ropet_buf[slot, :, pl.ds(c * tk, tk)].astype(_BF16))
        if c < n_chunks - 1:
            sc_bufs[c][...] = scores(latent(c), s_rope)
        else:
            c_new = jnp.concatenate([cnew_ref[...], jnp.zeros((pad, KV_LORA_RANK), _F32)], axis=0)
            c_b = jnp.concatenate([latent(c), c_new.astype(_BF16)], axis=0)
            cbl_scr[...] = c_b
            kr_new = jnp.concatenate([krnew_ref[...], jnp.zeros((pad, QK_ROPE_DIM), _F32)], axis=0)
            s_rope = jnp.concatenate([s_rope, _dot_t(q_rope, kr_new.astype(_BF16))], axis=1)
            row = lax.broadcasted_iota(jnp.int32, (nrow, tk + PAGE_SIZE), 0)
            col = lax.broadcasted_iota(jnp.int32, (nrow, tk + PAGE_SIZE), 1)
            sc_bufs[c][...] = jnp.where(col - tk <= (row & 7), scores(c_b, s_rope), NEG_INF)

    def stage_update(c):
        _softmax_step(sc_bufs[c][...], latent(c), m_scr, l_scr, acc_scr)

    stage_scores(0)

    _softmax_step(scl_scr[...], cbl_scr[...], m_scr, l_scr, acc_scr)
    o_ref[...] = (acc_scr[...] / l_scr[...]).reshape(o_ref.shape)
    m_scr[...] = jnp.full_like(m_scr, NEG_INF)
    l_scr[...] = jnp.zeros_like(l_scr)
    acc_scr[...] = jnp.zeros_like(acc_scr)

    for c in range(1, n_chunks):
        stage_scores(c)
        stage_update(c - 1)


def _sample_attention(page_table, qcat, c_new, kr_new, wukt, cache_lat, cache_rope_t, t_new):
    n_seq, n_pages = page_table.shape
    assert t_new == 8 and n_pages % SATT_PAGES_PER_CHUNK == 0 and n_pages // SATT_PAGES_PER_CHUNK >= 2
    tk = SATT_PAGES_PER_CHUNK * PAGE_SIZE
    past = n_pages * PAGE_SIZE
    nrow = N_HEADS * t_new
    dq = KV_LORA_RANK + QK_ROPE_DIM
    cur = lambda s: jnp.minimum(s, n_seq - 1)
    prev = lambda s: jnp.maximum(s - 1, 0)
    grid_spec = pltpu.PrefetchScalarGridSpec(
        num_scalar_prefetch=1,
        grid=(n_seq + 1,),
        in_specs=[
            pl.BlockSpec((N_HEADS, t_new, dq), lambda s, pt: (0, cur(s), 0)),
            pl.BlockSpec((t_new, KV_LORA_RANK), lambda s, pt: (cur(s), 0)),
            pl.BlockSpec((t_new, QK_ROPE_DIM), lambda s, pt: (cur(s), 0)),
            pl.BlockSpec((N_HEADS * QK_NOPE_DIM, KV_LORA_RANK), lambda s, pt: (0, 0)),
            pl.BlockSpec(memory_space=pl.ANY),
            pl.BlockSpec(memory_space=pl.ANY),
        ],
        out_specs=pl.BlockSpec((N_HEADS, t_new, KV_LORA_RANK), lambda s, pt: (0, prev(s), 0)),
        scratch_shapes=[
            pltpu.VMEM((N_HEADS * QK_NOPE_DIM + nrow, KV_LORA_RANK), _BF16),
            pltpu.VMEM((2, past, KV_LORA_RANK), _F32),
            pltpu.VMEM((2, QK_ROPE_DIM, past), _F32),
            pltpu.VMEM((nrow, tk), _F32),
            pltpu.VMEM((nrow, tk), _F32),
            pltpu.VMEM((nrow, tk + PAGE_SIZE), _F32),
            pltpu.VMEM((tk + PAGE_SIZE, KV_LORA_RANK), _BF16),
            pltpu.SemaphoreType.DMA((2,)),
            pltpu.SemaphoreType.DMA((2,)),
            pltpu.VMEM((nrow, 1), _F32),
            pltpu.VMEM((nrow, 1), _F32),
            pltpu.VMEM((nrow, KV_LORA_RANK), _F32),
        ],
    )
    return pl.pallas_call(
        functools.partial(_sattn_kernel, n_pages),
        grid_spec=grid_spec,
        out_shape=jax.ShapeDtypeStruct((N_HEADS, n_seq * t_new, KV_LORA_RANK), _F32),
        compiler_params=pltpu.CompilerParams(
            dimension_semantics=("arbitrary",), vmem_limit_bytes=VMEM_LIMIT),
        name="sample_attention",
    )(page_table.reshape(-1), qcat, c_new, kr_new, wukt, cache_lat, cache_rope_t)


def _merge_kernel(from_latent, a_ref, gate_ref, gout_ref, x_ref, wuv_ref, woa_ref, wout_ref, o_ref):
    if from_latent:
        heads = [_dot(a_ref[hd].astype(_BF16), wuv_ref[:, hd * V_HEAD_DIM:(hd + 1) * V_HEAD_DIM])
                 for hd in range(N_HEADS)]
        attn = jnp.concatenate(heads, axis=-1).astype(_BF16)
    else:
        attn = a_ref[...]
    merged = gate_ref[...].astype(_F32) * _dot(attn, woa_ref[...]) + gout_ref[...]
    o_ref[...] = x_ref[...] + _dot(merged.astype(_BF16), wout_ref[...])


def _merge(from_latent, attn, gate, gout, x, wuv, woa, wout):
    n = x.shape[0]
    tm = MERGE_TM
    tok = pl.BlockSpec((tm, D_MODEL), lambda i: (i, 0))
    if from_latent:
        a_spec = pl.BlockSpec((N_HEADS, tm, KV_LORA_RANK), lambda i: (0, i, 0))
    else:
        a_spec = tok
    return pl.pallas_call(
        functools.partial(_merge_kernel, from_latent),
        grid=(n // tm,),
        in_specs=[a_spec, tok, tok, tok, _const_spec(wuv.shape), _const_spec(woa.shape),
                  _const_spec(wout.shape)],
        out_specs=tok,
        out_shape=jax.ShapeDtypeStruct((n, D_MODEL), _F32),
        compiler_params=pltpu.CompilerParams(
            dimension_semantics=("parallel",), vmem_limit_bytes=VMEM_LIMIT),
        name="merge_sample" if from_latent else "merge_prompt",
    )(attn, gate, gout, x, wuv, woa, wout)


def _swap_halves(a, axis):
    lo, hi = jnp.split(a, 2, axis=axis)
    return jnp.concatenate([hi, lo], axis=axis)


def _slot(a):
    return jnp.concatenate([a, jnp.zeros_like(a)], axis=-1)


def _rope_tables(pos):
    half = QK_ROPE_DIM // 2
    inv = ROPE_BASE ** (-jnp.arange(half, dtype=_F32) / half)
    ang = pos.astype(_F32)[:, None] * inv[None, :]
    cos, sin = jnp.cos(ang), jnp.sin(ang)
    return _slot(jnp.concatenate([cos, cos], axis=-1)), _slot(jnp.concatenate([-sin, sin], axis=-1))


def _prep_weights(w_in, w_uq, w_uk, w_uv, w_o_gmlp, mix_norm, q_lora_norm, kv_lora_norm, q_nope_norm,
                  q_rope_norm, k_nope_norm, k_rope_norm, gmlp_v_norm):
    w_in = w_in.astype(_BF16)
    w_uq = w_uq.astype(_BF16)
    off_kr = Q_LORA_RANK + KV_LORA_RANK
    k_r = w_in[:, off_kr:off_kr + QK_ROPE_DIM]
    w_in_a = w_in[:, :off_kr]
    w_in_b = w_in[:, off_kr + QK_ROPE_DIM:]
    w_in_kr = jnp.concatenate([_slot(k_r), _slot(_swap_halves(k_r, -1))], axis=-1)
    assert off_kr + w_in_b.shape[-1] + w_in_kr.shape[-1] == _D_IN_PADDED
    wq = w_uq.reshape(Q_LORA_RANK, N_HEADS, QK_HEAD_DIM)
    wq_nope = wq[:, :, :QK_NOPE_DIM].reshape(Q_LORA_RANK, -1)
    wq_rope = wq[:, :, QK_NOPE_DIM:]
    w_uq_p = jnp.concatenate([wq_nope, _slot(wq_rope).reshape(Q_LORA_RANK, -1),
                              _slot(_swap_halves(wq_rope, -1)).reshape(Q_LORA_RANK, -1)], axis=-1)
    row = lambda a: a.reshape(1, -1).astype(_F32)
    return dict(
        mix_norm=row(mix_norm), w_in_a=w_in_a, w_in_b=w_in_b, w_in_kr=w_in_kr, q_lora_norm=row(q_lora_norm),
        w_uq=w_uq_p.astype(_BF16), kv_lora_norm=row(kv_lora_norm), w_uk=w_uk.astype(_BF16),
        w_uv=w_uv.astype(_BF16), w_uv_t=w_uv.T.astype(_BF16), q_nope_norm=row(q_nope_norm), q_rope_norm=row(_slot(q_rope_norm)),
        q_rope_norm_s=row(_slot(_swap_halves(q_rope_norm, -1))), k_nope_norm=row(k_nope_norm),
        k_rope_norm=row(_slot(k_rope_norm)), k_rope_norm_s=row(_slot(_swap_halves(k_rope_norm, -1))),
        gmlp_v_norm=row(gmlp_v_norm), w_o_gmlp=w_o_gmlp.astype(_BF16))


def kernel(x_prompt, x_sample, cache_kv_latent, cache_k_rope, page_table, ffn1_norm, ffn1_w_gate, ffn1_w_up, ffn1_w_down, mix_norm, w_in, q_lora_norm, w_uq, kv_lora_norm, w_uk, w_uv, q_nope_norm, q_rope_norm, k_nope_norm, k_rope_norm, gmlp_v_norm, gmlp_w_s, gmlp_b_s, w_o_attn, w_o_gmlp, w_out, ffn2_norm, ffn2_w_gate, ffn2_w_up, ffn2_w_down):
    batch, seq, _ = x_prompt.shape
    n_seq, t_new, _ = x_sample.shape
    assert ffn1_norm.shape[0] == 1
    past = page_table.shape[1] * PAGE_SIZE
    n_pool = cache_kv_latent.shape[1]
    l = 0
    row = lambda a: a.reshape(1, -1).astype(_F32)
    w = _prep_weights(w_in[l], w_uq[l], w_uk[l], w_uv[l], w_o_gmlp[l], mix_norm[l], q_lora_norm[l],
                      kv_lora_norm[l], q_nope_norm[l], q_rope_norm[l], k_nope_norm[l], k_rope_norm[l],
                      gmlp_v_norm[l])
    ffn1 = (row(ffn1_norm[l]), ffn1_w_gate[l].astype(_BF16), ffn1_w_up[l].astype(_BF16),
            ffn1_w_down[l].astype(_BF16))
    ffn2 = (row(ffn2_norm[l]), ffn2_w_gate[l].astype(_BF16), ffn2_w_up[l].astype(_BF16),
            ffn2_w_down[l].astype(_BF16))
    woa = w_o_attn[l].astype(_BF16)
    wout = w_out[l].astype(_BF16)

    wp = dict(w, gmlp_w_s=gmlp_w_s[l].astype(_F32), gmlp_b_s=gmlp_b_s[l].reshape(GMLP_GROUPS, CHUNK, 1))
    cos_p, sin_p = _rope_tables(jnp.arange(seq))
    xp = _ffn(x_prompt.reshape(batch * seq, D_MODEL), *ffn1)
    q, k, v, ckv_p, kr_p, gate_p, gout_p = _proj(False, xp, wp, cos_p, sin_p, seq // PROJ_TM)
    attn_p = _prompt_attention(q, k, v, batch, seq)
    xp = _merge(False, attn_p, gate_p, gout_p, xp, w["w_uv"], woa, wout)
    xp = _ffn(xp, *ffn2)

    reps = CHUNK // t_new
    ws_s = jax.vmap(lambda m: jnp.kron(jnp.eye(reps, dtype=_F32), m))(gmlp_w_s[l][:, :t_new, :t_new])
    bs_s = jnp.tile(gmlp_b_s[l][:, :t_new], (1, reps)).reshape(GMLP_GROUPS, CHUNK, 1)
    ws = dict(w, gmlp_w_s=ws_s, gmlp_b_s=bs_s)
    cos_s, sin_s = _rope_tables(past + jnp.arange(PROJ_TM) % t_new)
    xs = _ffn(x_sample.reshape(n_seq * t_new, D_MODEL), *ffn1)
    qcat, ckv_s, kr_s, vn_s, gate_s, gout_s = _proj(True, xs, ws, cos_s, sin_s, 1)
    o_lat = _sample_attention(page_table, qcat, ckv_s, kr_s, w["w_uk"].T,
                              cache_kv_latent.reshape(n_pool, PAGE_SIZE, KV_LORA_RANK),
                              jnp.swapaxes(cache_k_rope.reshape(n_pool, PAGE_SIZE, QK_ROPE_DIM), 1, 2),
                              t_new)
    xs = _merge(True, o_lat, gate_s, gout_s, xs, w["w_uv"], woa, wout)
    xs = _ffn(xs, *ffn2)

    return (xp.reshape(batch, seq, D_MODEL), xs.reshape(n_seq, t_new, D_MODEL),
            ckv_p.reshape(1, batch, seq, KV_LORA_RANK), kr_p.reshape(1, batch, seq, QK_ROPE_DIM),
            ckv_s.reshape(1, n_seq, t_new, KV_LORA_RANK), kr_s.reshape(1, n_seq, t_new, QK_ROPE_DIM),
            vn_s.reshape(1, n_seq, t_new, GMLP_WIDTH))
```

```python
import functools

import jax
import jax.numpy as jnp
import numpy as np
from jax import lax
from jax.experimental import pallas as pl
from jax.experimental.pallas import tpu as pltpu

D_MODEL = 1024
N_HEADS = 8
QK_NOPE_DIM = 128
QK_ROPE_DIM = 64
QK_HEAD_DIM = QK_NOPE_DIM + QK_ROPE_DIM
V_HEAD_DIM = 128
Q_LORA_RANK = 256
KV_LORA_RANK = 256
ROPE_BASE = 10000.0
ATTN_SCALE = QK_HEAD_DIM ** -0.5
LOG2E = 1.4426950408889634
GMLP_GROUPS = 4
GMLP_WIDTH = 1024
GMLP_GROUP_DIM = GMLP_WIDTH // GMLP_GROUPS
CHUNK = 128
PAGE_SIZE = 128
D_FF = 2816
EPS = 1e-6
NEG_INF = -1e30

LANE = 128
ROPE_SLOT = LANE

_OFF_CQ = 0
_OFF_CKV = _OFF_CQ + Q_LORA_RANK
_OFF_U = _OFF_CKV + KV_LORA_RANK
_OFF_V = _OFF_U + GMLP_WIDTH
_OFF_GA = _OFF_V + GMLP_WIDTH
_OFF_GG = _OFF_GA + D_MODEL
_OFF_KR = _OFF_GG + D_MODEL
_OFF_KRS = _OFF_KR + ROPE_SLOT
_D_IN_PADDED = _OFF_KRS + ROPE_SLOT

FFN_TM = 1024
FFN_TF = 256
PROJ_TM = 512
ATT_TQ = 512
ATT_TK = 512
ATT_HB = 4
ATT_ONES_ROWS = 16
MERGE_TM = 512
SATT_PAGES_PER_CHUNK = 16
VMEM_LIMIT = 56 * 1024 * 1024

_BF16 = jnp.bfloat16
_F32 = jnp.float32


def _dot(a, b):
    return jnp.dot(a, b, preferred_element_type=_F32)


def _dot_t(a, b):
    return lax.dot_general(a, b, (((1,), (1,)), ((), ())), preferred_element_type=_F32)


def _rms(x, n):
    return lax.rsqrt(jnp.sum(x * x, axis=-1, keepdims=True) * (1.0 / n) + EPS)


def _ffn_kernel(x_ref, g_ref, wg_ref, wu_ref, wd_ref, o_ref, act_scr):
    x = x_ref[...]
    h = (x * _rms(x, D_MODEL) * g_ref[...]).astype(_BF16)
    for j in range(D_FF // FFN_TF):
        sl = slice(j * FFN_TF, (j + 1) * FFN_TF)
        gate = _dot(h, wg_ref[:, sl])
        up = _dot(h, wu_ref[:, sl])
        act_scr[:, sl] = (gate * jax.nn.sigmoid(gate) * up).astype(_BF16)
    o_ref[...] = x_ref[...] + 0.5 * _dot(act_scr[...], wd_ref[...])


def _ffn(x, norm_g, wg, wu, wd):
    n = x.shape[0]
    tm = min(FFN_TM, n)
    return pl.pallas_call(
        _ffn_kernel,
        grid=(n // tm,),
        in_specs=[
            pl.BlockSpec((tm, D_MODEL), lambda i: (i, 0)),
            _const_spec(norm_g.shape), _const_spec(wg.shape), _const_spec(wu.shape), _const_spec(wd.shape),
        ],
        out_specs=pl.BlockSpec((tm, D_MODEL), lambda i: (i, 0)),
        out_shape=jax.ShapeDtypeStruct((n, D_MODEL), _F32),
        scratch_shapes=[pltpu.VMEM((tm, D_FF), _BF16)],
        compiler_params=pltpu.CompilerParams(
            dimension_semantics=("parallel",), vmem_limit_bytes=VMEM_LIMIT),
        name="ffn",
    )(x, norm_g, wg, wu, wd)


def _rope_slot(x, xs, g, gs, cos, sin):
    r = _rms(x, QK_ROPE_DIM)
    return (x * r * g) * cos + (xs * r * gs) * sin


def _gelu(x):
    return 0.5 * x * (1.0 + lax.erf(x * (2.0 ** -0.5)))


def _proj_kernel(sample, x_ref, mixg_ref, wt_ref, wkr_ref, qlg_ref, wuq_ref, kvg_ref, wuk_ref, wuv_ref,
                 qng_ref, qrg_ref, qrgs_ref, kng_ref, krg_ref, krgs_ref, vg_ref, ws_ref, bs_ref,
                 wog_ref, cos_ref, sin_ref, *rest):
    if sample:
        qcat_ref, ckv_ref, kr_ref, vn_ref, gate_ref, gout_ref, gm_scr = rest
    else:
        q_ref, k_ref, v_ref, ckv_ref, kr_ref, gate_ref, gout_ref, gm_scr = rest
    tm = x_ref.shape[0]
    x = x_ref[...]
    h = (x * _rms(x, D_MODEL) * mixg_ref[...]).astype(_BF16)
    cos = cos_ref[...]
    sin = sin_ref[...]

    def win(off, width):
        if off >= _OFF_KR:
            assert (off, width) == (_OFF_KR, 2 * ROPE_SLOT)
            return _dot_t(h, wkr_ref[...])
        row = off if off < _OFF_U else off + QK_ROPE_DIM
        return _dot_t(h, wt_ref[row:row + width, :])

    kr_raw = win(_OFF_KR, 2 * ROPE_SLOT)
    ckv = win(_OFF_CKV, KV_LORA_RANK)
    cq = win(_OFF_CQ, Q_LORA_RANK)
    u = win(_OFF_U, GMLP_WIDTH)
    v = win(_OFF_V, GMLP_WIDTH)

    kr = _rope_slot(kr_raw[:, :ROPE_SLOT], kr_raw[:, ROPE_SLOT:], krg_ref[...], krgs_ref[...],
                    cos, sin)[:, :QK_ROPE_DIM]
    kr_ref[...] = kr
    ckv = ckv * _rms(ckv, KV_LORA_RANK) * kvg_ref[...]
    ckv_ref[...] = ckv
    ckv_b = ckv.astype(_BF16)
    cq_b = (cq * _rms(cq, Q_LORA_RANK) * qlg_ref[...]).astype(_BF16)

    nq = N_HEADS * QK_NOPE_DIM
    nr = N_HEADS * ROPE_SLOT
    if not sample:
        kexp = _dot(ckv_b, wuk_ref[...])
        vals_t = _dot_t(wuv_ref[...], ckv_b)
    q_nope = _dot(cq_b, wuq_ref[:, 0:nq])
    q_rope = _dot(cq_b, wuq_ref[:, nq:nq + nr])
    q_rope_s = _dot(cq_b, wuq_ref[:, nq + nr:nq + 2 * nr])
    gate_g = win(_OFF_GG, D_MODEL)
    gate_a = win(_OFF_GA, D_MODEL)

    if not sample:
        kr_b = kr.astype(_BF16)
        for hd in range(N_HEADS):
            sl = slice(hd * QK_NOPE_DIM, (hd + 1) * QK_NOPE_DIM)
            kh = kexp[:, sl]
            k_ref[hd, :, 0:QK_NOPE_DIM] = (kh * _rms(kh, QK_NOPE_DIM) * kng_ref[...]).astype(_BF16)
            k_ref[hd, :, QK_NOPE_DIM:QK_HEAD_DIM] = kr_b
            v_ref[hd] = vals_t[hd * V_HEAD_DIM:(hd + 1) * V_HEAD_DIM, :].astype(_BF16)

    q_scale = ATTN_SCALE if sample else ATTN_SCALE * LOG2E
    for hd in range(N_HEADS):
        sl = slice(hd * QK_NOPE_DIM, (hd + 1) * QK_NOPE_DIM)
        qh = q_nope[:, sl]
        qh = qh * _rms(qh, QK_NOPE_DIM) * qng_ref[...] * q_scale
        rs = slice(hd * ROPE_SLOT, (hd + 1) * ROPE_SLOT)
        qr = _rope_slot(q_rope[:, rs], q_rope_s[:, rs], qrg_ref[...], qrgs_ref[...], cos, sin)
        qr = qr[:, :QK_ROPE_DIM] * q_scale
        if sample:
            qa = _dot_t((qh * kng_ref[...]).astype(_BF16), wuk_ref[:, sl])
            qcat_ref[hd, :, 0:KV_LORA_RANK] = qa
            qcat_ref[hd, :, KV_LORA_RANK:KV_LORA_RANK + QK_ROPE_DIM] = qr
        else:
            q_ref[hd, :, 0:QK_NOPE_DIM] = qh.astype(_BF16)
            q_ref[hd, :, QK_NOPE_DIM:QK_HEAD_DIM] = qr.astype(_BF16)

    u = _gelu(u)
    v = _gelu(v)
    row = lax.broadcasted_iota(jnp.int32, (CHUNK, CHUNK), 0)
    col = lax.broadcasted_iota(jnp.int32, (CHUNK, CHUNK), 1)
    for g in range(GMLP_GROUPS):
        gs = slice(g * GMLP_GROUP_DIM, (g + 1) * GMLP_GROUP_DIM)
        vg = v[:, gs]
        vg = vg * _rms(vg, GMLP_GROUP_DIM) * vg_ref[:, gs]
        if sample:
            vn_ref[:, gs] = vg
        vg_b = vg.astype(_BF16)
        w = jnp.where(col <= row, ws_ref[g], 0.0).astype(_BF16)
        for c in range(tm // CHUNK):
            cs = slice(c * CHUNK, (c + 1) * CHUNK)
            mix = _dot(w, vg_b[cs]) + bs_ref[g]
            gm_scr[cs, gs] = (u[cs, gs] * mix).astype(_BF16)
    gout_ref[...] = (jax.nn.sigmoid(gate_g) * _dot(gm_scr[...], wog_ref[...])).astype(gout_ref.dtype)
    gate_ref[...] = jax.nn.sigmoid(gate_a).astype(gate_ref.dtype)


def _const_spec(shape):
    nd = len(shape)
    return pl.BlockSpec(shape, lambda i: (0,) * nd, pipeline_mode=pl.Buffered(1))


def _proj(sample, x, w, cos, sin, n_pos_tiles):
    n = x.shape[0]
    tm = PROJ_TM
    tok = lambda width: pl.BlockSpec((tm, width), lambda i: (i, 0))
    heads = lambda width: pl.BlockSpec((N_HEADS, tm, width), lambda i: (0, i, 0))
    weights = [w["mix_norm"], w["w_in_t"], w["w_in_kr_t"], w["q_lora_norm"], w["w_uq"],
               w["kv_lora_norm"], w["w_uk"],
               w["w_uv_t"], w["q_nope_norm"], w["q_rope_norm"], w["q_rope_norm_s"], w["k_nope_norm"],
               w["k_rope_norm"], w["k_rope_norm_s"], w["gmlp_v_norm"], w["gmlp_w_s"], w["gmlp_b_s"],
               w["w_o_gmlp"]]
    pos_spec = pl.BlockSpec((tm, ROPE_SLOT), lambda i: (i % n_pos_tiles, 0))
    in_specs = [tok(D_MODEL)] + [_const_spec(a.shape) for a in weights] + [pos_spec, pos_spec]
    sds = jax.ShapeDtypeStruct
    if sample:
        out_shape = [sds((N_HEADS, n, KV_LORA_RANK + QK_ROPE_DIM), _F32), sds((n, KV_LORA_RANK), _F32),
                     sds((n, QK_ROPE_DIM), _F32), sds((n, GMLP_WIDTH), _F32),
                     sds((n, D_MODEL), _BF16), sds((n, D_MODEL), _BF16)]
        out_specs = [heads(KV_LORA_RANK + QK_ROPE_DIM), tok(KV_LORA_RANK), tok(QK_ROPE_DIM),
                     tok(GMLP_WIDTH), tok(D_MODEL), tok(D_MODEL)]
    else:
        out_shape = [sds((N_HEADS, n, QK_HEAD_DIM), _BF16), sds((N_HEADS, n, QK_HEAD_DIM), _BF16),
                     sds((N_HEADS, V_HEAD_DIM, n), _BF16), sds((n, KV_LORA_RANK), _F32),
                     sds((n, QK_ROPE_DIM), _F32), sds((n, D_MODEL), _BF16), sds((n, D_MODEL), _BF16)]
        vt_spec = pl.BlockSpec((N_HEADS, V_HEAD_DIM, tm), lambda i: (0, 0, i))
        out_specs = [heads(QK_HEAD_DIM), heads(QK_HEAD_DIM), vt_spec, tok(KV_LORA_RANK),
                     tok(QK_ROPE_DIM), tok(D_MODEL), tok(D_MODEL)]
    return pl.pallas_call(
        functools.partial(_proj_kernel, sample),
        grid=(n // tm,),
        in_specs=in_specs,
        out_specs=out_specs,
        out_shape=out_shape,
        scratch_shapes=[pltpu.VMEM((tm, GMLP_WIDTH), _BF16)],
        compiler_params=pltpu.CompilerParams(
            dimension_semantics=("parallel",), vmem_limit_bytes=VMEM_LIMIT),
        name="proj_sample" if sample else "proj_prompt",
    )(x, *weights, cos, sin)


def _softmax_step(s, v_b, m_scr, l_scr, acc_scr):
    m_prev = m_scr[...]
    m_new = jnp.maximum(m_prev, jnp.max(s, axis=-1, keepdims=True))
    alpha = jnp.exp(m_prev - m_new)
    p = jnp.exp(s - m_new)
    l_scr[...] = alpha * l_scr[...] + jnp.sum(p, axis=-1, keepdims=True)
    acc_scr[...] = alpha * acc_scr[...] + _dot(p.astype(_BF16), v_b)
    m_scr[...] = m_new


def _pattn_kernel(q_ref, k_ref, vt_ref, o_ref, sa_scr, sb_scr, m_scr, acc_scr):
    qi = pl.program_id(2)
    m_scr[...] = jnp.full_like(m_scr, NEG_INF)
    acc_scr[...] = jnp.zeros_like(acc_scr)

    def scores(kb, buf, masked):
        off = pl.multiple_of(kb * ATT_TK, ATT_TK)
        for hh in range(ATT_HB):
            s = _dot_t(k_ref[hh, pl.ds(off, ATT_TK), :], q_ref[hh])
            if masked:
                key = lax.broadcasted_iota(jnp.int32, s.shape, 0)
                qry = lax.broadcasted_iota(jnp.int32, s.shape, 1)
                s = jnp.where(key <= qry, s, NEG_INF)
            buf[hh] = s

    def update(kb, buf):
        off = pl.multiple_of(kb * ATT_TK, ATT_TK)
        for hh in range(ATT_HB):
            s = buf[hh]
            m_prev = m_scr[hh]
            m_new = jnp.maximum(m_prev, jnp.max(s, axis=0, keepdims=True))
            p = jnp.exp2(s - m_new).astype(_BF16)
            vt = vt_ref[hh, :, pl.ds(off, ATT_TK)]
            vt1 = jnp.concatenate([vt, jnp.ones((ATT_ONES_ROWS, ATT_TK), _BF16)], axis=0)
            acc_scr[hh] = jnp.exp2(m_prev - m_new) * acc_scr[hh] + _dot(vt1, p)
            m_scr[hh] = m_new

    def trip(t, ybuf, xbuf, masked):
        scores(t, xbuf, masked)
        update(t - 1, ybuf)

    @pl.when(qi == 0)
    def _():
        scores(0, sa_scr, True)
        update(0, sa_scr)

    @pl.when(qi > 0)
    def _():
        scores(0, sa_scr, False)

        def pair(j, carry):
            trip(2 * j + 1, sa_scr, sb_scr, False)
            trip(2 * j + 2, sb_scr, sa_scr, False)
            return carry

        lax.fori_loop(0, lax.shift_right_logical(qi - 1, 1), pair, 0)

        @pl.when((qi & 1) == 1)
        def _():
            trip(qi, sa_scr, sb_scr, True)
            update(qi, sb_scr)

        @pl.when((qi & 1) == 0)
        def _():
            trip(qi - 1, sa_scr, sb_scr, False)
            trip(qi, sb_scr, sa_scr, True)
            update(qi, sa_scr)

    for hh in range(ATT_HB):
        acc = acc_scr[hh]
        o_t = acc[:V_HEAD_DIM] / acc[V_HEAD_DIM:V_HEAD_DIM + 1]
        o_ref[:, hh * V_HEAD_DIM:(hh + 1) * V_HEAD_DIM] = o_t.T.astype(o_ref.dtype)


def _prompt_attention(q, k, vt, batch, seq):
    assert ATT_TQ == ATT_TK
    nq = seq // ATT_TQ
    hb = ATT_HB
    return pl.pallas_call(
        _pattn_kernel,
        grid=(batch, N_HEADS // hb, nq),
        in_specs=[
            pl.BlockSpec((hb, ATT_TQ, QK_HEAD_DIM), lambda b, h, i: (h, b * nq + i, 0)),
            pl.BlockSpec((hb, seq, QK_HEAD_DIM), lambda b, h, i: (h, b, 0)),
            pl.BlockSpec((hb, V_HEAD_DIM, seq), lambda b, h, i: (h, 0, b)),
        ],
        out_specs=pl.BlockSpec((ATT_TQ, hb * V_HEAD_DIM), lambda b, h, i: (b * nq + i, h)),
        out_shape=jax.ShapeDtypeStruct((batch * seq, N_HEADS * V_HEAD_DIM), _BF16),
        scratch_shapes=[pltpu.VMEM((hb, ATT_TK, ATT_TQ), _F32),
                        pltpu.VMEM((hb, ATT_TK, ATT_TQ), _F32),
                        pltpu.VMEM((hb, 1, ATT_TQ), _F32),
                        pltpu.VMEM((hb, V_HEAD_DIM + ATT_ONES_ROWS, ATT_TQ), _F32)],
        compiler_params=pltpu.CompilerParams(
            dimension_semantics=("parallel", "parallel", "arbitrary"), vmem_limit_bytes=VMEM_LIMIT),
        name="prompt_attention",
    )(q, k, vt)


def _sattn_kernel(n_pages, pt_ref, qcat_ref, cnew_ref, krnew_ref, wukt_ref, lat_hbm, ropet_hbm, o_ref,
                  a_scr, lat_buf, ropet_buf, sca_scr, scb_scr, scl_scr, cbl_scr, sem_lat, sem_rope,
                  m_scr, l_scr, acc_scr):
    pages = SATT_PAGES_PER_CHUNK
    n_chunks = n_pages // pages
    tk = pages * PAGE_SIZE
    nk = N_HEADS * QK_NOPE_DIM
    nrow = N_HEADS * 8
    s = pl.program_id(0)
    n_seq = pl.num_programs(0) - 1
    slot = lax.rem(s, 2)

    def copies(seq, dst_slot):
        out = []
        for g in range(n_pages):
            page = 0 if seq is None else pt_ref[seq * n_pages + g]
            dst = pl.ds(g * PAGE_SIZE, PAGE_SIZE)
            out.append(pltpu.make_async_copy(lat_hbm.at[page], lat_buf.at[dst_slot, dst], sem_lat.at[dst_slot]))
            out.append(pltpu.make_async_copy(ropet_hbm.at[page], ropet_buf.at[dst_slot, :, dst],
                                             sem_rope.at[dst_slot]))
        return out

    @pl.when(s == 0)
    def _():
        a_scr[0:nk, :] = wukt_ref[...]
        scl_scr[...] = jnp.zeros_like(scl_scr)
        cbl_scr[...] = jnp.zeros_like(cbl_scr)
        m_scr[...] = jnp.zeros_like(m_scr)
        l_scr[...] = jnp.ones_like(l_scr)
        acc_scr[...] = jnp.zeros_like(acc_scr)
        for d in copies(0, 0):
            d.start()

    @pl.when(s < n_seq)
    def _():
        for d in copies(jnp.minimum(s + 1, n_seq - 1), 1 - slot):
            d.start()

    for d in copies(None, slot):
        d.wait()

    q2 = qcat_ref[...].reshape(nrow, KV_LORA_RANK + QK_ROPE_DIM)
    a_scr[nk:nk + nrow, :] = q2[:, :KV_LORA_RANK].astype(_BF16)
    q_rope = q2[:, KV_LORA_RANK:].astype(_BF16)

    def scores(c_b, s_rope):
        big = _dot_t(a_scr[...], c_b)
        rows = []
        for hd in range(N_HEADS):
            kx = big[hd * QK_NOPE_DIM:(hd + 1) * QK_NOPE_DIM, :]
            r = lax.rsqrt(jnp.sum(kx * kx, axis=0, keepdims=True) * (1.0 / QK_NOPE_DIM) + EPS)
            rows.append(big[nk + 8 * hd:nk + 8 * hd + 8, :] * r + s_rope[8 * hd:8 * hd + 8, :])
        return jnp.concatenate(rows, axis=0)

    sc_bufs = [sca_scr, scb_scr] * (n_chunks // 2 + 1)
    sc_bufs = sc_bufs[:n_chunks - 1] + [scl_scr]
    pad = PAGE_SIZE - cnew_ref.shape[0]

    def latent(c):
        return lat_buf[slot, pl.ds(c * tk, tk), :].astype(_BF16)

    def stage_scores(c):
        s_rope = _dot(q_rope, ropet_buf[slot, :, pl.ds(c * tk, tk)].astype(_BF16))
        if c < n_chunks - 1:
            sc_bufs[c][...] = scores(latent(c), s_rope)
        else:
            c_new = jnp.concatenate([cnew_ref[...], jnp.zeros((pad, KV_LORA_RANK), _F32)], axis=0)
            c_b = jnp.concatenate([latent(c), c_new.astype(_BF16)], axis=0)
            cbl_scr[...] = c_b
            kr_new = jnp.concatenate([krnew_ref[...], jnp.zeros((pad, QK_ROPE_DIM), _F32)], axis=0)
            s_rope = jnp.concatenate([s_rope, _dot_t(q_rope, kr_new.astype(_BF16))], axis=1)
            row = lax.broadcasted_iota(jnp.int32, (nrow, tk + PAGE_SIZE), 0)
            col = lax.broadcasted_iota(jnp.int32, (nrow, tk + PAGE_SIZE), 1)
            sc_bufs[c][...] = jnp.where(col - tk <= (row & 7), scores(c_b, s_rope), NEG_INF)

    def stage_update(c):
        _softmax_step(sc_bufs[c][...], latent(c), m_scr, l_scr, acc_scr)

    stage_scores(0)

    _softmax_step(scl_scr[...], cbl_scr[...], m_scr, l_scr, acc_scr)
    o_ref[...] = (acc_scr[...] / l_scr[...]).reshape(o_ref.shape)
    m_scr[...] = jnp.full_like(m_scr, NEG_INF)
    l_scr[...] = jnp.zeros_like(l_scr)
    acc_scr[...] = jnp.zeros_like(acc_scr)

    for c in range(1, n_chunks):
        stage_scores(c)
        stage_update(c - 1)


def _sample_attention(page_table, qcat, c_new, kr_new, wukt, cache_lat, cache_rope_t, t_new):
    n_seq, n_pages = page_table.shape
    assert t_new == 8 and n_pages % SATT_PAGES_PER_CHUNK == 0 and n_pages // SATT_PAGES_PER_CHUNK >= 2
    tk = SATT_PAGES_PER_CHUNK * PAGE_SIZE
    past = n_pages * PAGE_SIZE
    nrow = N_HEADS * t_new
    dq = KV_LORA_RANK + QK_ROPE_DIM
    cur = lambda s: jnp.minimum(s, n_seq - 1)
    prev = lambda s: jnp.maximum(s - 1, 0)
    grid_spec = pltpu.PrefetchScalarGridSpec(
        num_scalar_prefetch=1,
        grid=(n_seq + 1,),
        in_specs=[
            pl.BlockSpec((N_HEADS, t_new, dq), lambda s, pt: (0, cur(s), 0)),
            pl.BlockSpec((t_new, KV_LORA_RANK), lambda s, pt: (cur(s), 0)),
            pl.BlockSpec((t_new, QK_ROPE_DIM), lambda s, pt: (cur(s), 0)),
            pl.BlockSpec((N_HEADS * QK_NOPE_DIM, KV_LORA_RANK), lambda s, pt: (0, 0)),
            pl.BlockSpec(memory_space=pl.ANY),
            pl.BlockSpec(memory_space=pl.ANY),
        ],
        out_specs=pl.BlockSpec((N_HEADS, t_new, KV_LORA_RANK), lambda s, pt: (0, prev(s), 0)),
        scratch_shapes=[
            pltpu.VMEM((N_HEADS * QK_NOPE_DIM + nrow, KV_LORA_RANK), _BF16),
            pltpu.VMEM((2, past, KV_LORA_RANK), _F32),
            pltpu.VMEM((2, QK_ROPE_DIM, past), _F32),
            pltpu.VMEM((nrow, tk), _F32),
            pltpu.VMEM((nrow, tk), _F32),
            pltpu.VMEM((nrow, tk + PAGE_SIZE), _F32),
            pltpu.VMEM((tk + PAGE_SIZE, KV_LORA_RANK), _BF16),
            pltpu.SemaphoreType.DMA((2,)),
            pltpu.SemaphoreType.DMA((2,)),
            pltpu.VMEM((nrow, 1), _F32),
            pltpu.VMEM((nrow, 1), _F32),
            pltpu.VMEM((nrow, KV_LORA_RANK), _F32),
        ],
    )
    return pl.pallas_call(
        functools.partial(_sattn_kernel, n_pages),
        grid_spec=grid_spec,
        out_shape=jax.ShapeDtypeStruct((N_HEADS, n_seq * t_new, KV_LORA_RANK), _F32),
        compiler_params=pltpu.CompilerParams(
            dimension_semantics=("arbitrary",), vmem_limit_bytes=VMEM_LIMIT),
        name="sample_attention",
    )(page_table.reshape(-1), qcat, c_new, kr_new, wukt, cache_lat, cache_rope_t)


def _merge_kernel(from_latent, a_ref, gate_ref, gout_ref, x_ref, wuv_ref, woa_ref, wout_ref, o_ref):
    if from_latent:
        heads = [_dot(a_ref[hd].astype(_BF16), wuv_ref[:, hd * V_HEAD_DIM:(hd + 1) * V_HEAD_DIM])
                 for hd in range(N_HEADS)]
        attn = jnp.concatenate(heads, axis=-1).astype(_BF16)
    else:
        attn = a_ref[...]
    merged = gate_ref[...].astype(_F32) * _dot(attn, woa_ref[...]) + gout_ref[...]
    o_ref[...] = x_ref[...] + _dot(merged.astype(_BF16), wout_ref[...])


def _merge(from_latent, attn, gate, gout, x, wuv, woa, wout):
    n = x.shape[0]
    tm = MERGE_TM
    tok = pl.BlockSpec((tm, D_MODEL), lambda i: (i, 0))
    if from_latent:
        a_spec = pl.BlockSpec((N_HEADS, tm, KV_LORA_RANK), lambda i: (0, i, 0))
    else:
        a_spec = tok
    return pl.pallas_call(
        functools.partial(_merge_kernel, from_latent),
        grid=(n // tm,),
        in_specs=[a_spec, tok, tok, tok, _const_spec(wuv.shape), _const_spec(woa.shape),
                  _const_spec(wout.shape)],
        out_specs=tok,
        out_shape=jax.ShapeDtypeStruct((n, D_MODEL), _F32),
        compiler_params=pltpu.CompilerParams(
            dimension_semantics=("parallel",), vmem_limit_bytes=VMEM_LIMIT),
        name="merge_sample" if from_latent else "merge_prompt",
    )(attn, gate, gout, x, wuv, woa, wout)


def _swap_halves(a, axis):
    lo, hi = jnp.split(a, 2, axis=axis)
    return jnp.concatenate([hi, lo], axis=axis)


def _slot(a):
    return jnp.concatenate([a, jnp.zeros_like(a)], axis=-1)


def _rope_tables(pos):
    half = QK_ROPE_DIM // 2
    inv = ROPE_BASE ** (-jnp.arange(half, dtype=_F32) / half)
    ang = pos.astype(_F32)[:, None] * inv[None, :]
    cos, sin = jnp.cos(ang), jnp.sin(ang)
    return _slot(jnp.concatenate([cos, cos], axis=-1)), _slot(jnp.concatenate([-sin, sin], axis=-1))


def _prep_weights(w_in, w_uq, w_uk, w_uv, w_o_gmlp, mix_norm, q_lora_norm, kv_lora_norm, q_nope_norm,
                  q_rope_norm, k_nope_norm, k_rope_norm, gmlp_v_norm):
    w_uq = w_uq.astype(_BF16)
    off_kr = Q_LORA_RANK + KV_LORA_RANK
    w_in_t = w_in.T.astype(_BF16)
    k_r_t = w_in_t[off_kr:off_kr + QK_ROPE_DIM]
    zero = jnp.zeros_like(k_r_t)
    w_in_kr_t = jnp.concatenate([k_r_t, zero, _swap_halves(k_r_t, 0), zero], axis=0)
    assert w_in_kr_t.shape[0] == 2 * ROPE_SLOT and _D_IN_PADDED == _OFF_KR + 2 * ROPE_SLOT
    wq = w_uq.reshape(Q_LORA_RANK, N_HEADS, QK_HEAD_DIM)
    wq_nope = wq[:, :, :QK_NOPE_DIM].reshape(Q_LORA_RANK, -1)
    wq_rope = wq[:, :, QK_NOPE_DIM:]
    w_uq_p = jnp.concatenate([wq_nope, _slot(wq_rope).reshape(Q_LORA_RANK, -1),
                              _slot(_swap_halves(wq_rope, -1)).reshape(Q_LORA_RANK, -1)], axis=-1)
    row = lambda a: a.reshape(1, -1).astype(_F32)
    return dict(
        mix_norm=row(mix_norm), w_in_t=w_in_t, w_in_kr_t=w_in_kr_t, q_lora_norm=row(q_lora_norm),
        w_uq=w_uq_p.astype(_BF16), kv_lora_norm=row(kv_lora_norm), w_uk=w_uk.astype(_BF16),
        w_uv=w_uv.astype(_BF16), w_uv_t=w_uv.T.astype(_BF16), q_nope_norm=row(q_nope_norm), q_rope_norm=row(_slot(q_rope_norm)),
        q_rope_norm_s=row(_slot(_swap_halves(q_rope_norm, -1))), k_nope_norm=row(k_nope_norm),
        k_rope_norm=row(_slot(k_rope_norm)), k_rope_norm_s=row(_slot(_swap_halves(k_rope_norm, -1))),
        gmlp_v_norm=row(gmlp_v_norm), w_o_gmlp=w_o_gmlp.astype(_BF16))


def kernel(x_prompt, x_sample, cache_kv_latent, cache_k_rope, page_table, ffn1_norm, ffn1_w_gate, ffn1_w_up, ffn1_w_down, mix_norm, w_in, q_lora_norm, w_uq, kv_lora_norm, w_uk, w_uv, q_nope_norm, q_rope_norm, k_nope_norm, k_rope_norm, gmlp_v_norm, gmlp_w_s, gmlp_b_s, w_o_attn, w_o_gmlp, w_out, ffn2_norm, ffn2_w_gate, ffn2_w_up, ffn2_w_down):
    batch, seq, _ = x_prompt.shape
    n_seq, t_new, _ = x_sample.shape
    assert ffn1_norm.shape[0] == 1
    past = page_table.shape[1] * PAGE_SIZE
    n_pool = cache_kv_latent.shape[1]
    l = 0
    row = lambda a: a.reshape(1, -1).astype(_F32)
    w = _prep_weights(w_in[l], w_uq[l], w_uk[l], w_uv[l], w_o_gmlp[l], mix_norm[l], q_lora_norm[l],
                      kv_lora_norm[l], q_nope_norm[l], q_rope_norm[l], k_nope_norm[l], k_rope_norm[l],
                      gmlp_v_norm[l])
    ffn1 = (row(ffn1_norm[l]), ffn1_w_gate[l].astype(_BF16), ffn1_w_up[l].astype(_BF16),
            ffn1_w_down[l].astype(_BF16))
    ffn2 = (row(ffn2_norm[l]), ffn2_w_gate[l].astype(_BF16), ffn2_w_up[l].astype(_BF16),
            ffn2_w_down[l].astype(_BF16))
    woa = w_o_attn[l].astype(_BF16)
    wout = w_out[l].astype(_BF16)

    wp = dict(w, gmlp_w_s=gmlp_w_s[l].astype(_F32), gmlp_b_s=gmlp_b_s[l].reshape(GMLP_GROUPS, CHUNK, 1))
    cos_p, sin_p = _rope_tables(jnp.arange(seq))
    xp = _ffn(x_prompt.reshape(batch * seq, D_MODEL), *ffn1)
    q, k, v, ckv_p, kr_p, gate_p, gout_p = _proj(False, xp, wp, cos_p, sin_p, seq // PROJ_TM)
    attn_p = _prompt_attention(q, k, v, batch, seq)
    xp = _merge(False, attn_p, gate_p, gout_p, xp, w["w_uv"], woa, wout)
    xp = _ffn(xp, *ffn2)

    reps = CHUNK // t_new
    ws_s = jax.vmap(lambda m: jnp.kron(jnp.eye(reps, dtype=_F32), m))(gmlp_w_s[l][:, :t_new, :t_new])
    bs_s = jnp.tile(gmlp_b_s[l][:, :t_new], (1, reps)).reshape(GMLP_GROUPS, CHUNK, 1)
    ws = dict(w, gmlp_w_s=ws_s, gmlp_b_s=bs_s)
    cos_s, sin_s = _rope_tables(past + jnp.arange(PROJ_TM) % t_new)
    xs = _ffn(x_sample.reshape(n_seq * t_new, D_MODEL), *ffn1)
    qcat, ckv_s, kr_s, vn_s, gate_s, gout_s = _proj(True, xs, ws, cos_s, sin_s, 1)
    o_lat = _sample_attention(page_table, qcat, ckv_s, kr_s, w["w_uk"].T,
                              cache_kv_latent.reshape(n_pool, PAGE_SIZE, KV_LORA_RANK),
                              jnp.swapaxes(cache_k_rope.reshape(n_pool, PAGE_SIZE, QK_ROPE_DIM), 1, 2),
                              t_new)
    xs = _merge(True, o_lat, gate_s, gout_s, xs, w["w_uv"], woa, wout)
    xs = _ffn(xs, *ffn2)

    return (xp.reshape(batch, seq, D_MODEL), xs.reshape(n_seq, t_new, D_MODEL),
            ckv_p.reshape(1, batch, seq, KV_LORA_RANK), kr_p.reshape(1, batch, seq, QK_ROPE_DIM),
            ckv_s.reshape(1, n_seq, t_new, KV_LORA_RANK), kr_s.reshape(1, n_seq, t_new, QK_ROPE_DIM),
            vn_s.reshape(1, n_seq, t_new, GMLP_WIDTH))
```

```python
import functools

import jax
import jax.numpy as jnp
import numpy as np
from jax import lax
from jax.experimental import pallas as pl
from jax.experimental.pallas import tpu as pltpu

D_MODEL = 1024
N_HEADS = 8
QK_NOPE_DIM = 128
QK_ROPE_DIM = 64
QK_HEAD_DIM = QK_NOPE_DIM + QK_ROPE_DIM
V_HEAD_DIM = 128
Q_LORA_RANK = 256
KV_LORA_RANK = 256
ROPE_BASE = 10000.0
ATTN_SCALE = QK_HEAD_DIM ** -0.5
LOG2E = 1.4426950408889634
GMLP_GROUPS = 4
GMLP_WIDTH = 1024
GMLP_GROUP_DIM = GMLP_WIDTH // GMLP_GROUPS
CHUNK = 128
PAGE_SIZE = 128
D_FF = 2816
EPS = 1e-6
NEG_INF = -1e30

LANE = 128
ROPE_SLOT = LANE

_OFF_CQ = 0
_OFF_CKV = _OFF_CQ + Q_LORA_RANK
_OFF_U = _OFF_CKV + KV_LORA_RANK
_OFF_V = _OFF_U + GMLP_WIDTH
_OFF_GA = _OFF_V + GMLP_WIDTH
_OFF_GG = _OFF_GA + D_MODEL
_OFF_KR = _OFF_GG + D_MODEL
_OFF_KRS = _OFF_KR + ROPE_SLOT
_D_IN_PADDED = _OFF_KRS + ROPE_SLOT

FFN_TM = 1024
FFN_TF = 256
PROJ_TM = 512
ATT_TQ = 512
ATT_TK = 512
ATT_HB = 4
ATT_ONES_ROWS = 16
MERGE_TM = 1024
SATT_PAGES_PER_CHUNK = 16
VMEM_LIMIT = 56 * 1024 * 1024

_BF16 = jnp.bfloat16
_F32 = jnp.float32


def _dot(a, b):
    return jnp.dot(a, b, preferred_element_type=_F32)


def _dot_t(a, b):
    return lax.dot_general(a, b, (((1,), (1,)), ((), ())), preferred_element_type=_F32)


def _rms(x, n):
    return lax.rsqrt(jnp.sum(x * x, axis=-1, keepdims=True) * (1.0 / n) + EPS)


def _ffn_kernel(x_ref, g_ref, wg_ref, wu_ref, wd_ref, o_ref, act_scr):
    x = x_ref[...]
    h = (x * _rms(x, D_MODEL) * g_ref[...]).astype(_BF16)
    for j in range(D_FF // FFN_TF):
        sl = slice(j * FFN_TF, (j + 1) * FFN_TF)
        gate = _dot(h, wg_ref[:, sl])
        up = _dot(h, wu_ref[:, sl])
        act_scr[:, sl] = (gate * jax.nn.sigmoid(gate) * up).astype(_BF16)
    o_ref[...] = x_ref[...] + 0.5 * _dot(act_scr[...], wd_ref[...])


def _ffn(x, norm_g, wg, wu, wd):
    n = x.shape[0]
    tm = min(FFN_TM, n)
    return pl.pallas_call(
        _ffn_kernel,
        grid=(n // tm,),
        in_specs=[
            pl.BlockSpec((tm, D_MODEL), lambda i: (i, 0)),
            _const_spec(norm_g.shape), _const_spec(wg.shape), _const_spec(wu.shape), _const_spec(wd.shape),
        ],
        out_specs=pl.BlockSpec((tm, D_MODEL), lambda i: (i, 0)),
        out_shape=jax.ShapeDtypeStruct((n, D_MODEL), _F32),
        scratch_shapes=[pltpu.VMEM((tm, D_FF), _BF16)],
        compiler_params=pltpu.CompilerParams(
            dimension_semantics=("parallel",), vmem_limit_bytes=VMEM_LIMIT),
        name="ffn",
    )(x, norm_g, wg, wu, wd)


def _rope_slot(x, xs, g, gs, cos, sin):
    r = _rms(x, QK_ROPE_DIM)
    return (x * r * g) * cos + (xs * r * gs) * sin


def _gelu(x):
    return 0.5 * x * (1.0 + lax.erf(x * (2.0 ** -0.5)))


def _proj_kernel(sample, x_ref, mixg_ref, wt_ref, wkr_ref, qlg_ref, wuq_ref, kvg_ref, wuk_ref, wuv_ref,
                 qng_ref, qrg_ref, qrgs_ref, kng_ref, krg_ref, krgs_ref, vg_ref, ws_ref, bs_ref,
                 wog_ref, cos_ref, sin_ref, *rest):
    if sample:
        qcat_ref, ckv_ref, kr_ref, vn_ref, gate_ref, gout_ref, gm_scr = rest
    else:
        q_ref, k_ref, v_ref, ckv_ref, kr_ref, gate_ref, gout_ref, gm_scr = rest
    tm = x_ref.shape[0]
    x = x_ref[...]
    h = (x * _rms(x, D_MODEL) * mixg_ref[...]).astype(_BF16)
    cos = cos_ref[...]
    sin = sin_ref[...]

    def win(off, width):
        if off >= _OFF_KR:
            assert (off, width) == (_OFF_KR, 2 * ROPE_SLOT)
            return _dot_t(h, wkr_ref[...])
        row = off if off < _OFF_U else off + QK_ROPE_DIM
        return _dot_t(h, wt_ref[row:row + width, :])

    kr_raw = win(_OFF_KR, 2 * ROPE_SLOT)
    ckv = win(_OFF_CKV, KV_LORA_RANK)
    cq = win(_OFF_CQ, Q_LORA_RANK)
    u = win(_OFF_U, GMLP_WIDTH)
    v = win(_OFF_V, GMLP_WIDTH)

    kr = _rope_slot(kr_raw[:, :ROPE_SLOT], kr_raw[:, ROPE_SLOT:], krg_ref[...], krgs_ref[...],
                    cos, sin)[:, :QK_ROPE_DIM]
    if sample:
        kr_ref[...] = kr
    else:
        kr_ref[0] = kr.T
    ckv = ckv * _rms(ckv, KV_LORA_RANK) * kvg_ref[...]
    ckv_ref[...] = ckv
    ckv_b = ckv.astype(_BF16)
    cq_b = (cq * _rms(cq, Q_LORA_RANK) * qlg_ref[...]).astype(_BF16)

    nq = N_HEADS * QK_NOPE_DIM
    nr = N_HEADS * ROPE_SLOT
    if not sample:
        kexp = _dot(ckv_b, wuk_ref[...])
        vals_t = _dot_t(wuv_ref[...], ckv_b)
    q_nope = _dot(cq_b, wuq_ref[:, 0:nq])
    q_rope = _dot(cq_b, wuq_ref[:, nq:nq + nr])
    q_rope_s = _dot(cq_b, wuq_ref[:, nq + nr:nq + 2 * nr])
    gate_g = jax.nn.sigmoid(win(_OFF_GG, D_MODEL))
    gate_ref[...] = jax.nn.sigmoid(win(_OFF_GA, D_MODEL)).astype(gate_ref.dtype)

    if not sample:
        kr_b = kr.astype(_BF16)
        for hd in range(N_HEADS):
            sl = slice(hd * QK_NOPE_DIM, (hd + 1) * QK_NOPE_DIM)
            kh = kexp[:, sl]
            k_ref[hd, :, 0:QK_NOPE_DIM] = (kh * _rms(kh, QK_NOPE_DIM) * kng_ref[...]).astype(_BF16)
            k_ref[hd, :, QK_NOPE_DIM:QK_HEAD_DIM] = kr_b
            v_ref[hd] = vals_t[hd * V_HEAD_DIM:(hd + 1) * V_HEAD_DIM, :].astype(_BF16)

    q_scale = ATTN_SCALE if sample else ATTN_SCALE * LOG2E
    for hd in range(N_HEADS):
        sl = slice(hd * QK_NOPE_DIM, (hd + 1) * QK_NOPE_DIM)
        qh = q_nope[:, sl]
        qh = qh * _rms(qh, QK_NOPE_DIM) * qng_ref[...] * q_scale
        rs = slice(hd * ROPE_SLOT, (hd + 1) * ROPE_SLOT)
        qr = _rope_slot(q_rope[:, rs], q_rope_s[:, rs], qrg_ref[...], qrgs_ref[...], cos, sin)
        qr = qr[:, :QK_ROPE_DIM] * q_scale
        if sample:
            qa = _dot_t((qh * kng_ref[...]).astype(_BF16), wuk_ref[:, sl])
            qcat_ref[hd, :, 0:KV_LORA_RANK] = qa
            qcat_ref[hd, :, KV_LORA_RANK:KV_LORA_RANK + QK_ROPE_DIM] = qr
        else:
            q_ref[hd, :, 0:QK_NOPE_DIM] = qh.astype(_BF16)
            q_ref[hd, :, QK_NOPE_DIM:QK_HEAD_DIM] = qr.astype(_BF16)

    u = _gelu(u)
    v = _gelu(v)
    row = lax.broadcasted_iota(jnp.int32, (CHUNK, CHUNK), 0)
    col = lax.broadcasted_iota(jnp.int32, (CHUNK, CHUNK), 1)
    for g in range(GMLP_GROUPS):
        gs = slice(g * GMLP_GROUP_DIM, (g + 1) * GMLP_GROUP_DIM)
        vg = v[:, gs]
        vg = vg * _rms(vg, GMLP_GROUP_DIM) * vg_ref[:, gs]
        if sample:
            vn_ref[:, gs] = vg
        vg_b = vg.astype(_BF16)
        w = jnp.where(col <= row, ws_ref[g], 0.0).astype(_BF16)
        for c in range(tm // CHUNK):
            cs = slice(c * CHUNK, (c + 1) * CHUNK)
            mix = _dot(w, vg_b[cs]) + bs_ref[g]
            gm_scr[cs, gs] = (u[cs, gs] * mix).astype(_BF16)
    gout_ref[...] = (gate_g * _dot(gm_scr[...], wog_ref[...])).astype(gout_ref.dtype)


def _const_spec(shape):
    nd = len(shape)
    return pl.BlockSpec(shape, lambda i: (0,) * nd, pipeline_mode=pl.Buffered(1))


def _proj(sample, x, w, cos, sin, n_pos_tiles):
    n = x.shape[0]
    tm = PROJ_TM
    tok = lambda width: pl.BlockSpec((tm, width), lambda i: (i, 0))
    heads = lambda width: pl.BlockSpec((N_HEADS, tm, width), lambda i: (0, i, 0))
    weights = [w["mix_norm"], w["w_in_t"], w["w_in_kr_t"], w["q_lora_norm"], w["w_uq"],
               w["kv_lora_norm"], w["w_uk"],
               w["w_uv_t"], w["q_nope_norm"], w["q_rope_norm"], w["q_rope_norm_s"], w["k_nope_norm"],
               w["k_rope_norm"], w["k_rope_norm_s"], w["gmlp_v_norm"], w["gmlp_w_s"], w["gmlp_b_s"],
               w["w_o_gmlp"]]
    pos_spec = pl.BlockSpec((tm, ROPE_SLOT), lambda i: (i % n_pos_tiles, 0))
    in_specs = [tok(D_MODEL)] + [_const_spec(a.shape) for a in weights] + [pos_spec, pos_spec]
    sds = jax.ShapeDtypeStruct
    if sample:
        out_shape = [sds((N_HEADS, n, KV_LORA_RANK + QK_ROPE_DIM), _F32), sds((n, KV_LORA_RANK), _F32),
                     sds((n, QK_ROPE_DIM), _F32), sds((n, GMLP_WIDTH), _F32),
                     sds((n, D_MODEL), _BF16), sds((n, D_MODEL), _BF16)]
        out_specs = [heads(KV_LORA_RANK + QK_ROPE_DIM), tok(KV_LORA_RANK), tok(QK_ROPE_DIM),
                     tok(GMLP_WIDTH), tok(D_MODEL), tok(D_MODEL)]
    else:
        out_shape = [sds((N_HEADS, n, QK_HEAD_DIM), _BF16), sds((N_HEADS, n, QK_HEAD_DIM), _BF16),
                     sds((N_HEADS, V_HEAD_DIM, n), _BF16), sds((n, KV_LORA_RANK), _F32),
                     sds((n // (n_pos_tiles * tm), QK_ROPE_DIM, n_pos_tiles * tm), _F32),
                     sds((n, D_MODEL), _BF16), sds((n, D_MODEL), _BF16)]
        vt_spec = pl.BlockSpec((N_HEADS, V_HEAD_DIM, tm), lambda i: (0, 0, i))
        krt_spec = pl.BlockSpec((1, QK_ROPE_DIM, tm), lambda i: (i // n_pos_tiles, 0, i % n_pos_tiles))
        out_specs = [heads(QK_HEAD_DIM), heads(QK_HEAD_DIM), vt_spec, tok(KV_LORA_RANK),
                     krt_spec, tok(D_MODEL), tok(D_MODEL)]
    return pl.pallas_call(
        functools.partial(_proj_kernel, sample),
        grid=(n // tm,),
        in_specs=in_specs,
        out_specs=out_specs,
        out_shape=out_shape,
        scratch_shapes=[pltpu.VMEM((tm, GMLP_WIDTH), _BF16)],
        compiler_params=pltpu.CompilerParams(
            dimension_semantics=("parallel",), vmem_limit_bytes=VMEM_LIMIT),
        name="proj_sample" if sample else "proj_prompt",
    )(x, *weights, cos, sin)


def _softmax_step(s, v_b, m_scr, l_scr, acc_scr):
    m_prev = m_scr[...]
    m_new = jnp.maximum(m_prev, jnp.max(s, axis=-1, keepdims=True))
    alpha = jnp.exp(m_prev - m_new)
    p = jnp.exp(s - m_new)
    l_scr[...] = alpha * l_scr[...] + jnp.sum(p, axis=-1, keepdims=True)
    acc_scr[...] = alpha * acc_scr[...] + _dot(p.astype(_BF16), v_b)
    m_scr[...] = m_new


def _pattn_kernel(q_ref, k_ref, vt_ref, o_ref, sa_scr, sb_scr, m_scr, acc_scr):
    qi = pl.program_id(2)
    m_scr[...] = jnp.full_like(m_scr, NEG_INF)
    acc_scr[...] = jnp.zeros_like(acc_scr)

    def scores(kb, buf, masked):
        off = pl.multiple_of(kb * ATT_TK, ATT_TK)
        for hh in range(ATT_HB):
            s = _dot_t(k_ref[hh, pl.ds(off, ATT_TK), :], q_ref[hh])
            if masked:
                key = lax.broadcasted_iota(jnp.int32, s.shape, 0)
                qry = lax.broadcasted_iota(jnp.int32, s.shape, 1)
                s = jnp.where(key <= qry, s, NEG_INF)
            buf[hh] = s

    def update(kb, buf):
        off = pl.multiple_of(kb * ATT_TK, ATT_TK)
        for hh in range(ATT_HB):
            s = buf[hh]
            m_prev = m_scr[hh]
            m_new = jnp.maximum(m_prev, jnp.max(s, axis=0, keepdims=True))
            p = jnp.exp2(s - m_new).astype(_BF16)
            vt = vt_ref[hh, :, pl.ds(off, ATT_TK)]
            vt1 = jnp.concatenate([vt, jnp.ones((ATT_ONES_ROWS, ATT_TK), _BF16)], axis=0)
            acc_scr[hh] = jnp.exp2(m_prev - m_new) * acc_scr[hh] + _dot(vt1, p)
            m_scr[hh] = m_new

    def trip(t, ybuf, xbuf, masked):
        scores(t, xbuf, masked)
        update(t - 1, ybuf)

    @pl.when(qi == 0)
    def _():
        scores(0, sa_scr, True)
        update(0, sa_scr)

    @pl.when(qi > 0)
    def _():
        scores(0, sa_scr, False)

        def pair(j, carry):
            trip(2 * j + 1, sa_scr, sb_scr, False)
            trip(2 * j + 2, sb_scr, sa_scr, False)
            return carry

        lax.fori_loop(0, lax.shift_right_logical(qi - 1, 1), pair, 0)

        @pl.when((qi & 1) == 1)
        def _():
            trip(qi, sa_scr, sb_scr, True)
            update(qi, sb_scr)

        @pl.when((qi & 1) == 0)
        def _():
            trip(qi - 1, sa_scr, sb_scr, False)
            trip(qi, sb_scr, sa_scr, True)
            update(qi, sa_scr)

    for hh in range(ATT_HB):
        acc = acc_scr[hh]
        o_t = acc[:V_HEAD_DIM] / acc[V_HEAD_DIM:V_HEAD_DIM + 1]
        o_ref[:, hh * V_HEAD_DIM:(hh + 1) * V_HEAD_DIM] = o_t.T.astype(o_ref.dtype)


def _prompt_attention(q, k, vt, batch, seq):
    assert ATT_TQ == ATT_TK
    nq = seq // ATT_TQ
    hb = ATT_HB
    return pl.pallas_call(
        _pattn_kernel,
        grid=(batch, N_HEADS // hb, nq),
        in_specs=[
            pl.BlockSpec((hb, ATT_TQ, QK_HEAD_DIM), lambda b, h, i: (h, b * nq + i, 0)),
            pl.BlockSpec((hb, seq, QK_HEAD_DIM), lambda b, h, i: (h, b, 0)),
            pl.BlockSpec((hb, V_HEAD_DIM, seq), lambda b, h, i: (h, 0, b)),
        ],
        out_specs=pl.BlockSpec((ATT_TQ, hb * V_HEAD_DIM), lambda b, h, i: (b * nq + i, h)),
        out_shape=jax.ShapeDtypeStruct((batch * seq, N_HEADS * V_HEAD_DIM), _BF16),
        scratch_shapes=[pltpu.VMEM((hb, ATT_TK, ATT_TQ), _F32),
                        pltpu.VMEM((hb, ATT_TK, ATT_TQ), _F32),
                        pltpu.VMEM((hb, 1, ATT_TQ), _F32),
                        pltpu.VMEM((hb, V_HEAD_DIM + ATT_ONES_ROWS, ATT_TQ), _F32)],
        compiler_params=pltpu.CompilerParams(
            dimension_semantics=("parallel", "parallel", "arbitrary"), vmem_limit_bytes=VMEM_LIMIT),
        name="prompt_attention",
    )(q, k, vt)


def _sattn_kernel(n_pages, pt_ref, qcat_ref, cnew_ref, krnew_ref, wukt_ref, lat_hbm, ropet_hbm, o_ref,
                  a_scr, lat_buf, ropet_buf, sca_scr, scb_scr, scl_scr, cbl_scr, sem_lat, sem_rope,
                  m_scr, l_scr, acc_scr):
    pages = SATT_PAGES_PER_CHUNK
    n_chunks = n_pages // pages
    tk = pages * PAGE_SIZE
    nk = N_HEADS * QK_NOPE_DIM
    nrow = N_HEADS * 8
    s = pl.program_id(0)
    n_seq = pl.num_programs(0) - 1
    slot = lax.rem(s, 2)

    def copies(seq, dst_slot):
        out = []
        for g in range(n_pages):
            page = 0 if seq is None else pt_ref[seq * n_pages + g]
            dst = pl.ds(g * PAGE_SIZE, PAGE_SIZE)
            out.append(pltpu.make_async_copy(lat_hbm.at[page], lat_buf.at[dst_slot, dst], sem_lat.at[dst_slot]))
            out.append(pltpu.make_async_copy(ropet_hbm.at[page], ropet_buf.at[dst_slot, :, dst],
                                             sem_rope.at[dst_slot]))
        return out

    @pl.when(s == 0)
    def _():
        a_scr[0:nk, :] = wukt_ref[...]
        scl_scr[...] = jnp.zeros_like(scl_scr)
        cbl_scr[...] = jnp.zeros_like(cbl_scr)
        m_scr[...] = jnp.zeros_like(m_scr)
        l_scr[...] = jnp.ones_like(l_scr)
        acc_scr[...] = jnp.zeros_like(acc_scr)
        for d in copies(0, 0):
            d.start()

    @pl.when(s < n_seq)
    def _():
        for d in copies(jnp.minimum(s + 1, n_seq - 1), 1 - slot):
            d.start()

    for d in copies(None, slot):
        d.wait()

    q2 = qcat_ref[...].reshape(nrow, KV_LORA_RANK + QK_ROPE_DIM)
    a_scr[nk:nk + nrow, :] = q2[:, :KV_LORA_RANK].astype(_BF16)
    q_rope = q2[:, KV_LORA_RANK:].astype(_BF16)

    def scores(c_b, s_rope):
        big = _dot_t(a_scr[...], c_b)
        rows = []
        for hd in range(N_HEADS):
            kx = big[hd * QK_NOPE_DIM:(hd + 1) * QK_NOPE_DIM, :]
            r = lax.rsqrt(jnp.sum(kx * kx, axis=0, keepdims=True) * (1.0 / QK_NOPE_DIM) + EPS)
            rows.append(big[nk + 8 * hd:nk + 8 * hd + 8, :] * r + s_rope[8 * hd:8 * hd + 8, :])
        return jnp.concatenate(rows, axis=0)

    sc_bufs = [sca_scr, scb_scr] * (n_chunks // 2 + 1)
    sc_bufs = sc_bufs[:n_chunks - 1] + [scl_scr]
    pad = PAGE_SIZE - cnew_ref.shape[0]

    def latent(c):
        return lat_buf[slot, pl.ds(c * tk, tk), :].astype(_BF16)

    def stage_scores(c):
        s_rope = _dot(q_rope, ropet_buf[slot, :, pl.ds(c * tk, tk)].astype(_BF16))
        if c < n_chunks - 1:
            sc_bufs[c][...] = scores(latent(c), s_rope)
        else:
            c_new = jnp.concatenate([cnew_ref[...], jnp.zeros((pad, KV_LORA_RANK), _F32)], axis=0)
            c_b = jnp.concatenate([latent(c), c_new.astype(_BF16)], axis=0)
            cbl_scr[...] = c_b
            kr_new = jnp.concatenate([krnew_ref[...], jnp.zeros((pad, QK_ROPE_DIM), _F32)], axis=0)
            s_rope = jnp.concatenate([s_rope, _dot_t(q_rope, kr_new.astype(_BF16))], axis=1)
            row = lax.broadcasted_iota(jnp.int32, (nrow, tk + PAGE_SIZE), 0)
            col = lax.broadcasted_iota(jnp.int32, (nrow, tk + PAGE_SIZE), 1)
            sc_bufs[c][...] = jnp.where(col - tk <= (row & 7), scores(c_b, s_rope), NEG_INF)

    def stage_update(c):
        _softmax_step(sc_bufs[c][...], latent(c), m_scr, l_scr, acc_scr)

    stage_scores(0)

    _softmax_step(scl_scr[...], cbl_scr[...], m_scr, l_scr, acc_scr)
    o_ref[...] = (acc_scr[...] / l_scr[...]).reshape(o_ref.shape)
    m_scr[...] = jnp.full_like(m_scr, NEG_INF)
    l_scr[...] = jnp.zeros_like(l_scr)
    acc_scr[...] = jnp.zeros_like(acc_scr)

    for c in range(1, n_chunks):
        stage_scores(c)
        stage_update(c - 1)


def _sample_attention(page_table, qcat, c_new, kr_new, wukt, cache_lat, cache_rope_t, t_new):
    n_seq, n_pages = page_table.shape
    assert t_new == 8 and n_pages % SATT_PAGES_PER_CHUNK == 0 and n_pages // SATT_PAGES_PER_CHUNK >= 2
    tk = SATT_PAGES_PER_CHUNK * PAGE_SIZE
    past = n_pages * PAGE_SIZE
    nrow = N_HEADS * t_new
    dq = KV_LORA_RANK + QK_ROPE_DIM
    cur = lambda s: jnp.minimum(s, n_seq - 1)
    prev = lambda s: jnp.maximum(s - 1, 0)
    grid_spec = pltpu.PrefetchScalarGridSpec(
        num_scalar_prefetch=1,
        grid=(n_seq + 1,),
        in_specs=[
            pl.BlockSpec((N_HEADS, t_new, dq), lambda s, pt: (0, cur(s), 0)),
            pl.BlockSpec((t_new, KV_LORA_RANK), lambda s, pt: (cur(s), 0)),
            pl.BlockSpec((t_new, QK_ROPE_DIM), lambda s, pt: (cur(s), 0)),
            pl.BlockSpec((N_HEADS * QK_NOPE_DIM, KV_LORA_RANK), lambda s, pt: (0, 0)),
            pl.BlockSpec(memory_space=pl.ANY),
            pl.BlockSpec(memory_space=pl.ANY),
        ],
        out_specs=pl.BlockSpec((N_HEADS, t_new, KV_LORA_RANK), lambda s, pt: (0, prev(s), 0)),
        scratch_shapes=[
            pltpu.VMEM((N_HEADS * QK_NOPE_DIM + nrow, KV_LORA_RANK), _BF16),
            pltpu.VMEM((2, past, KV_LORA_RANK), _F32),
            pltpu.VMEM((2, QK_ROPE_DIM, past), _F32),
            pltpu.VMEM((nrow, tk), _F32),
            pltpu.VMEM((nrow, tk), _F32),
            pltpu.VMEM((nrow, tk + PAGE_SIZE), _F32),
            pltpu.VMEM((tk + PAGE_SIZE, KV_LORA_RANK), _BF16),
            pltpu.SemaphoreType.DMA((2,)),
            pltpu.SemaphoreType.DMA((2,)),
            pltpu.VMEM((nrow, 1), _F32),
            pltpu.VMEM((nrow, 1), _F32),
            pltpu.VMEM((nrow, KV_LORA_RANK), _F32),
        ],
    )
    return pl.pallas_call(
        functools.partial(_sattn_kernel, n_pages),
        grid_spec=grid_spec,
        out_shape=jax.ShapeDtypeStruct((N_HEADS, n_seq * t_new, KV_LORA_RANK), _F32),
        compiler_params=pltpu.CompilerParams(
            dimension_semantics=("arbitrary",), vmem_limit_bytes=VMEM_LIMIT),
        name="sample_attention",
    )(page_table.reshape(-1), qcat, c_new, kr_new, wukt, cache_lat, cache_rope_t)


def _merge_kernel(from_latent, a_ref, gate_ref, gout_ref, x_ref, wuv_ref, woa_ref, wout_ref, o_ref):
    if from_latent:
        heads = [_dot(a_ref[hd].astype(_BF16), wuv_ref[:, hd * V_HEAD_DIM:(hd + 1) * V_HEAD_DIM])
                 for hd in range(N_HEADS)]
        attn = jnp.concatenate(heads, axis=-1).astype(_BF16)
    else:
        attn = a_ref[...]
    merged = gate_ref[...].astype(_F32) * _dot(attn, woa_ref[...]) + gout_ref[...]
    o_ref[...] = x_ref[...] + _dot(merged.astype(_BF16), wout_ref[...])


def _merge(from_latent, attn, gate, gout, x, wuv, woa, wout):
    n = x.shape[0]
    tm = MERGE_TM
    tok = pl.BlockSpec((tm, D_MODEL), lambda i: (i, 0))
    if from_latent:
        a_spec = pl.BlockSpec((N_HEADS, tm, KV_LORA_RANK), lambda i: (0, i, 0))
    else:
        a_spec = tok
    return pl.pallas_call(
        functools.partial(_merge_kernel, from_latent),
        grid=(n // tm,),
        in_specs=[a_spec, tok, tok, tok, _const_spec(wuv.shape), _const_spec(woa.shape),
                  _const_spec(wout.shape)],
        out_specs=tok,
        out_shape=jax.ShapeDtypeStruct((n, D_MODEL), _F32),
        compiler_params=pltpu.CompilerParams(
            dimension_semantics=("parallel",), vmem_limit_bytes=VMEM_LIMIT),
        name="merge_sample" if from_latent else "merge_prompt",
    )(attn, gate, gout, x, wuv, woa, wout)


def _swap_halves(a, axis):
    lo, hi = jnp.split(a, 2, axis=axis)
    return jnp.concatenate([hi, lo], axis=axis)


def _slot(a):
    return jnp.concatenate([a, jnp.zeros_like(a)], axis=-1)


def _rope_tables(pos):
    half = QK_ROPE_DIM // 2
    inv = ROPE_BASE ** (-np.arange(half, dtype=np.float64) / half)
    ang = np.asarray(pos, np.float64)[:, None] * inv[None, :]
    cos, sin = np.cos(ang), np.sin(ang)
    zero = np.zeros((ang.shape[0], QK_ROPE_DIM))
    return (jnp.asarray(np.concatenate([cos, cos, zero], axis=-1), _F32),
            jnp.asarray(np.concatenate([-sin, sin, zero], axis=-1), _F32))


def _prep_weights(w_in, w_uq, w_uk, w_uv, w_o_gmlp, mix_norm, q_lora_norm, kv_lora_norm, q_nope_norm,
                  q_rope_norm, k_nope_norm, k_rope_norm, gmlp_v_norm):
    w_uq = w_uq.astype(_BF16)
    off_kr = Q_LORA_RANK + KV_LORA_RANK
    w_in_t = w_in.T.astype(_BF16)
    k_r_t = w_in_t[off_kr:off_kr + QK_ROPE_DIM]
    zero = jnp.zeros_like(k_r_t)
    w_in_kr_t = jnp.concatenate([k_r_t, zero, _swap_halves(k_r_t, 0), zero], axis=0)
    assert w_in_kr_t.shape[0] == 2 * ROPE_SLOT and _D_IN_PADDED == _OFF_KR + 2 * ROPE_SLOT
    wq = w_uq.reshape(Q_LORA_RANK, N_HEADS, QK_HEAD_DIM)
    wq_nope = wq[:, :, :QK_NOPE_DIM].reshape(Q_LORA_RANK, -1)
    wq_rope = wq[:, :, QK_NOPE_DIM:]
    w_uq_p = jnp.concatenate([wq_nope, _slot(wq_rope).reshape(Q_LORA_RANK, -1),
                              _slot(_swap_halves(wq_rope, -1)).reshape(Q_LORA_RANK, -1)], axis=-1)
    row = lambda a: a.reshape(1, -1).astype(_F32)
    return dict(
        mix_norm=row(mix_norm), w_in_t=w_in_t, w_in_kr_t=w_in_kr_t, q_lora_norm=row(q_lora_norm),
        w_uq=w_uq_p.astype(_BF16), kv_lora_norm=row(kv_lora_norm), w_uk=w_uk.astype(_BF16),
        w_uv=w_uv.astype(_BF16), w_uv_t=w_uv.T.astype(_BF16), q_nope_norm=row(q_nope_norm), q_rope_norm=row(_slot(q_rope_norm)),
        q_rope_norm_s=row(_slot(_swap_halves(q_rope_norm, -1))), k_nope_norm=row(k_nope_norm),
        k_rope_norm=row(_slot(k_rope_norm)), k_rope_norm_s=row(_slot(_swap_halves(k_rope_norm, -1))),
        gmlp_v_norm=row(gmlp_v_norm), w_o_gmlp=w_o_gmlp.astype(_BF16))


def kernel(x_prompt, x_sample, cache_kv_latent, cache_k_rope, page_table, ffn1_norm, ffn1_w_gate, ffn1_w_up, ffn1_w_down, mix_norm, w_in, q_lora_norm, w_uq, kv_lora_norm, w_uk, w_uv, q_nope_norm, q_rope_norm, k_nope_norm, k_rope_norm, gmlp_v_norm, gmlp_w_s, gmlp_b_s, w_o_attn, w_o_gmlp, w_out, ffn2_norm, ffn2_w_gate, ffn2_w_up, ffn2_w_down):
    batch, seq, _ = x_prompt.shape
    n_seq, t_new, _ = x_sample.shape
    assert ffn1_norm.shape[0] == 1
    past = page_table.shape[1] * PAGE_SIZE
    n_pool = cache_kv_latent.shape[1]
    l = 0
    row = lambda a: a.reshape(1, -1).astype(_F32)
    w = _prep_weights(w_in[l], w_uq[l], w_uk[l], w_uv[l], w_o_gmlp[l], mix_norm[l], q_lora_norm[l],
                      kv_lora_norm[l], q_nope_norm[l], q_rope_norm[l], k_nope_norm[l], k_rope_norm[l],
                      gmlp_v_norm[l])
    ffn1 = (row(ffn1_norm[l]), ffn1_w_gate[l].astype(_BF16), ffn1_w_up[l].astype(_BF16),
            ffn1_w_down[l].astype(_BF16))
    ffn2 = (row(ffn2_norm[l]), ffn2_w_gate[l].astype(_BF16), ffn2_w_up[l].astype(_BF16),
            ffn2_w_down[l].astype(_BF16))
    woa = w_o_attn[l].astype(_BF16)
    wout = w_out[l].astype(_BF16)

    wp = dict(w, gmlp_w_s=gmlp_w_s[l].astype(_F32), gmlp_b_s=gmlp_b_s[l].reshape(GMLP_GROUPS, CHUNK, 1))
    cos_p, sin_p = _rope_tables(np.arange(seq))
    xp = _ffn(x_prompt.reshape(batch * seq, D_MODEL), *ffn1)
    q, k, v, ckv_p, kr_p, gate_p, gout_p = _proj(False, xp, wp, cos_p, sin_p, seq // PROJ_TM)
    attn_p = _prompt_attention(q, k, v, batch, seq)
    xp = _merge(False, attn_p, gate_p, gout_p, xp, w["w_uv"], woa, wout)
    xp = _ffn(xp, *ffn2)

    reps = CHUNK // t_new
    ws_s = jax.vmap(lambda m: jnp.kron(jnp.eye(reps, dtype=_F32), m))(gmlp_w_s[l][:, :t_new, :t_new])
    bs_s = jnp.tile(gmlp_b_s[l][:, :t_new], (1, reps)).reshape(GMLP_GROUPS, CHUNK, 1)
    ws = dict(w, gmlp_w_s=ws_s, gmlp_b_s=bs_s)
    cos_s, sin_s = _rope_tables(past + np.arange(PROJ_TM) % t_new)
    xs = _ffn(x_sample.reshape(n_seq * t_new, D_MODEL), *ffn1)
    qcat, ckv_s, kr_s, vn_s, gate_s, gout_s = _proj(True, xs, ws, cos_s, sin_s, 1)
    o_lat = _sample_attention(page_table, qcat, ckv_s, kr_s, w["w_uk"].T,
                              cache_kv_latent.reshape(n_pool, PAGE_SIZE, KV_LORA_RANK),
                              jnp.swapaxes(cache_k_rope.reshape(n_pool, PAGE_SIZE, QK_ROPE_DIM), 1, 2),
                              t_new)
    xs = _merge(True, o_lat, gate_s, gout_s, xs, w["w_uv"], woa, wout)
    xs = _ffn(xs, *ffn2)

    return (xp.reshape(batch, seq, D_MODEL), xs.reshape(n_seq, t_new, D_MODEL),
            ckv_p.reshape(1, batch, seq, KV_LORA_RANK), jnp.swapaxes(kr_p, 1, 2)[None],
            ckv_s.reshape(1, n_seq, t_new, KV_LORA_RANK), kr_s.reshape(1, n_seq, t_new, QK_ROPE_DIM),
            vn_s.reshape(1, n_seq, t_new, GMLP_WIDTH))
```

```python
import functools

import jax
import jax.numpy as jnp
import numpy as np
from jax import lax
from jax.experimental import pallas as pl
from jax.experimental.pallas import tpu as pltpu

D_MODEL = 1024
N_HEADS = 8
QK_NOPE_DIM = 128
QK_ROPE_DIM = 64
QK_HEAD_DIM = QK_NOPE_DIM + QK_ROPE_DIM
V_HEAD_DIM = 128
Q_LORA_RANK = 256
KV_LORA_RANK = 256
ROPE_BASE = 10000.0
ATTN_SCALE = QK_HEAD_DIM ** -0.5
LOG2E = 1.4426950408889634
GMLP_GROUPS = 4
GMLP_WIDTH = 1024
GMLP_GROUP_DIM = GMLP_WIDTH // GMLP_GROUPS
CHUNK = 128
PAGE_SIZE = 128
D_FF = 2816
EPS = 1e-6
NEG_INF = -1e30

LANE = 128
ROPE_SLOT = LANE

_OFF_CQ = 0
_OFF_CKV = _OFF_CQ + Q_LORA_RANK
_OFF_U = _OFF_CKV + KV_LORA_RANK
_OFF_V = _OFF_U + GMLP_WIDTH
_OFF_GA = _OFF_V + GMLP_WIDTH
_OFF_GG = _OFF_GA + D_MODEL
_OFF_KR = _OFF_GG + D_MODEL
_OFF_KRS = _OFF_KR + ROPE_SLOT
_D_IN_PADDED = _OFF_KRS + ROPE_SLOT

FFN_TM = 1024
FFN_TF = 256
PROJ_TM = 512
ATT_TQ = 512
ATT_TK = 512
ATT_HB = 4
ATT_ONES_ROWS = 16
MERGE_TM = 1024
SATT_PAGES_PER_CHUNK = 16
VMEM_LIMIT = 56 * 1024 * 1024

_BF16 = jnp.bfloat16
_F32 = jnp.float32


def _dot(a, b):
    return jnp.dot(a, b, preferred_element_type=_F32)


def _dot_t(a, b):
    return lax.dot_general(a, b, (((1,), (1,)), ((), ())), preferred_element_type=_F32)


def _rms(x, n):
    return lax.rsqrt(jnp.sum(x * x, axis=-1, keepdims=True) * (1.0 / n) + EPS)


def _ffn_kernel(x_ref, g_ref, wg_hbm, wu_hbm, wd_hbm, o_ref, wg_scr, wu_scr, wd_scr, stage_g, stage_u,
                stage_d, sem, act_scr):
    n_blk = D_FF // FFN_TF

    def block_copies(j, slot):
        cols = pl.ds(j * FFN_TF, FFN_TF)
        return [pltpu.make_async_copy(wg_hbm.at[:, cols], stage_g.at[slot], sem.at[slot]),
                pltpu.make_async_copy(wu_hbm.at[:, cols], stage_u.at[slot], sem.at[slot]),
                pltpu.make_async_copy(wd_hbm.at[cols, :], stage_d.at[slot], sem.at[slot])]

    def fetch_block(j):
        slot = j % 2
        if j + 1 < n_blk:
            for d in block_copies(j + 1, 1 - slot):
                d.start()
        for d in block_copies(j, slot):
            d.wait()
        sl = slice(j * FFN_TF, (j + 1) * FFN_TF)
        wg_scr[:, sl] = stage_g[slot].astype(_BF16)
        wu_scr[:, sl] = stage_u[slot].astype(_BF16)
        wd_scr[sl, :] = stage_d[slot].astype(_BF16)

    def tile(first_step):
        x = x_ref[...]
        h = (x * _rms(x, D_MODEL) * g_ref[...]).astype(_BF16)
        if first_step:
            for d in block_copies(0, 0):
                d.start()
        for j in range(n_blk):
            if first_step:
                fetch_block(j)
            sl = slice(j * FFN_TF, (j + 1) * FFN_TF)
            gate = _dot(h, wg_scr[:, sl])
            up = _dot(h, wu_scr[:, sl])
            act_scr[:, sl] = (gate * jax.nn.sigmoid(gate) * up).astype(_BF16)
        o_ref[...] = x + 0.5 * _dot(act_scr[...], wd_scr[...])

    pl.when(pl.program_id(0) == 0)(functools.partial(tile, True))
    pl.when(pl.program_id(0) > 0)(functools.partial(tile, False))


def _ffn(x, norm_g, wg, wu, wd):
    n = x.shape[0]
    tm = min(FFN_TM, n)
    any_spec = pl.BlockSpec(memory_space=pl.ANY)
    return pl.pallas_call(
        _ffn_kernel,
        grid=(n // tm,),
        in_specs=[pl.BlockSpec((tm, D_MODEL), lambda i: (i, 0)), _const_spec(norm_g.shape),
                  any_spec, any_spec, any_spec],
        out_specs=pl.BlockSpec((tm, D_MODEL), lambda i: (i, 0)),
        out_shape=jax.ShapeDtypeStruct((n, D_MODEL), _F32),
        scratch_shapes=[pltpu.VMEM((D_MODEL, D_FF), _BF16), pltpu.VMEM((D_MODEL, D_FF), _BF16),
                        pltpu.VMEM((D_FF, D_MODEL), _BF16),
                        pltpu.VMEM((2, D_MODEL, FFN_TF), _F32), pltpu.VMEM((2, D_MODEL, FFN_TF), _F32),
                        pltpu.VMEM((2, FFN_TF, D_MODEL), _F32),
                        pltpu.SemaphoreType.DMA((2,)),
                        pltpu.VMEM((tm, D_FF), _BF16)],
        compiler_params=pltpu.CompilerParams(
            dimension_semantics=("arbitrary",), vmem_limit_bytes=VMEM_LIMIT),
        name="ffn",
    )(x, norm_g, wg, wu, wd)


def _rope_slot(x, xs, g, gs, cos, sin):
    r = _rms(x, QK_ROPE_DIM)
    return (x * r * g) * cos + (xs * r * gs) * sin


def _gelu(x):
    return 0.5 * x * (1.0 + lax.erf(x * (2.0 ** -0.5)))


def _proj_kernel(sample, x_ref, mixg_ref, wt_ref, wkr_ref, qlg_ref, wuq_ref, kvg_ref, wuk_ref, wuv_ref,
                 qng_ref, qrg_ref, qrgs_ref, kng_ref, krg_ref, krgs_ref, vg_ref, ws_ref, bs_ref,
                 wog_ref, cos_ref, sin_ref, *rest):
    if sample:
        qcat_ref, ckv_ref, kr_ref, vn_ref, gate_ref, gout_ref, gm_scr = rest
    else:
        q_ref, k_ref, v_ref, ckv_ref, kr_ref, gate_ref, gout_ref, gm_scr = rest
    tm = x_ref.shape[0]
    x = x_ref[...]
    h = (x * _rms(x, D_MODEL) * mixg_ref[...]).astype(_BF16)
    cos = cos_ref[...]
    sin = sin_ref[...]

    def win(off, width):
        if off >= _OFF_KR:
            assert (off, width) == (_OFF_KR, 2 * ROPE_SLOT)
            return _dot_t(h, wkr_ref[...])
        row = off if off < _OFF_U else off + QK_ROPE_DIM
        return _dot_t(h, wt_ref[row:row + width, :])

    kr_raw = win(_OFF_KR, 2 * ROPE_SLOT)
    ckv = win(_OFF_CKV, KV_LORA_RANK)
    cq = win(_OFF_CQ, Q_LORA_RANK)
    u = win(_OFF_U, GMLP_WIDTH)
    v = win(_OFF_V, GMLP_WIDTH)

    kr = _rope_slot(kr_raw[:, :ROPE_SLOT], kr_raw[:, ROPE_SLOT:], krg_ref[...], krgs_ref[...],
                    cos, sin)[:, :QK_ROPE_DIM]
    if sample:
        kr_ref[...] = kr
    else:
        kr_ref[0] = kr.T
    ckv = ckv * _rms(ckv, KV_LORA_RANK) * kvg_ref[...]
    ckv_ref[...] = ckv
    ckv_b = ckv.astype(_BF16)
    cq_b = (cq * _rms(cq, Q_LORA_RANK) * qlg_ref[...]).astype(_BF16)

    nq = N_HEADS * QK_NOPE_DIM
    nr = N_HEADS * ROPE_SLOT
    if not sample:
        kexp = _dot(ckv_b, wuk_ref[...])
        vals_t = _dot_t(wuv_ref[...], ckv_b)
    q_nope = _dot(cq_b, wuq_ref[:, 0:nq])
    q_rope = _dot(cq_b, wuq_ref[:, nq:nq + nr])
    q_rope_s = _dot(cq_b, wuq_ref[:, nq + nr:nq + 2 * nr])
    gate_g = jax.nn.sigmoid(win(_OFF_GG, D_MODEL))
    gate_ref[...] = jax.nn.sigmoid(win(_OFF_GA, D_MODEL)).astype(gate_ref.dtype)

    if not sample:
        kr_b = kr.astype(_BF16)
        for hd in range(N_HEADS):
            sl = slice(hd * QK_NOPE_DIM, (hd + 1) * QK_NOPE_DIM)
            kh = kexp[:, sl]
            k_ref[hd, :, 0:QK_NOPE_DIM] = (kh * _rms(kh, QK_NOPE_DIM) * kng_ref[...]).astype(_BF16)
            k_ref[hd, :, QK_NOPE_DIM:QK_HEAD_DIM] = kr_b
            v_ref[hd] = vals_t[hd * V_HEAD_DIM:(hd + 1) * V_HEAD_DIM, :].astype(_BF16)

    q_scale = ATTN_SCALE if sample else ATTN_SCALE * LOG2E
    for hd in range(N_HEADS):
        sl = slice(hd * QK_NOPE_DIM, (hd + 1) * QK_NOPE_DIM)
        qh = q_nope[:, sl]
        qh = qh * _rms(qh, QK_NOPE_DIM) * qng_ref[...] * q_scale
        rs = slice(hd * ROPE_SLOT, (hd + 1) * ROPE_SLOT)
        qr = _rope_slot(q_rope[:, rs], q_rope_s[:, rs], qrg_ref[...], qrgs_ref[...], cos, sin)
        qr = qr[:, :QK_ROPE_DIM] * q_scale
        if sample:
            qa = _dot_t((qh * kng_ref[...]).astype(_BF16), wuk_ref[:, sl])
            qcat_ref[hd, :, 0:KV_LORA_RANK] = qa
            qcat_ref[hd, :, KV_LORA_RANK:KV_LORA_RANK + QK_ROPE_DIM] = qr
        else:
            q_ref[hd, :, 0:QK_NOPE_DIM] = qh.astype(_BF16)
            q_ref[hd, :, QK_NOPE_DIM:QK_HEAD_DIM] = qr.astype(_BF16)

    u = _gelu(u)
    v = _gelu(v)
    row = lax.broadcasted_iota(jnp.int32, (CHUNK, CHUNK), 0)
    col = lax.broadcasted_iota(jnp.int32, (CHUNK, CHUNK), 1)
    for g in range(GMLP_GROUPS):
        gs = slice(g * GMLP_GROUP_DIM, (g + 1) * GMLP_GROUP_DIM)
        vg = v[:, gs]
        vg = vg * _rms(vg, GMLP_GROUP_DIM) * vg_ref[:, gs]
        if sample:
            vn_ref[:, gs] = vg
        vg_b = vg.astype(_BF16)
        w = jnp.where(col <= row, ws_ref[g], 0.0).astype(_BF16)
        for c in range(tm // CHUNK):
            cs = slice(c * CHUNK, (c + 1) * CHUNK)
            mix = _dot(w, vg_b[cs]) + bs_ref[g]
            gm_scr[cs, gs] = (u[cs, gs] * mix).astype(_BF16)
    gout_ref[...] = (gate_g * _dot(gm_scr[...], wog_ref[...])).astype(gout_ref.dtype)


def _const_spec(shape):
    nd = len(shape)
    return pl.BlockSpec(shape, lambda i: (0,) * nd, pipeline_mode=pl.Buffered(1))


def _proj(sample, x, w, cos, sin, n_pos_tiles):
    n = x.shape[0]
    tm = PROJ_TM
    tok = lambda width: pl.BlockSpec((tm, width), lambda i: (i, 0))
    heads = lambda width: pl.BlockSpec((N_HEADS, tm, width), lambda i: (0, i, 0))
    weights = [w["mix_norm"], w["w_in_t"], w["w_in_kr_t"], w["q_lora_norm"], w["w_uq"],
               w["kv_lora_norm"], w["w_uk"],
               w["w_uv_t"], w["q_nope_norm"], w["q_rope_norm"], w["q_rope_norm_s"], w["k_nope_norm"],
               w["k_rope_norm"], w["k_rope_norm_s"], w["gmlp_v_norm"], w["gmlp_w_s"], w["gmlp_b_s"],
               w["w_o_gmlp"]]
    pos_spec = pl.BlockSpec((tm, ROPE_SLOT), lambda i: (i % n_pos_tiles, 0))
    in_specs = [tok(D_MODEL)] + [_const_spec(a.shape) for a in weights] + [pos_spec, pos_spec]
    sds = jax.ShapeDtypeStruct
    if sample:
        out_shape = [sds((N_HEADS, n, KV_LORA_RANK + QK_ROPE_DIM), _F32), sds((n, KV_LORA_RANK), _F32),
                     sds((n, QK_ROPE_DIM), _F32), sds((n, GMLP_WIDTH), _F32),
                     sds((n, D_MODEL), _BF16), sds((n, D_MODEL), _BF16)]
        out_specs = [heads(KV_LORA_RANK + QK_ROPE_DIM), tok(KV_LORA_RANK), tok(QK_ROPE_DIM),
                     tok(GMLP_WIDTH), tok(D_MODEL), tok(D_MODEL)]
    else:
        out_shape = [sds((N_HEADS, n, QK_HEAD_DIM), _BF16), sds((N_HEADS, n, QK_HEAD_DIM), _BF16),
                     sds((N_HEADS, V_HEAD_DIM, n), _BF16), sds((n, KV_LORA_RANK), _F32),
                     sds((n // (n_pos_tiles * tm), QK_ROPE_DIM, n_pos_tiles * tm), _F32),
                     sds((n, D_MODEL), _BF16), sds((n, D_MODEL), _BF16)]
        vt_spec = pl.BlockSpec((N_HEADS, V_HEAD_DIM, tm), lambda i: (0, 0, i))
        krt_spec = pl.BlockSpec((1, QK_ROPE_DIM, tm), lambda i: (i // n_pos_tiles, 0, i % n_pos_tiles))
        out_specs = [heads(QK_HEAD_DIM), heads(QK_HEAD_DIM), vt_spec, tok(KV_LORA_RANK),
                     krt_spec, tok(D_MODEL), tok(D_MODEL)]
    return pl.pallas_call(
        functools.partial(_proj_kernel, sample),
        grid=(n // tm,),
        in_specs=in_specs,
        out_specs=out_specs,
        out_shape=out_shape,
        scratch_shapes=[pltpu.VMEM((tm, GMLP_WIDTH), _BF16)],
        compiler_params=pltpu.CompilerParams(
            dimension_semantics=("parallel",), vmem_limit_bytes=VMEM_LIMIT),
        name="proj_sample" if sample else "proj_prompt",
    )(x, *weights, cos, sin)


def _softmax_step(s, v_b, m_scr, l_scr, acc_scr):
    m_prev = m_scr[...]
    m_new = jnp.maximum(m_prev, jnp.max(s, axis=-1, keepdims=True))
    alpha = jnp.exp(m_prev - m_new)
    p = jnp.exp(s - m_new)
    l_scr[...] = alpha * l_scr[...] + jnp.sum(p, axis=-1, keepdims=True)
    acc_scr[...] = alpha * acc_scr[...] + _dot(p.astype(_BF16), v_b)
    m_scr[...] = m_new


def _pattn_kernel(q_ref, k_ref, vt_ref, o_ref, sa_scr, sb_scr, m_scr, acc_scr):
    qi = pl.program_id(2)
    m_scr[...] = jnp.full_like(m_scr, NEG_INF)
    acc_scr[...] = jnp.zeros_like(acc_scr)

    def scores(kb, buf, masked):
        off = pl.multiple_of(kb * ATT_TK, ATT_TK)
        for hh in range(ATT_HB):
            s = _dot_t(k_ref[hh, pl.ds(off, ATT_TK), :], q_ref[hh])
            if masked:
                key = lax.broadcasted_iota(jnp.int32, s.shape, 0)
                qry = lax.broadcasted_iota(jnp.int32, s.shape, 1)
                s = jnp.where(key <= qry, s, NEG_INF)
            buf[hh] = s

    def update(kb, buf):
        off = pl.multiple_of(kb * ATT_TK, ATT_TK)
        for hh in range(ATT_HB):
            s = buf[hh]
            m_prev = m_scr[hh]
            m_new = jnp.maximum(m_prev, jnp.max(s, axis=0, keepdims=True))
            p = jnp.exp2(s - m_new).astype(_BF16)
            vt = vt_ref[hh, :, pl.ds(off, ATT_TK)]
            vt1 = jnp.concatenate([vt, jnp.ones((ATT_ONES_ROWS, ATT_TK), _BF16)], axis=0)
            acc_scr[hh] = jnp.exp2(m_prev - m_new) * acc_scr[hh] + _dot(vt1, p)
            m_scr[hh] = m_new

    def trip(t, ybuf, xbuf, masked):
        scores(t, xbuf, masked)
        update(t - 1, ybuf)

    @pl.when(qi == 0)
    def _():
        scores(0, sa_scr, True)
        update(0, sa_scr)

    @pl.when(qi > 0)
    def _():
        scores(0, sa_scr, False)

        def pair(j, carry):
            trip(2 * j + 1, sa_scr, sb_scr, False)
            trip(2 * j + 2, sb_scr, sa_scr, False)
            return carry

        lax.fori_loop(0, lax.shift_right_logical(qi - 1, 1), pair, 0)

        @pl.when((qi & 1) == 1)
        def _():
            trip(qi, sa_scr, sb_scr, True)
            update(qi, sb_scr)

        @pl.when((qi & 1) == 0)
        def _():
            trip(qi - 1, sa_scr, sb_scr, False)
            trip(qi, sb_scr, sa_scr, True)
            update(qi, sa_scr)

    for hh in range(ATT_HB):
        acc = acc_scr[hh]
        o_t = acc[:V_HEAD_DIM] / acc[V_HEAD_DIM:V_HEAD_DIM + 1]
        o_ref[:, hh * V_HEAD_DIM:(hh + 1) * V_HEAD_DIM] = o_t.T.astype(o_ref.dtype)


def _prompt_attention(q, k, vt, batch, seq):
    assert ATT_TQ == ATT_TK
    nq = seq // ATT_TQ
    hb = ATT_HB
    return pl.pallas_call(
        _pattn_kernel,
        grid=(batch, N_HEADS // hb, nq),
        in_specs=[
            pl.BlockSpec((hb, ATT_TQ, QK_HEAD_DIM), lambda b, h, i: (h, b * nq + i, 0)),
            pl.BlockSpec((hb, seq, QK_HEAD_DIM), lambda b, h, i: (h, b, 0)),
            pl.BlockSpec((hb, V_HEAD_DIM, seq), lambda b, h, i: (h, 0, b)),
        ],
        out_specs=pl.BlockSpec((ATT_TQ, hb * V_HEAD_DIM), lambda b, h, i: (b * nq + i, h)),
        out_shape=jax.ShapeDtypeStruct((batch * seq, N_HEADS * V_HEAD_DIM), _BF16),
        scratch_shapes=[pltpu.VMEM((hb, ATT_TK, ATT_TQ), _F32),
                        pltpu.VMEM((hb, ATT_TK, ATT_TQ), _F32),
                        pltpu.VMEM((hb, 1, ATT_TQ), _F32),
                        pltpu.VMEM((hb, V_HEAD_DIM + ATT_ONES_ROWS, ATT_TQ), _F32)],
        compiler_params=pltpu.CompilerParams(
            dimension_semantics=("parallel", "parallel", "arbitrary"), vmem_limit_bytes=VMEM_LIMIT),
        name="prompt_attention",
    )(q, k, vt)


def _sattn_kernel(n_pages, pt_ref, qcat_ref, cnew_ref, krnew_ref, wukt_ref, lat_hbm, ropet_hbm, o_ref,
                  a_scr, lat_buf, ropet_buf, sca_scr, scb_scr, scl_scr, cbl_scr, sem_lat, sem_rope,
                  m_scr, l_scr, acc_scr):
    pages = SATT_PAGES_PER_CHUNK
    n_chunks = n_pages // pages
    tk = pages * PAGE_SIZE
    nk = N_HEADS * QK_NOPE_DIM
    nrow = N_HEADS * 8
    s = pl.program_id(0)
    n_seq = pl.num_programs(0) - 1
    slot = lax.rem(s, 2)

    def copies(seq, dst_slot):
        out = []
        for g in range(n_pages):
            page = 0 if seq is None else pt_ref[seq * n_pages + g]
            dst = pl.ds(g * PAGE_SIZE, PAGE_SIZE)
            out.append(pltpu.make_async_copy(lat_hbm.at[page], lat_buf.at[dst_slot, dst], sem_lat.at[dst_slot]))
            out.append(pltpu.make_async_copy(ropet_hbm.at[page], ropet_buf.at[dst_slot, :, dst],
                                             sem_rope.at[dst_slot]))
        return out

    @pl.when(s == 0)
    def _():
        a_scr[0:nk, :] = wukt_ref[...]
        scl_scr[...] = jnp.zeros_like(scl_scr)
        cbl_scr[...] = jnp.zeros_like(cbl_scr)
        m_scr[...] = jnp.zeros_like(m_scr)
        l_scr[...] = jnp.ones_like(l_scr)
        acc_scr[...] = jnp.zeros_like(acc_scr)
        for d in copies(0, 0):
            d.start()

    @pl.when(s < n_seq)
    def _():
        for d in copies(jnp.minimum(s + 1, n_seq - 1), 1 - slot):
            d.start()

    for d in copies(None, slot):
        d.wait()

    q2 = qcat_ref[...].reshape(nrow, KV_LORA_RANK + QK_ROPE_DIM)
    a_scr[nk:nk + nrow, :] = q2[:, :KV_LORA_RANK].astype(_BF16)
    q_rope = q2[:, KV_LORA_RANK:].astype(_BF16)

    def scores(c_b, s_rope):
        big = _dot_t(a_scr[...], c_b)
        rows = []
        for hd in range(N_HEADS):
            kx = big[hd * QK_NOPE_DIM:(hd + 1) * QK_NOPE_DIM, :]
            r = lax.rsqrt(jnp.sum(kx * kx, axis=0, keepdims=True) * (1.0 / QK_NOPE_DIM) + EPS)
            rows.append(big[nk + 8 * hd:nk + 8 * hd + 8, :] * r + s_rope[8 * hd:8 * hd + 8, :])
        return jnp.concatenate(rows, axis=0)

    sc_bufs = [sca_scr, scb_scr] * (n_chunks // 2 + 1)
    sc_bufs = sc_bufs[:n_chunks - 1] + [scl_scr]
    pad = PAGE_SIZE - cnew_ref.shape[0]

    def latent(c):
        return lat_buf[slot, pl.ds(c * tk, tk), :].astype(_BF16)

    def stage_scores(c):
        s_rope = _dot(q_rope, ropet_buf[slot, :, pl.ds(c * tk, tk)].astype(_BF16))
        if c < n_chunks - 1:
            sc_bufs[c][...] = scores(latent(c), s_rope)
        else:
            c_new = jnp.concatenate([cnew_ref[...], jnp.zeros((pad, KV_LORA_RANK), _F32)], axis=0)
            c_b = jnp.concatenate([latent(c), c_new.astype(_BF16)], axis=0)
            cbl_scr[...] = c_b
            kr_new = jnp.concatenate([krnew_ref[...], jnp.zeros((pad, QK_ROPE_DIM), _F32)], axis=0)
            s_rope = jnp.concatenate([s_rope, _dot_t(q_rope, kr_new.astype(_BF16))], axis=1)
            row = lax.broadcasted_iota(jnp.int32, (nrow, tk + PAGE_SIZE), 0)
            col = lax.broadcasted_iota(jnp.int32, (nrow, tk + PAGE_SIZE), 1)
            sc_bufs[c][...] = jnp.where(col - tk <= (row & 7), scores(c_b, s_rope), NEG_INF)

    def stage_update(c):
        _softmax_step(sc_bufs[c][...], latent(c), m_scr, l_scr, acc_scr)

    stage_scores(0)

    _softmax_step(scl_scr[...], cbl_scr[...], m_scr, l_scr, acc_scr)
    o_ref[...] = (acc_scr[...] / l_scr[...]).reshape(o_ref.shape)
    m_scr[...] = jnp.full_like(m_scr, NEG_INF)
    l_scr[...] = jnp.zeros_like(l_scr)
    acc_scr[...] = jnp.zeros_like(acc_scr)

    for c in range(1, n_chunks):
        stage_scores(c)
        stage_update(c - 1)


def _sample_attention(page_table, qcat, c_new, kr_new, wukt, cache_lat, cache_rope_t, t_new):
    n_seq, n_pages = page_table.shape
    assert t_new == 8 and n_pages % SATT_PAGES_PER_CHUNK == 0 and n_pages // SATT_PAGES_PER_CHUNK >= 2
    tk = SATT_PAGES_PER_CHUNK * PAGE_SIZE
    past = n_pages * PAGE_SIZE
    nrow = N_HEADS * t_new
    dq = KV_LORA_RANK + QK_ROPE_DIM
    cur = lambda s: jnp.minimum(s, n_seq - 1)
    prev = lambda s: jnp.maximum(s - 1, 0)
    grid_spec = pltpu.PrefetchScalarGridSpec(
        num_scalar_prefetch=1,
        grid=(n_seq + 1,),
        in_specs=[
            pl.BlockSpec((N_HEADS, t_new, dq), lambda s, pt: (0, cur(s), 0)),
            pl.BlockSpec((t_new, KV_LORA_RANK), lambda s, pt: (cur(s), 0)),
            pl.BlockSpec((t_new, QK_ROPE_DIM), lambda s, pt: (cur(s), 0)),
            pl.BlockSpec((N_HEADS * QK_NOPE_DIM, KV_LORA_RANK), lambda s, pt: (0, 0)),
            pl.BlockSpec(memory_space=pl.ANY),
            pl.BlockSpec(memory_space=pl.ANY),
        ],
        out_specs=pl.BlockSpec((N_HEADS, t_new, KV_LORA_RANK), lambda s, pt: (0, prev(s), 0)),
        scratch_shapes=[
            pltpu.VMEM((N_HEADS * QK_NOPE_DIM + nrow, KV_LORA_RANK), _BF16),
            pltpu.VMEM((2, past, KV_LORA_RANK), _F32),
            pltpu.VMEM((2, QK_ROPE_DIM, past), _F32),
            pltpu.VMEM((nrow, tk), _F32),
            pltpu.VMEM((nrow, tk), _F32),
            pltpu.VMEM((nrow, tk + PAGE_SIZE), _F32),
            pltpu.VMEM((tk + PAGE_SIZE, KV_LORA_RANK), _BF16),
            pltpu.SemaphoreType.DMA((2,)),
            pltpu.SemaphoreType.DMA((2,)),
            pltpu.VMEM((nrow, 1), _F32),
            pltpu.VMEM((nrow, 1), _F32),
            pltpu.VMEM((nrow, KV_LORA_RANK), _F32),
        ],
    )
    return pl.pallas_call(
        functools.partial(_sattn_kernel, n_pages),
        grid_spec=grid_spec,
        out_shape=jax.ShapeDtypeStruct((N_HEADS, n_seq * t_new, KV_LORA_RANK), _F32),
        compiler_params=pltpu.CompilerParams(
            dimension_semantics=("arbitrary",), vmem_limit_bytes=VMEM_LIMIT),
        name="sample_attention",
    )(page_table.reshape(-1), qcat, c_new, kr_new, wukt, cache_lat, cache_rope_t)


def _merge_kernel(from_latent, a_ref, gate_ref, gout_ref, x_ref, wuv_ref, woa_ref, wout_ref, o_ref):
    if from_latent:
        heads = [_dot(a_ref[hd].astype(_BF16), wuv_ref[:, hd * V_HEAD_DIM:(hd + 1) * V_HEAD_DIM])
                 for hd in range(N_HEADS)]
        attn = jnp.concatenate(heads, axis=-1).astype(_BF16)
    else:
        attn = a_ref[...]
    merged = gate_ref[...].astype(_F32) * _dot(attn, woa_ref[...]) + gout_ref[...]
    o_ref[...] = x_ref[...] + _dot(merged.astype(_BF16), wout_ref[...])


def _merge(from_latent, attn, gate, gout, x, wuv, woa, wout):
    n = x.shape[0]
    tm = MERGE_TM
    tok = pl.BlockSpec((tm, D_MODEL), lambda i: (i, 0))
    if from_latent:
        a_spec = pl.BlockSpec((N_HEADS, tm, KV_LORA_RANK), lambda i: (0, i, 0))
    else:
        a_spec = tok
    return pl.pallas_call(
        functools.partial(_merge_kernel, from_latent),
        grid=(n // tm,),
        in_specs=[a_spec, tok, tok, tok, _const_spec(wuv.shape), _const_spec(woa.shape),
                  _const_spec(wout.shape)],
        out_specs=tok,
        out_shape=jax.ShapeDtypeStruct((n, D_MODEL), _F32),
        compiler_params=pltpu.CompilerParams(
            dimension_semantics=("parallel",), vmem_limit_bytes=VMEM_LIMIT),
        name="merge_sample" if from_latent else "merge_prompt",
    )(attn, gate, gout, x, wuv, woa, wout)


def _swap_halves(a, axis):
    lo, hi = jnp.split(a, 2, axis=axis)
    return jnp.concatenate([hi, lo], axis=axis)


def _slot(a):
    return jnp.concatenate([a, jnp.zeros_like(a)], axis=-1)


def _rope_tables(pos):
    half = QK_ROPE_DIM // 2
    inv = ROPE_BASE ** (-np.arange(half, dtype=np.float64) / half)
    ang = np.asarray(pos, np.float64)[:, None] * inv[None, :]
    cos, sin = np.cos(ang), np.sin(ang)
    zero = np.zeros((ang.shape[0], QK_ROPE_DIM))
    return (jnp.asarray(np.concatenate([cos, cos, zero], axis=-1), _F32),
            jnp.asarray(np.concatenate([-sin, sin, zero], axis=-1), _F32))


def _prep_weights(w_in, w_uq, w_uk, w_uv, w_o_gmlp, mix_norm, q_lora_norm, kv_lora_norm, q_nope_norm,
                  q_rope_norm, k_nope_norm, k_rope_norm, gmlp_v_norm):
    w_uq = w_uq.astype(_BF16)
    off_kr = Q_LORA_RANK + KV_LORA_RANK
    w_in_t = w_in.T.astype(_BF16)
    k_r_t = w_in_t[off_kr:off_kr + QK_ROPE_DIM]
    zero = jnp.zeros_like(k_r_t)
    w_in_kr_t = jnp.concatenate([k_r_t, zero, _swap_halves(k_r_t, 0), zero], axis=0)
    assert w_in_kr_t.shape[0] == 2 * ROPE_SLOT and _D_IN_PADDED == _OFF_KR + 2 * ROPE_SLOT
    wq = w_uq.reshape(Q_LORA_RANK, N_HEADS, QK_HEAD_DIM)
    wq_nope = wq[:, :, :QK_NOPE_DIM].reshape(Q_LORA_RANK, -1)
    wq_rope = wq[:, :, QK_NOPE_DIM:]
    w_uq_p = jnp.concatenate([wq_nope, _slot(wq_rope).reshape(Q_LORA_RANK, -1),
                              _slot(_swap_halves(wq_rope, -1)).reshape(Q_LORA_RANK, -1)], axis=-1)
    row = lambda a: a.reshape(1, -1).astype(_F32)
    return dict(
        mix_norm=row(mix_norm), w_in_t=w_in_t, w_in_kr_t=w_in_kr_t, q_lora_norm=row(q_lora_norm),
        w_uq=w_uq_p.astype(_BF16), kv_lora_norm=row(kv_lora_norm), w_uk=w_uk.astype(_BF16),
        w_uv=w_uv.astype(_BF16), w_uv_t=w_uv.T.astype(_BF16), q_nope_norm=row(q_nope_norm), q_rope_norm=row(_slot(q_rope_norm)),
        q_rope_norm_s=row(_slot(_swap_halves(q_rope_norm, -1))), k_nope_norm=row(k_nope_norm),
        k_rope_norm=row(_slot(k_rope_norm)), k_rope_norm_s=row(_slot(_swap_halves(k_rope_norm, -1))),
        gmlp_v_norm=row(gmlp_v_norm), w_o_gmlp=w_o_gmlp.astype(_BF16))


def kernel(x_prompt, x_sample, cache_kv_latent, cache_k_rope, page_table, ffn1_norm, ffn1_w_gate, ffn1_w_up, ffn1_w_down, mix_norm, w_in, q_lora_norm, w_uq, kv_lora_norm, w_uk, w_uv, q_nope_norm, q_rope_norm, k_nope_norm, k_rope_norm, gmlp_v_norm, gmlp_w_s, gmlp_b_s, w_o_attn, w_o_gmlp, w_out, ffn2_norm, ffn2_w_gate, ffn2_w_up, ffn2_w_down):
    batch, seq, _ = x_prompt.shape
    n_seq, t_new, _ = x_sample.shape
    assert ffn1_norm.shape[0] == 1
    past = page_table.shape[1] * PAGE_SIZE
    n_pool = cache_kv_latent.shape[1]
    l = 0
    row = lambda a: a.reshape(1, -1).astype(_F32)
    w = _prep_weights(w_in[l], w_uq[l], w_uk[l], w_uv[l], w_o_gmlp[l], mix_norm[l], q_lora_norm[l],
                      kv_lora_norm[l], q_nope_norm[l], q_rope_norm[l], k_nope_norm[l], k_rope_norm[l],
                      gmlp_v_norm[l])
    ffn1 = (row(ffn1_norm[l]), ffn1_w_gate.reshape(D_MODEL, D_FF), ffn1_w_up.reshape(D_MODEL, D_FF),
            ffn1_w_down.reshape(D_FF, D_MODEL))
    ffn2 = (row(ffn2_norm[l]), ffn2_w_gate.reshape(D_MODEL, D_FF), ffn2_w_up.reshape(D_MODEL, D_FF),
            ffn2_w_down.reshape(D_FF, D_MODEL))
    woa = w_o_attn[l].astype(_BF16)
    wout = w_out[l].astype(_BF16)

    wp = dict(w, gmlp_w_s=gmlp_w_s[l].astype(_F32), gmlp_b_s=gmlp_b_s[l].reshape(GMLP_GROUPS, CHUNK, 1))
    cos_p, sin_p = _rope_tables(np.arange(seq))
    xp = _ffn(x_prompt.reshape(batch * seq, D_MODEL), *ffn1)
    q, k, v, ckv_p, kr_p, gate_p, gout_p = _proj(False, xp, wp, cos_p, sin_p, seq // PROJ_TM)
    attn_p = _prompt_attention(q, k, v, batch, seq)
    xp = _merge(False, attn_p, gate_p, gout_p, xp, w["w_uv"], woa, wout)
    xp = _ffn(xp, *ffn2)

    reps = CHUNK // t_new
    ws_s = jax.vmap(lambda m: jnp.kron(jnp.eye(reps, dtype=_F32), m))(gmlp_w_s[l][:, :t_new, :t_new])
    bs_s = jnp.tile(gmlp_b_s[l][:, :t_new], (1, reps)).reshape(GMLP_GROUPS, CHUNK, 1)
    ws = dict(w, gmlp_w_s=ws_s, gmlp_b_s=bs_s)
    cos_s, sin_s = _rope_tables(past + np.arange(PROJ_TM) % t_new)
    xs = _ffn(x_sample.reshape(n_seq * t_new, D_MODEL), *ffn1)
    qcat, ckv_s, kr_s, vn_s, gate_s, gout_s = _proj(True, xs, ws, cos_s, sin_s, 1)
    o_lat = _sample_attention(page_table, qcat, ckv_s, kr_s, w["w_uk"].T,
                              cache_kv_latent.reshape(n_pool, PAGE_SIZE, KV_LORA_RANK),
                              jnp.swapaxes(cache_k_rope.reshape(n_pool, PAGE_SIZE, QK_ROPE_DIM), 1, 2),
                              t_new)
    xs = _merge(True, o_lat, gate_s, gout_s, xs, w["w_uv"], woa, wout)
    xs = _ffn(xs, *ffn2)

    return (xp.reshape(batch, seq, D_MODEL), xs.reshape(n_seq, t_new, D_MODEL),
            ckv_p.reshape(1, batch, seq, KV_LORA_RANK), jnp.swapaxes(kr_p, 1, 2)[None],
            ckv_s.reshape(1, n_seq, t_new, KV_LORA_RANK), kr_s.reshape(1, n_seq, t_new, QK_ROPE_DIM),
            vn_s.reshape(1, n_seq, t_new, GMLP_WIDTH))
```

```python
import functools

import jax
import jax.numpy as jnp
import numpy as np
from jax import lax
from jax.experimental import pallas as pl
from jax.experimental.pallas import tpu as pltpu

D_MODEL = 1024
N_HEADS = 8
QK_NOPE_DIM = 128
QK_ROPE_DIM = 64
QK_HEAD_DIM = QK_NOPE_DIM + QK_ROPE_DIM
V_HEAD_DIM = 128
Q_LORA_RANK = 256
KV_LORA_RANK = 256
ROPE_BASE = 10000.0
ATTN_SCALE = QK_HEAD_DIM ** -0.5
LOG2E = 1.4426950408889634
GMLP_GROUPS = 4
GMLP_WIDTH = 1024
GMLP_GROUP_DIM = GMLP_WIDTH // GMLP_GROUPS
CHUNK = 128
PAGE_SIZE = 128
D_FF = 2816
EPS = 1e-6
NEG_INF = -1e30

LANE = 128
ROPE_SLOT = LANE

_OFF_CQ = 0
_OFF_CKV = _OFF_CQ + Q_LORA_RANK
_OFF_U = _OFF_CKV + KV_LORA_RANK
_OFF_V = _OFF_U + GMLP_WIDTH
_OFF_GA = _OFF_V + GMLP_WIDTH
_OFF_GG = _OFF_GA + D_MODEL
_OFF_KR = _OFF_GG + D_MODEL
_OFF_KRS = _OFF_KR + ROPE_SLOT
_D_IN_PADDED = _OFF_KRS + ROPE_SLOT

FFN_TM = 1024
FFN_TF = 256
PROJ_TM = 512
ATT_TQ = 512
ATT_TK = 512
ATT_HB = 4
ATT_ONES_ROWS = 16
MERGE_TM = 1024
SATT_PAGES_PER_CHUNK = 16
VMEM_LIMIT = 56 * 1024 * 1024

_BF16 = jnp.bfloat16
_F32 = jnp.float32


def _dot(a, b):
    return jnp.dot(a, b, preferred_element_type=_F32)


def _dot_t(a, b):
    return lax.dot_general(a, b, (((1,), (1,)), ((), ())), preferred_element_type=_F32)


def _rms(x, n):
    return lax.rsqrt(jnp.sum(x * x, axis=-1, keepdims=True) * (1.0 / n) + EPS)


def _ffn_kernel(x_ref, g_ref, wg_hbm, wu_hbm, wd_hbm, o_ref, wg_scr, wu_scr, wd_scr, stage_g, stage_u,
                stage_d, sem, act_scr):
    n_blk = D_FF // FFN_TF

    def block_copies(j, slot):
        cols = pl.ds(j * FFN_TF, FFN_TF)
        return [pltpu.make_async_copy(wg_hbm.at[:, cols], stage_g.at[slot], sem.at[slot]),
                pltpu.make_async_copy(wu_hbm.at[:, cols], stage_u.at[slot], sem.at[slot]),
                pltpu.make_async_copy(wd_hbm.at[cols, :], stage_d.at[slot], sem.at[slot])]

    def fetch_block(j):
        slot = j % 2
        if j + 1 < n_blk:
            for d in block_copies(j + 1, 1 - slot):
                d.start()
        for d in block_copies(j, slot):
            d.wait()
        sl = slice(j * FFN_TF, (j + 1) * FFN_TF)
        wg_scr[:, sl] = stage_g[slot].astype(_BF16)
        wu_scr[:, sl] = stage_u[slot].astype(_BF16)
        wd_scr[sl, :] = stage_d[slot].astype(_BF16)

    def tile(first_step):
        x = x_ref[...]
        h = (x * _rms(x, D_MODEL) * g_ref[...]).astype(_BF16)
        if first_step:
            for d in block_copies(0, 0):
                d.start()
        for j in range(n_blk):
            if first_step:
                fetch_block(j)
            sl = slice(j * FFN_TF, (j + 1) * FFN_TF)
            gate = _dot(h, wg_scr[:, sl])
            up = _dot(h, wu_scr[:, sl])
            act_scr[:, sl] = (gate * jax.nn.sigmoid(gate) * up).astype(_BF16)
        o_ref[...] = x + 0.5 * _dot(act_scr[...], wd_scr[...])

    pl.when(pl.program_id(0) == 0)(functools.partial(tile, True))
    pl.when(pl.program_id(0) > 0)(functools.partial(tile, False))


def _ffn(x, norm_g, wg, wu, wd):
    n = x.shape[0]
    tm = min(FFN_TM, n)
    any_spec = pl.BlockSpec(memory_space=pl.ANY)
    return pl.pallas_call(
        _ffn_kernel,
        grid=(n // tm,),
        in_specs=[pl.BlockSpec((tm, D_MODEL), lambda i: (i, 0)), _const_spec(norm_g.shape),
                  any_spec, any_spec, any_spec],
        out_specs=pl.BlockSpec((tm, D_MODEL), lambda i: (i, 0)),
        out_shape=jax.ShapeDtypeStruct((n, D_MODEL), _F32),
        scratch_shapes=[pltpu.VMEM((D_MODEL, D_FF), _BF16), pltpu.VMEM((D_MODEL, D_FF), _BF16),
                        pltpu.VMEM((D_FF, D_MODEL), _BF16),
                        pltpu.VMEM((2, D_MODEL, FFN_TF), _F32), pltpu.VMEM((2, D_MODEL, FFN_TF), _F32),
                        pltpu.VMEM((2, FFN_TF, D_MODEL), _F32),
                        pltpu.SemaphoreType.DMA((2,)),
                        pltpu.VMEM((tm, D_FF), _BF16)],
        compiler_params=pltpu.CompilerParams(
            dimension_semantics=("arbitrary",), vmem_limit_bytes=VMEM_LIMIT),
        name="ffn",
    )(x, norm_g, wg, wu, wd)


def _rope_slot(x, xs, g, gs, cos, sin):
    r = _rms(x, QK_ROPE_DIM)
    return (x * r * g) * cos + (xs * r * gs) * sin


def _gelu(x):
    return 0.5 * x * (1.0 + lax.erf(x * (2.0 ** -0.5)))


def _proj_kernel(sample, x_ref, mixg_ref, wt_ref, wkr_ref, qlg_ref, wuq_ref, kvg_ref, wuk_ref, wuv_ref,
                 qng_ref, qrg_ref, qrgs_ref, kng_ref, krg_ref, krgs_ref, vg_ref, ws_ref, bs_ref,
                 wog_ref, cos_ref, sin_ref, *rest):
    if sample:
        qcat_ref, ckv_ref, kr_ref, vn_ref, gate_ref, gout_ref, gm_scr = rest
    else:
        q_ref, k_ref, v_ref, ckv_ref, kr_ref, gate_ref, gout_ref, gm_scr = rest
    tm = x_ref.shape[0]
    x = x_ref[...]
    h = (x * _rms(x, D_MODEL) * mixg_ref[...]).astype(_BF16)
    cos = cos_ref[...]
    sin = sin_ref[...]

    def win(off, width):
        if off >= _OFF_KR:
            assert (off, width) == (_OFF_KR, 2 * ROPE_SLOT)
            return _dot_t(h, wkr_ref[...])
        row = off if off < _OFF_U else off + QK_ROPE_DIM
        return _dot_t(h, wt_ref[row:row + width, :])

    kr_raw = win(_OFF_KR, 2 * ROPE_SLOT)
    ckv = win(_OFF_CKV, KV_LORA_RANK)
    cq = win(_OFF_CQ, Q_LORA_RANK)
    u = win(_OFF_U, GMLP_WIDTH)
    v = win(_OFF_V, GMLP_WIDTH)

    kr = _rope_slot(kr_raw[:, :ROPE_SLOT], kr_raw[:, ROPE_SLOT:], krg_ref[...], krgs_ref[...],
                    cos, sin)[:, :QK_ROPE_DIM]
    if sample:
        kr_ref[...] = kr
    else:
        kr_ref[0] = kr.T
    ckv = ckv * _rms(ckv, KV_LORA_RANK) * kvg_ref[...]
    ckv_ref[...] = ckv
    ckv_b = ckv.astype(_BF16)
    cq_b = (cq * _rms(cq, Q_LORA_RANK) * qlg_ref[...]).astype(_BF16)

    nq = N_HEADS * QK_NOPE_DIM
    nr = N_HEADS * ROPE_SLOT
    if not sample:
        kexp = _dot(ckv_b, wuk_ref[...])
        vals_t = _dot_t(wuv_ref[...], ckv_b)
    q_nope = _dot(cq_b, wuq_ref[:, 0:nq])
    q_rope = _dot(cq_b, wuq_ref[:, nq:nq + nr])
    q_rope_s = _dot(cq_b, wuq_ref[:, nq + nr:nq + 2 * nr])
    gate_g = jax.nn.sigmoid(win(_OFF_GG, D_MODEL))
    gate_ref[...] = jax.nn.sigmoid(win(_OFF_GA, D_MODEL)).astype(gate_ref.dtype)

    if not sample:
        kr_b = kr.astype(_BF16)
        for hd in range(N_HEADS):
            sl = slice(hd * QK_NOPE_DIM, (hd + 1) * QK_NOPE_DIM)
            kh = kexp[:, sl]
            k_ref[hd, :, 0:QK_NOPE_DIM] = (kh * _rms(kh, QK_NOPE_DIM) * kng_ref[...]).astype(_BF16)
            k_ref[hd, :, QK_NOPE_DIM:QK_HEAD_DIM] = kr_b
            v_ref[hd] = vals_t[hd * V_HEAD_DIM:(hd + 1) * V_HEAD_DIM, :].astype(_BF16)

    q_scale = ATTN_SCALE if sample else ATTN_SCALE * LOG2E
    for hd in range(N_HEADS):
        sl = slice(hd * QK_NOPE_DIM, (hd + 1) * QK_NOPE_DIM)
        qh = q_nope[:, sl]
        qh = qh * _rms(qh, QK_NOPE_DIM) * qng_ref[...] * q_scale
        rs = slice(hd * ROPE_SLOT, (hd + 1) * ROPE_SLOT)
        qr = _rope_slot(q_rope[:, rs], q_rope_s[:, rs], qrg_ref[...], qrgs_ref[...], cos, sin)
        qr = qr[:, :QK_ROPE_DIM] * q_scale
        if sample:
            qa = _dot_t((qh * kng_ref[...]).astype(_BF16), wuk_ref[:, sl])
            qcat_ref[hd, :, 0:KV_LORA_RANK] = qa
            qcat_ref[hd, :, KV_LORA_RANK:KV_LORA_RANK + QK_ROPE_DIM] = qr
        else:
            q_ref[hd, :, 0:QK_NOPE_DIM] = qh.astype(_BF16)
            q_ref[hd, :, QK_NOPE_DIM:QK_HEAD_DIM] = qr.astype(_BF16)

    u = _gelu(u)
    v = _gelu(v)
    row = lax.broadcasted_iota(jnp.int32, (CHUNK, CHUNK), 0)
    col = lax.broadcasted_iota(jnp.int32, (CHUNK, CHUNK), 1)
    for g in range(GMLP_GROUPS):
        gs = slice(g * GMLP_GROUP_DIM, (g + 1) * GMLP_GROUP_DIM)
        vg = v[:, gs]
        vg = vg * _rms(vg, GMLP_GROUP_DIM) * vg_ref[:, gs]
        if sample:
            vn_ref[:, gs] = vg
        vg_b = vg.astype(_BF16)
        w = jnp.where(col <= row, ws_ref[g], 0.0).astype(_BF16)
        for c in range(tm // CHUNK):
            cs = slice(c * CHUNK, (c + 1) * CHUNK)
            mix = _dot(w, vg_b[cs]) + bs_ref[g]
            gm_scr[cs, gs] = (u[cs, gs] * mix).astype(_BF16)
    gout_ref[...] = (gate_g * _dot(gm_scr[...], wog_ref[...])).astype(gout_ref.dtype)


def _const_spec(shape):
    nd = len(shape)
    return pl.BlockSpec(shape, lambda i: (0,) * nd, pipeline_mode=pl.Buffered(1))


def _proj(sample, x, w, cos, sin, n_pos_tiles):
    n = x.shape[0]
    tm = PROJ_TM
    tok = lambda width: pl.BlockSpec((tm, width), lambda i: (i, 0))
    heads = lambda width: pl.BlockSpec((N_HEADS, tm, width), lambda i: (0, i, 0))
    weights = [w["mix_norm"], w["w_in_t"], w["w_in_kr_t"], w["q_lora_norm"], w["w_uq"],
               w["kv_lora_norm"], w["w_uk"],
               w["w_uv_t"], w["q_nope_norm"], w["q_rope_norm"], w["q_rope_norm_s"], w["k_nope_norm"],
               w["k_rope_norm"], w["k_rope_norm_s"], w["gmlp_v_norm"], w["gmlp_w_s"], w["gmlp_b_s"],
               w["w_o_gmlp"]]
    pos_spec = pl.BlockSpec((tm, ROPE_SLOT), lambda i: (i % n_pos_tiles, 0))
    in_specs = [tok(D_MODEL)] + [_const_spec(a.shape) for a in weights] + [pos_spec, pos_spec]
    sds = jax.ShapeDtypeStruct
    if sample:
        out_shape = [sds((N_HEADS, n, KV_LORA_RANK + QK_ROPE_DIM), _F32), sds((n, KV_LORA_RANK), _F32),
                     sds((n, QK_ROPE_DIM), _F32), sds((n, GMLP_WIDTH), _F32),
                     sds((n, D_MODEL), _BF16), sds((n, D_MODEL), _BF16)]
        out_specs = [heads(KV_LORA_RANK + QK_ROPE_DIM), tok(KV_LORA_RANK), tok(QK_ROPE_DIM),
                     tok(GMLP_WIDTH), tok(D_MODEL), tok(D_MODEL)]
    else:
        out_shape = [sds((N_HEADS, n, QK_HEAD_DIM), _BF16), sds((N_HEADS, n, QK_HEAD_DIM), _BF16),
                     sds((N_HEADS, V_HEAD_DIM, n), _BF16), sds((n, KV_LORA_RANK), _F32),
                     sds((n // (n_pos_tiles * tm), QK_ROPE_DIM, n_pos_tiles * tm), _F32),
                     sds((n, D_MODEL), _BF16), sds((n, D_MODEL), _BF16)]
        vt_spec = pl.BlockSpec((N_HEADS, V_HEAD_DIM, tm), lambda i: (0, 0, i))
        krt_spec = pl.BlockSpec((1, QK_ROPE_DIM, tm), lambda i: (i // n_pos_tiles, 0, i % n_pos_tiles))
        out_specs = [heads(QK_HEAD_DIM), heads(QK_HEAD_DIM), vt_spec, tok(KV_LORA_RANK),
                     krt_spec, tok(D_MODEL), tok(D_MODEL)]
    return pl.pallas_call(
        functools.partial(_proj_kernel, sample),
        grid=(n // tm,),
        in_specs=in_specs,
        out_specs=out_specs,
        out_shape=out_shape,
        scratch_shapes=[pltpu.VMEM((tm, GMLP_WIDTH), _BF16)],
        compiler_params=pltpu.CompilerParams(
            dimension_semantics=("parallel",), vmem_limit_bytes=VMEM_LIMIT),
        name="proj_sample" if sample else "proj_prompt",
    )(x, *weights, cos, sin)


def _softmax_step(s, v_b, m_scr, l_scr, acc_scr):
    m_prev = m_scr[...]
    m_new = jnp.maximum(m_prev, jnp.max(s, axis=-1, keepdims=True))
    alpha = jnp.exp(m_prev - m_new)
    p = jnp.exp(s - m_new)
    l_scr[...] = alpha * l_scr[...] + jnp.sum(p, axis=-1, keepdims=True)
    acc_scr[...] = alpha * acc_scr[...] + _dot(p.astype(_BF16), v_b)
    m_scr[...] = m_new


def _pattn_kernel(q_ref, k_ref, vt_ref, o_ref, sa_scr, sb_scr, m_scr, acc_scr):
    qi = pl.program_id(2)
    m_scr[...] = jnp.full_like(m_scr, NEG_INF)
    acc_scr[...] = jnp.zeros_like(acc_scr)

    def scores_head(kb, buf, masked, hh):
        off = pl.multiple_of(kb * ATT_TK, ATT_TK)
        s = _dot_t(k_ref[hh, pl.ds(off, ATT_TK), :], q_ref[hh])
        if masked:
            key = lax.broadcasted_iota(jnp.int32, s.shape, 0)
            qry = lax.broadcasted_iota(jnp.int32, s.shape, 1)
            s = jnp.where(key <= qry, s, NEG_INF)
        buf[hh] = s

    def update_head(kb, buf, hh):
        off = pl.multiple_of(kb * ATT_TK, ATT_TK)
        s = buf[hh]
        m_prev = m_scr[hh]
        m_new = jnp.maximum(m_prev, jnp.max(s, axis=0, keepdims=True))
        p = jnp.exp2(s - m_new).astype(_BF16)
        vt = vt_ref[hh, :, pl.ds(off, ATT_TK)]
        vt1 = jnp.concatenate([vt, jnp.ones((ATT_ONES_ROWS, ATT_TK), _BF16)], axis=0)
        acc_scr[hh] = jnp.exp2(m_prev - m_new) * acc_scr[hh] + _dot(vt1, p)
        m_scr[hh] = m_new

    def scores(kb, buf, masked):
        for hh in range(ATT_HB):
            scores_head(kb, buf, masked, hh)

    def update(kb, buf):
        for hh in range(ATT_HB):
            update_head(kb, buf, hh)

    def trip(t, ybuf, xbuf, masked):
        for hh in range(ATT_HB):
            scores_head(t, xbuf, masked, hh)
            update_head(t - 1, ybuf, hh)

    @pl.when(qi == 0)
    def _():
        scores(0, sa_scr, True)
        update(0, sa_scr)

    @pl.when(qi > 0)
    def _():
        scores(0, sa_scr, False)

        def pair(j, carry):
            trip(2 * j + 1, sa_scr, sb_scr, False)
            trip(2 * j + 2, sb_scr, sa_scr, False)
            return carry

        lax.fori_loop(0, lax.shift_right_logical(qi - 1, 1), pair, 0)

        @pl.when((qi & 1) == 1)
        def _():
            trip(qi, sa_scr, sb_scr, True)
            update(qi, sb_scr)

        @pl.when((qi & 1) == 0)
        def _():
            trip(qi - 1, sa_scr, sb_scr, False)
            trip(qi, sb_scr, sa_scr, True)
            update(qi, sa_scr)

    for hh in range(ATT_HB):
        acc = acc_scr[hh]
        o_t = acc[:V_HEAD_DIM] / acc[V_HEAD_DIM:V_HEAD_DIM + 1]
        o_ref[:, hh * V_HEAD_DIM:(hh + 1) * V_HEAD_DIM] = o_t.T.astype(o_ref.dtype)


def _prompt_attention(q, k, vt, batch, seq):
    assert ATT_TQ == ATT_TK
    nq = seq // ATT_TQ
    hb = ATT_HB
    return pl.pallas_call(
        _pattn_kernel,
        grid=(batch, N_HEADS // hb, nq),
        in_specs=[
            pl.BlockSpec((hb, ATT_TQ, QK_HEAD_DIM), lambda b, h, i: (h, b * nq + i, 0)),
            pl.BlockSpec((hb, seq, QK_HEAD_DIM), lambda b, h, i: (h, b, 0)),
            pl.BlockSpec((hb, V_HEAD_DIM, seq), lambda b, h, i: (h, 0, b)),
        ],
        out_specs=pl.BlockSpec((ATT_TQ, hb * V_HEAD_DIM), lambda b, h, i: (b * nq + i, h)),
        out_shape=jax.ShapeDtypeStruct((batch * seq, N_HEADS * V_HEAD_DIM), _BF16),
        scratch_shapes=[pltpu.VMEM((hb, ATT_TK, ATT_TQ), _F32),
                        pltpu.VMEM((hb, ATT_TK, ATT_TQ), _F32),
                        pltpu.VMEM((hb, 1, ATT_TQ), _F32),
                        pltpu.VMEM((hb, V_HEAD_DIM + ATT_ONES_ROWS, ATT_TQ), _F32)],
        compiler_params=pltpu.CompilerParams(
            dimension_semantics=("parallel", "parallel", "arbitrary"), vmem_limit_bytes=VMEM_LIMIT),
        name="prompt_attention",
    )(q, k, vt)


def _sattn_kernel(n_pages, pt_ref, qcat_ref, cnew_ref, krnew_ref, wukt_ref, lat_hbm, ropet_hbm, o_ref,
                  a_scr, lat_buf, ropet_buf, sca_scr, scb_scr, scl_scr, cbl_scr, sem_lat, sem_rope,
                  m_scr, l_scr, acc_scr):
    pages = SATT_PAGES_PER_CHUNK
    n_chunks = n_pages // pages
    tk = pages * PAGE_SIZE
    nk = N_HEADS * QK_NOPE_DIM
    nrow = N_HEADS * 8
    s = pl.program_id(0)
    n_seq = pl.num_programs(0) - 1
    slot = lax.rem(s, 2)

    def copies(seq, dst_slot):
        out = []
        for g in range(n_pages):
            page = 0 if seq is None else pt_ref[seq * n_pages + g]
            dst = pl.ds(g * PAGE_SIZE, PAGE_SIZE)
            out.append(pltpu.make_async_copy(lat_hbm.at[page], lat_buf.at[dst_slot, dst], sem_lat.at[dst_slot]))
            out.append(pltpu.make_async_copy(ropet_hbm.at[page], ropet_buf.at[dst_slot, :, dst],
                                             sem_rope.at[dst_slot]))
        return out

    @pl.when(s == 0)
    def _():
        a_scr[0:nk, :] = wukt_ref[...]
        scl_scr[...] = jnp.zeros_like(scl_scr)
        cbl_scr[...] = jnp.zeros_like(cbl_scr)
        m_scr[...] = jnp.zeros_like(m_scr)
        l_scr[...] = jnp.ones_like(l_scr)
        acc_scr[...] = jnp.zeros_like(acc_scr)
        for d in copies(0, 0):
            d.start()

    @pl.when(s < n_seq)
    def _():
        for d in copies(jnp.minimum(s + 1, n_seq - 1), 1 - slot):
            d.start()

    for d in copies(None, slot):
        d.wait()

    q2 = qcat_ref[...].reshape(nrow, KV_LORA_RANK + QK_ROPE_DIM)
    a_scr[nk:nk + nrow, :] = q2[:, :KV_LORA_RANK].astype(_BF16)
    q_rope = q2[:, KV_LORA_RANK:].astype(_BF16)

    def scores(c_b, s_rope):
        big = _dot_t(a_scr[...], c_b)
        rows = []
        for hd in range(N_HEADS):
            kx = big[hd * QK_NOPE_DIM:(hd + 1) * QK_NOPE_DIM, :]
            r = lax.rsqrt(jnp.sum(kx * kx, axis=0, keepdims=True) * (1.0 / QK_NOPE_DIM) + EPS)
            rows.append(big[nk + 8 * hd:nk + 8 * hd + 8, :] * r + s_rope[8 * hd:8 * hd + 8, :])
        return jnp.concatenate(rows, axis=0)

    sc_bufs = [sca_scr, scb_scr] * (n_chunks // 2 + 1)
    sc_bufs = sc_bufs[:n_chunks - 1] + [scl_scr]
    pad = PAGE_SIZE - cnew_ref.shape[0]

    def latent(c):
        return lat_buf[slot, pl.ds(c * tk, tk), :].astype(_BF16)

    def stage_scores(c):
        s_rope = _dot(q_rope, ropet_buf[slot, :, pl.ds(c * tk, tk)].astype(_BF16))
        if c < n_chunks - 1:
            sc_bufs[c][...] = scores(latent(c), s_rope)
        else:
            c_new = jnp.concatenate([cnew_ref[...], jnp.zeros((pad, KV_LORA_RANK), _F32)], axis=0)
            c_b = jnp.concatenate([latent(c), c_new.astype(_BF16)], axis=0)
            cbl_scr[...] = c_b
            kr_new = jnp.concatenate([krnew_ref[...], jnp.zeros((pad, QK_ROPE_DIM), _F32)], axis=0)
            s_rope = jnp.concatenate([s_rope, _dot_t(q_rope, kr_new.astype(_BF16))], axis=1)
            row = lax.broadcasted_iota(jnp.int32, (nrow, tk + PAGE_SIZE), 0)
            col = lax.broadcasted_iota(jnp.int32, (nrow, tk + PAGE_SIZE), 1)
            sc_bufs[c][...] = jnp.where(col - tk <= (row & 7), scores(c_b, s_rope), NEG_INF)

    def stage_update(c):
        _softmax_step(sc_bufs[c][...], latent(c), m_scr, l_scr, acc_scr)

    stage_scores(0)

    _softmax_step(scl_scr[...], cbl_scr[...], m_scr, l_scr, acc_scr)
    o_ref[...] = (acc_scr[...] / l_scr[...]).reshape(o_ref.shape)
    m_scr[...] = jnp.full_like(m_scr, NEG_INF)
    l_scr[...] = jnp.zeros_like(l_scr)
    acc_scr[...] = jnp.zeros_like(acc_scr)

    for c in range(1, n_chunks):
        stage_scores(c)
        stage_update(c - 1)


def _sample_attention(page_table, qcat, c_new, kr_new, wukt, cache_lat, cache_rope_t, t_new):
    n_seq, n_pages = page_table.shape
    assert t_new == 8 and n_pages % SATT_PAGES_PER_CHUNK == 0 and n_pages // SATT_PAGES_PER_CHUNK >= 2
    tk = SATT_PAGES_PER_CHUNK * PAGE_SIZE
    past = n_pages * PAGE_SIZE
    nrow = N_HEADS * t_new
    dq = KV_LORA_RANK + QK_ROPE_DIM
    cur = lambda s: jnp.minimum(s, n_seq - 1)
    prev = lambda s: jnp.maximum(s - 1, 0)
    grid_spec = pltpu.PrefetchScalarGridSpec(
        num_scalar_prefetch=1,
        grid=(n_seq + 1,),
        in_specs=[
            pl.BlockSpec((N_HEADS, t_new, dq), lambda s, pt: (0, cur(s), 0)),
            pl.BlockSpec((t_new, KV_LORA_RANK), lambda s, pt: (cur(s), 0)),
            pl.BlockSpec((t_new, QK_ROPE_DIM), lambda s, pt: (cur(s), 0)),
            pl.BlockSpec((N_HEADS * QK_NOPE_DIM, KV_LORA_RANK), lambda s, pt: (0, 0)),
            pl.BlockSpec(memory_space=pl.ANY),
            pl.BlockSpec(memory_space=pl.ANY),
        ],
        out_specs=pl.BlockSpec((N_HEADS, t_new, KV_LORA_RANK), lambda s, pt: (0, prev(s), 0)),
        scratch_shapes=[
            pltpu.VMEM((N_HEADS * QK_NOPE_DIM + nrow, KV_LORA_RANK), _BF16),
            pltpu.VMEM((2, past, KV_LORA_RANK), _F32),
            pltpu.VMEM((2, QK_ROPE_DIM, past), _F32),
            pltpu.VMEM((nrow, tk), _F32),
            pltpu.VMEM((nrow, tk), _F32),
            pltpu.VMEM((nrow, tk + PAGE_SIZE), _F32),
            pltpu.VMEM((tk + PAGE_SIZE, KV_LORA_RANK), _BF16),
            pltpu.SemaphoreType.DMA((2,)),
            pltpu.SemaphoreType.DMA((2,)),
            pltpu.VMEM((nrow, 1), _F32),
            pltpu.VMEM((nrow, 1), _F32),
            pltpu.VMEM((nrow, KV_LORA_RANK), _F32),
        ],
    )
    return pl.pallas_call(
        functools.partial(_sattn_kernel, n_pages),
        grid_spec=grid_spec,
        out_shape=jax.ShapeDtypeStruct((N_HEADS, n_seq * t_new, KV_LORA_RANK), _F32),
        compiler_params=pltpu.CompilerParams(
            dimension_semantics=("arbitrary",), vmem_limit_bytes=VMEM_LIMIT),
        name="sample_attention",
    )(page_table.reshape(-1), qcat, c_new, kr_new, wukt, cache_lat, cache_rope_t)


def _merge_kernel(from_latent, a_ref, gate_ref, gout_ref, x_ref, wuv_ref, woa_ref, wout_ref, o_ref):
    if from_latent:
        heads = [_dot(a_ref[hd].astype(_BF16), wuv_ref[:, hd * V_HEAD_DIM:(hd + 1) * V_HEAD_DIM])
                 for hd in range(N_HEADS)]
        attn = jnp.concatenate(heads, axis=-1).astype(_BF16)
    else:
        attn = a_ref[...]
    merged = gate_ref[...].astype(_F32) * _dot(attn, woa_ref[...]) + gout_ref[...]
    o_ref[...] = x_ref[...] + _dot(merged.astype(_BF16), wout_ref[...])


def _merge(from_latent, attn, gate, gout, x, wuv, woa, wout):
    n = x.shape[0]
    tm = min(MERGE_TM, n // 2)
    tok = pl.BlockSpec((tm, D_MODEL), lambda i: (i, 0))
    if from_latent:
        a_spec = pl.BlockSpec((N_HEADS, tm, KV_LORA_RANK), lambda i: (0, i, 0))
    else:
        a_spec = tok
    return pl.pallas_call(
        functools.partial(_merge_kernel, from_latent),
        grid=(n // tm,),
        in_specs=[a_spec, tok, tok, tok, _const_spec(wuv.shape), _const_spec(woa.shape),
                  _const_spec(wout.shape)],
        out_specs=tok,
        out_shape=jax.ShapeDtypeStruct((n, D_MODEL), _F32),
        compiler_params=pltpu.CompilerParams(
            dimension_semantics=("parallel",), vmem_limit_bytes=VMEM_LIMIT),
        name="merge_sample" if from_latent else "merge_prompt",
    )(attn, gate, gout, x, wuv, woa, wout)


def _swap_halves(a, axis):
    lo, hi = jnp.split(a, 2, axis=axis)
    return jnp.concatenate([hi, lo], axis=axis)


def _slot(a):
    return jnp.concatenate([a, jnp.zeros_like(a)], axis=-1)


def _rope_tables(pos):
    half = QK_ROPE_DIM // 2
    inv = ROPE_BASE ** (-np.arange(half, dtype=np.float64) / half)
    ang = np.asarray(pos, np.float64)[:, None] * inv[None, :]
    cos, sin = np.cos(ang), np.sin(ang)
    zero = np.zeros((ang.shape[0], QK_ROPE_DIM))
    return (jnp.asarray(np.concatenate([cos, cos, zero], axis=-1), _F32),
            jnp.asarray(np.concatenate([-sin, sin, zero], axis=-1), _F32))


def _prep_weights(w_in, w_uq, w_uk, w_uv, w_o_gmlp, mix_norm, q_lora_norm, kv_lora_norm, q_nope_norm,
                  q_rope_norm, k_nope_norm, k_rope_norm, gmlp_v_norm):
    w_uq = w_uq.astype(_BF16)
    off_kr = Q_LORA_RANK + KV_LORA_RANK
    w_in_t = w_in.T.astype(_BF16)
    k_r_t = w_in_t[off_kr:off_kr + QK_ROPE_DIM]
    zero = jnp.zeros_like(k_r_t)
    w_in_kr_t = jnp.concatenate([k_r_t, zero, _swap_halves(k_r_t, 0), zero], axis=0)
    assert w_in_kr_t.shape[0] == 2 * ROPE_SLOT and _D_IN_PADDED == _OFF_KR + 2 * ROPE_SLOT
    wq = w_uq.reshape(Q_LORA_RANK, N_HEADS, QK_HEAD_DIM)
    wq_nope = wq[:, :, :QK_NOPE_DIM].reshape(Q_LORA_RANK, -1)
    wq_rope = wq[:, :, QK_NOPE_DIM:]
    w_uq_p = jnp.concatenate([wq_nope, _slot(wq_rope).reshape(Q_LORA_RANK, -1),
                              _slot(_swap_halves(wq_rope, -1)).reshape(Q_LORA_RANK, -1)], axis=-1)
    row = lambda a: a.reshape(1, -1).astype(_F32)
    return dict(
        mix_norm=row(mix_norm), w_in_t=w_in_t, w_in_kr_t=w_in_kr_t, q_lora_norm=row(q_lora_norm),
        w_uq=w_uq_p.astype(_BF16), kv_lora_norm=row(kv_lora_norm), w_uk=w_uk.astype(_BF16),
        w_uv=w_uv.astype(_BF16), w_uv_t=w_uv.T.astype(_BF16), q_nope_norm=row(q_nope_norm), q_rope_norm=row(_slot(q_rope_norm)),
        q_rope_norm_s=row(_slot(_swap_halves(q_rope_norm, -1))), k_nope_norm=row(k_nope_norm),
        k_rope_norm=row(_slot(k_rope_norm)), k_rope_norm_s=row(_slot(_swap_halves(k_rope_norm, -1))),
        gmlp_v_norm=row(gmlp_v_norm), w_o_gmlp=w_o_gmlp.astype(_BF16))


def kernel(x_prompt, x_sample, cache_kv_latent, cache_k_rope, page_table, ffn1_norm, ffn1_w_gate, ffn1_w_up, ffn1_w_down, mix_norm, w_in, q_lora_norm, w_uq, kv_lora_norm, w_uk, w_uv, q_nope_norm, q_rope_norm, k_nope_norm, k_rope_norm, gmlp_v_norm, gmlp_w_s, gmlp_b_s, w_o_attn, w_o_gmlp, w_out, ffn2_norm, ffn2_w_gate, ffn2_w_up, ffn2_w_down):
    batch, seq, _ = x_prompt.shape
    n_seq, t_new, _ = x_sample.shape
    assert ffn1_norm.shape[0] == 1
    past = page_table.shape[1] * PAGE_SIZE
    n_pool = cache_kv_latent.shape[1]
    l = 0
    row = lambda a: a.reshape(1, -1).astype(_F32)
    w = _prep_weights(w_in[l], w_uq[l], w_uk[l], w_uv[l], w_o_gmlp[l], mix_norm[l], q_lora_norm[l],
                      kv_lora_norm[l], q_nope_norm[l], q_rope_norm[l], k_nope_norm[l], k_rope_norm[l],
                      gmlp_v_norm[l])
    ffn1 = (row(ffn1_norm[l]), ffn1_w_gate.reshape(D_MODEL, D_FF), ffn1_w_up.reshape(D_MODEL, D_FF),
            ffn1_w_down.reshape(D_FF, D_MODEL))
    ffn2 = (row(ffn2_norm[l]), ffn2_w_gate.reshape(D_MODEL, D_FF), ffn2_w_up.reshape(D_MODEL, D_FF),
            ffn2_w_down.reshape(D_FF, D_MODEL))
    woa = w_o_attn[l].astype(_BF16)
    wout = w_out[l].astype(_BF16)

    wp = dict(w, gmlp_w_s=gmlp_w_s[l].astype(_F32), gmlp_b_s=gmlp_b_s[l].reshape(GMLP_GROUPS, CHUNK, 1))
    cos_p, sin_p = _rope_tables(np.arange(seq))
    xp = _ffn(x_prompt.reshape(batch * seq, D_MODEL), *ffn1)
    q, k, v, ckv_p, kr_p, gate_p, gout_p = _proj(False, xp, wp, cos_p, sin_p, seq // PROJ_TM)
    attn_p = _prompt_attention(q, k, v, batch, seq)
    xp = _merge(False, attn_p, gate_p, gout_p, xp, w["w_uv"], woa, wout)
    xp = _ffn(xp, *ffn2)

    reps = CHUNK // t_new
    ws_s = jax.vmap(lambda m: jnp.kron(jnp.eye(reps, dtype=_F32), m))(gmlp_w_s[l][:, :t_new, :t_new])
    bs_s = jnp.tile(gmlp_b_s[l][:, :t_new], (1, reps)).reshape(GMLP_GROUPS, CHUNK, 1)
    ws = dict(w, gmlp_w_s=ws_s, gmlp_b_s=bs_s)
    cos_s, sin_s = _rope_tables(past + np.arange(PROJ_TM) % t_new)
    xs = _ffn(x_sample.reshape(n_seq * t_new, D_MODEL), *ffn1)
    qcat, ckv_s, kr_s, vn_s, gate_s, gout_s = _proj(True, xs, ws, cos_s, sin_s, 1)
    o_lat = _sample_attention(page_table, qcat, ckv_s, kr_s, w["w_uk"].T,
                              cache_kv_latent.reshape(n_pool, PAGE_SIZE, KV_LORA_RANK),
                              jnp.swapaxes(cache_k_rope.reshape(n_pool, PAGE_SIZE, QK_ROPE_DIM), 1, 2),
                              t_new)
    xs = _merge(True, o_lat, gate_s, gout_s, xs, w["w_uv"], woa, wout)
    xs = _ffn(xs, *ffn2)

    return (xp.reshape(batch, seq, D_MODEL), xs.reshape(n_seq, t_new, D_MODEL),
            ckv_p.reshape(1, batch, seq, KV_LORA_RANK), jnp.swapaxes(kr_p, 1, 2)[None],
            ckv_s.reshape(1, n_seq, t_new, KV_LORA_RANK), kr_s.reshape(1, n_seq, t_new, QK_ROPE_DIM),
            vn_s.reshape(1, n_seq, t_new, GMLP_WIDTH))
```

```python
import functools

import jax
import jax.numpy as jnp
import numpy as np
from jax import lax
from jax.experimental import pallas as pl
from jax.experimental.pallas import tpu as pltpu

D_MODEL = 1024
N_HEADS = 8
QK_NOPE_DIM = 128
QK_ROPE_DIM = 64
QK_HEAD_DIM = QK_NOPE_DIM + QK_ROPE_DIM
V_HEAD_DIM = 128
Q_LORA_RANK = 256
KV_LORA_RANK = 256
ROPE_BASE = 10000.0
ATTN_SCALE = QK_HEAD_DIM ** -0.5
LOG2E = 1.4426950408889634
GMLP_GROUPS = 4
GMLP_WIDTH = 1024
GMLP_GROUP_DIM = GMLP_WIDTH // GMLP_GROUPS
CHUNK = 128
PAGE_SIZE = 128
D_FF = 2816
EPS = 1e-6
NEG_INF = -1e30

LANE = 128
ROPE_SLOT = LANE

_OFF_CQ = 0
_OFF_CKV = _OFF_CQ + Q_LORA_RANK
_OFF_U = _OFF_CKV + KV_LORA_RANK
_OFF_V = _OFF_U + GMLP_WIDTH
_OFF_GA = _OFF_V + GMLP_WIDTH
_OFF_GG = _OFF_GA + D_MODEL
_OFF_KR = _OFF_GG + D_MODEL
_OFF_KRS = _OFF_KR + ROPE_SLOT
_D_IN_PADDED = _OFF_KRS + ROPE_SLOT

FFN_TM = 1024
FFN_TF = 256
PROJ_TM = 512
ATT_TQ = 512
ATT_TK = 512
ATT_HB = 4
ATT_ONES_ROWS = 16
MERGE_TM = 1024
SATT_PAGES_PER_CHUNK = 16
VMEM_LIMIT = 56 * 1024 * 1024

_BF16 = jnp.bfloat16
_F32 = jnp.float32


def _dot(a, b):
    return jnp.dot(a, b, preferred_element_type=_F32)


def _dot_t(a, b):
    return lax.dot_general(a, b, (((1,), (1,)), ((), ())), preferred_element_type=_F32)


def _rms(x, n):
    return lax.rsqrt(jnp.sum(x * x, axis=-1, keepdims=True) * (1.0 / n) + EPS)


def _ffn_kernel(x_ref, g_ref, wg_hbm, wu_hbm, wd_hbm, o_ref, wg_scr, wu_scr, wd_scr, stage_g, stage_u,
                stage_d, sem, act_scr):
    n_blk = D_FF // FFN_TF

    def block_copies(j, slot):
        cols = pl.ds(j * FFN_TF, FFN_TF)
        return [pltpu.make_async_copy(wg_hbm.at[:, cols], stage_g.at[slot], sem.at[slot]),
                pltpu.make_async_copy(wu_hbm.at[:, cols], stage_u.at[slot], sem.at[slot]),
                pltpu.make_async_copy(wd_hbm.at[cols, :], stage_d.at[slot], sem.at[slot])]

    def fetch_block(j):
        slot = j % 2
        if j + 1 < n_blk:
            for d in block_copies(j + 1, 1 - slot):
                d.start()
        for d in block_copies(j, slot):
            d.wait()
        sl = slice(j * FFN_TF, (j + 1) * FFN_TF)
        wg_scr[:, sl] = stage_g[slot].astype(_BF16)
        wu_scr[:, sl] = stage_u[slot].astype(_BF16)
        wd_scr[sl, :] = stage_d[slot].astype(_BF16)

    def tile(first_step):
        x = x_ref[...]
        h = (x * _rms(x, D_MODEL) * g_ref[...]).astype(_BF16)
        if first_step:
            for d in block_copies(0, 0):
                d.start()
        for j in range(n_blk):
            if first_step:
                fetch_block(j)
            sl = slice(j * FFN_TF, (j + 1) * FFN_TF)
            gate = _dot(h, wg_scr[:, sl])
            up = _dot(h, wu_scr[:, sl])
            act_scr[:, sl] = (gate * jax.nn.sigmoid(gate) * up).astype(_BF16)
        o_ref[...] = x + 0.5 * _dot(act_scr[...], wd_scr[...])

    pl.when(pl.program_id(0) == 0)(functools.partial(tile, True))
    pl.when(pl.program_id(0) > 0)(functools.partial(tile, False))


def _ffn(x, norm_g, wg, wu, wd):
    n = x.shape[0]
    tm = min(FFN_TM, n // 2)
    any_spec = pl.BlockSpec(memory_space=pl.ANY)
    return pl.pallas_call(
        _ffn_kernel,
        grid=(n // tm,),
        in_specs=[pl.BlockSpec((tm, D_MODEL), lambda i: (i, 0)), _const_spec(norm_g.shape),
                  any_spec, any_spec, any_spec],
        out_specs=pl.BlockSpec((tm, D_MODEL), lambda i: (i, 0)),
        out_shape=jax.ShapeDtypeStruct((n, D_MODEL), _F32),
        scratch_shapes=[pltpu.VMEM((D_MODEL, D_FF), _BF16), pltpu.VMEM((D_MODEL, D_FF), _BF16),
                        pltpu.VMEM((D_FF, D_MODEL), _BF16),
                        pltpu.VMEM((2, D_MODEL, FFN_TF), _F32), pltpu.VMEM((2, D_MODEL, FFN_TF), _F32),
                        pltpu.VMEM((2, FFN_TF, D_MODEL), _F32),
                        pltpu.SemaphoreType.DMA((2,)),
                        pltpu.VMEM((tm, D_FF), _BF16)],
        compiler_params=pltpu.CompilerParams(
            dimension_semantics=("arbitrary",), vmem_limit_bytes=VMEM_LIMIT),
        name="ffn",
    )(x, norm_g, wg, wu, wd)


def _rope_slot(x, xs, g, gs, cos, sin):
    r = _rms(x, QK_ROPE_DIM)
    return (x * r * g) * cos + (xs * r * gs) * sin


def _gelu(x):
    return 0.5 * x * (1.0 + lax.erf(x * (2.0 ** -0.5)))


def _proj_kernel(sample, x_ref, mixg_ref, wt_ref, wkr_ref, qlg_ref, wuq_ref, kvg_ref, wuk_ref, wuv_ref,
                 qng_ref, qrg_ref, qrgs_ref, kng_ref, krg_ref, krgs_ref, vg_ref, ws_ref, bs_ref,
                 wog_ref, cos_ref, sin_ref, *rest):
    if sample:
        qcat_ref, ckv_ref, kr_ref, vn_ref, gate_ref, gout_ref, gm_scr = rest
    else:
        q_ref, k_ref, v_ref, ckv_ref, kr_ref, gate_ref, gout_ref, gm_scr = rest
    tm = x_ref.shape[0]
    x = x_ref[...]
    h = (x * _rms(x, D_MODEL) * mixg_ref[...]).astype(_BF16)
    cos = cos_ref[...]
    sin = sin_ref[...]

    def win(off, width):
        if off >= _OFF_KR:
            assert (off, width) == (_OFF_KR, 2 * ROPE_SLOT)
            return _dot_t(h, wkr_ref[...])
        row = off if off < _OFF_U else off + QK_ROPE_DIM
        return _dot_t(h, wt_ref[row:row + width, :])

    kr_raw = win(_OFF_KR, 2 * ROPE_SLOT)
    ckv = win(_OFF_CKV, KV_LORA_RANK)
    cq = win(_OFF_CQ, Q_LORA_RANK)
    u = win(_OFF_U, GMLP_WIDTH)
    v = win(_OFF_V, GMLP_WIDTH)

    kr = _rope_slot(kr_raw[:, :ROPE_SLOT], kr_raw[:, ROPE_SLOT:], krg_ref[...], krgs_ref[...],
                    cos, sin)[:, :QK_ROPE_DIM]
    if sample:
        kr_ref[...] = kr
    else:
        kr_ref[0] = kr.T
    ckv = ckv * _rms(ckv, KV_LORA_RANK) * kvg_ref[...]
    ckv_ref[...] = ckv
    ckv_b = ckv.astype(_BF16)
    cq_b = (cq * _rms(cq, Q_LORA_RANK) * qlg_ref[...]).astype(_BF16)

    nq = N_HEADS * QK_NOPE_DIM
    nr = N_HEADS * ROPE_SLOT
    if not sample:
        kexp = _dot(ckv_b, wuk_ref[...])
        vals_t = _dot_t(wuv_ref[...], ckv_b)
    q_nope = _dot(cq_b, wuq_ref[:, 0:nq])
    q_rope = _dot(cq_b, wuq_ref[:, nq:nq + nr])
    q_rope_s = _dot(cq_b, wuq_ref[:, nq + nr:nq + 2 * nr])
    gate_g = jax.nn.sigmoid(win(_OFF_GG, D_MODEL))
    gate_ref[...] = jax.nn.sigmoid(win(_OFF_GA, D_MODEL)).astype(gate_ref.dtype)

    if not sample:
        kr_b = kr.astype(_BF16)
        for hd in range(N_HEADS):
            sl = slice(hd * QK_NOPE_DIM, (hd + 1) * QK_NOPE_DIM)
            kh = kexp[:, sl]
            k_ref[hd, :, 0:QK_NOPE_DIM] = (kh * _rms(kh, QK_NOPE_DIM) * kng_ref[...]).astype(_BF16)
            k_ref[hd, :, QK_NOPE_DIM:QK_HEAD_DIM] = kr_b
            v_ref[hd] = vals_t[hd * V_HEAD_DIM:(hd + 1) * V_HEAD_DIM, :].astype(_BF16)

    q_scale = ATTN_SCALE if sample else ATTN_SCALE * LOG2E
    for hd in range(N_HEADS):
        sl = slice(hd * QK_NOPE_DIM, (hd + 1) * QK_NOPE_DIM)
        qh = q_nope[:, sl]
        qh = qh * _rms(qh, QK_NOPE_DIM) * qng_ref[...] * q_scale
        rs = slice(hd * ROPE_SLOT, (hd + 1) * ROPE_SLOT)
        qr = _rope_slot(q_rope[:, rs], q_rope_s[:, rs], qrg_ref[...], qrgs_ref[...], cos, sin)
        qr = qr[:, :QK_ROPE_DIM] * q_scale
        if sample:
            qa = _dot_t((qh * kng_ref[...]).astype(_BF16), wuk_ref[:, sl])
            qcat_ref[hd, :, 0:KV_LORA_RANK] = qa
            qcat_ref[hd, :, KV_LORA_RANK:KV_LORA_RANK + QK_ROPE_DIM] = qr
        else:
            q_ref[hd, :, 0:QK_NOPE_DIM] = qh.astype(_BF16)
            q_ref[hd, :, QK_NOPE_DIM:QK_HEAD_DIM] = qr.astype(_BF16)

    u = _gelu(u)
    v = _gelu(v)
    row = lax.broadcasted_iota(jnp.int32, (CHUNK, CHUNK), 0)
    col = lax.broadcasted_iota(jnp.int32, (CHUNK, CHUNK), 1)
    for g in range(GMLP_GROUPS):
        gs = slice(g * GMLP_GROUP_DIM, (g + 1) * GMLP_GROUP_DIM)
        vg = v[:, gs]
        vg = vg * _rms(vg, GMLP_GROUP_DIM) * vg_ref[:, gs]
        if sample:
            vn_ref[:, gs] = vg
        vg_b = vg.astype(_BF16)
        w = jnp.where(col <= row, ws_ref[g], 0.0).astype(_BF16)
        for c in range(tm // CHUNK):
            cs = slice(c * CHUNK, (c + 1) * CHUNK)
            mix = _dot(w, vg_b[cs]) + bs_ref[g]
            gm_scr[cs, gs] = (u[cs, gs] * mix).astype(_BF16)
    gout_ref[...] = (gate_g * _dot(gm_scr[...], wog_ref[...])).astype(gout_ref.dtype)


def _const_spec(shape):
    nd = len(shape)
    return pl.BlockSpec(shape, lambda i: (0,) * nd, pipeline_mode=pl.Buffered(1))


def _proj(sample, x, w, cos, sin, n_pos_tiles):
    n = x.shape[0]
    tm = PROJ_TM
    tok = lambda width: pl.BlockSpec((tm, width), lambda i: (i, 0))
    heads = lambda width: pl.BlockSpec((N_HEADS, tm, width), lambda i: (0, i, 0))
    weights = [w["mix_norm"], w["w_in_t"], w["w_in_kr_t"], w["q_lora_norm"], w["w_uq"],
               w["kv_lora_norm"], w["w_uk"],
               w["w_uv_t"], w["q_nope_norm"], w["q_rope_norm"], w["q_rope_norm_s"], w["k_nope_norm"],
               w["k_rope_norm"], w["k_rope_norm_s"], w["gmlp_v_norm"], w["gmlp_w_s"], w["gmlp_b_s"],
               w["w_o_gmlp"]]
    pos_spec = pl.BlockSpec((tm, ROPE_SLOT), lambda i: (i % n_pos_tiles, 0))
    in_specs = [tok(D_MODEL)] + [_const_spec(a.shape) for a in weights] + [pos_spec, pos_spec]
    sds = jax.ShapeDtypeStruct
    if sample:
        out_shape = [sds((N_HEADS, n, KV_LORA_RANK + QK_ROPE_DIM), _F32), sds((n, KV_LORA_RANK), _F32),
                     sds((n, QK_ROPE_DIM), _F32), sds((n, GMLP_WIDTH), _F32),
                     sds((n, D_MODEL), _BF16), sds((n, D_MODEL), _BF16)]
        out_specs = [heads(KV_LORA_RANK + QK_ROPE_DIM), tok(KV_LORA_RANK), tok(QK_ROPE_DIM),
                     tok(GMLP_WIDTH), tok(D_MODEL), tok(D_MODEL)]
    else:
        out_shape = [sds((N_HEADS, n, QK_HEAD_DIM), _BF16), sds((N_HEADS, n, QK_HEAD_DIM), _BF16),
                     sds((N_HEADS, V_HEAD_DIM, n), _BF16), sds((n, KV_LORA_RANK), _F32),
                     sds((n // (n_pos_tiles * tm), QK_ROPE_DIM, n_pos_tiles * tm), _F32),
                     sds((n, D_MODEL), _BF16), sds((n, D_MODEL), _BF16)]
        vt_spec = pl.BlockSpec((N_HEADS, V_HEAD_DIM, tm), lambda i: (0, 0, i))
        krt_spec = pl.BlockSpec((1, QK_ROPE_DIM, tm), lambda i: (i // n_pos_tiles, 0, i % n_pos_tiles))
        out_specs = [heads(QK_HEAD_DIM), heads(QK_HEAD_DIM), vt_spec, tok(KV_LORA_RANK),
                     krt_spec, tok(D_MODEL), tok(D_MODEL)]
    return pl.pallas_call(
        functools.partial(_proj_kernel, sample),
        grid=(n // tm,),
        in_specs=in_specs,
        out_specs=out_specs,
        out_shape=out_shape,
        scratch_shapes=[pltpu.VMEM((tm, GMLP_WIDTH), _BF16)],
        compiler_params=pltpu.CompilerParams(
            dimension_semantics=("parallel",), vmem_limit_bytes=VMEM_LIMIT),
        name="proj_sample" if sample else "proj_prompt",
    )(x, *weights, cos, sin)


def _softmax_step(s, v_b, m_scr, l_scr, acc_scr):
    m_prev = m_scr[...]
    m_new = jnp.maximum(m_prev, jnp.max(s, axis=-1, keepdims=True))
    alpha = jnp.exp(m_prev - m_new)
    p = jnp.exp(s - m_new)
    l_scr[...] = alpha * l_scr[...] + jnp.sum(p, axis=-1, keepdims=True)
    acc_scr[...] = alpha * acc_scr[...] + _dot(p.astype(_BF16), v_b)
    m_scr[...] = m_new


def _pattn_kernel(q_ref, k_ref, vt_ref, o_ref, sa_scr, sb_scr, m_scr, acc_scr):
    qi = pl.program_id(2)
    m_scr[...] = jnp.full_like(m_scr, NEG_INF)
    acc_scr[...] = jnp.zeros_like(acc_scr)

    def scores_head(kb, buf, masked, hh):
        off = pl.multiple_of(kb * ATT_TK, ATT_TK)
        s = _dot_t(k_ref[hh, pl.ds(off, ATT_TK), :], q_ref[hh])
        if masked:
            key = lax.broadcasted_iota(jnp.int32, s.shape, 0)
            qry = lax.broadcasted_iota(jnp.int32, s.shape, 1)
            s = jnp.where(key <= qry, s, NEG_INF)
        buf[hh] = s

    def update_head(kb, buf, hh):
        off = pl.multiple_of(kb * ATT_TK, ATT_TK)
        s = buf[hh]
        m_prev = m_scr[hh]
        m_new = jnp.maximum(m_prev, jnp.max(s, axis=0, keepdims=True))
        p = jnp.exp2(s - m_new).astype(_BF16)
        vt = vt_ref[hh, :, pl.ds(off, ATT_TK)]
        vt1 = jnp.concatenate([vt, jnp.ones((ATT_ONES_ROWS, ATT_TK), _BF16)], axis=0)
        acc_scr[hh] = jnp.exp2(m_prev - m_new) * acc_scr[hh] + _dot(vt1, p)
        m_scr[hh] = m_new

    def scores(kb, buf, masked):
        for hh in range(ATT_HB):
            scores_head(kb, buf, masked, hh)

    def update(kb, buf):
        for hh in range(ATT_HB):
            update_head(kb, buf, hh)

    def trip(t, ybuf, xbuf, masked):
        for hh in range(ATT_HB):
            scores_head(t, xbuf, masked, hh)
            update_head(t - 1, ybuf, hh)

    @pl.when(qi == 0)
    def _():
        scores(0, sa_scr, True)
        update(0, sa_scr)

    @pl.when(qi > 0)
    def _():
        scores(0, sa_scr, False)

        def pair(j, carry):
            trip(2 * j + 1, sa_scr, sb_scr, False)
            trip(2 * j + 2, sb_scr, sa_scr, False)
            return carry

        lax.fori_loop(0, lax.shift_right_logical(qi - 1, 1), pair, 0)

        @pl.when((qi & 1) == 1)
        def _():
            trip(qi, sa_scr, sb_scr, True)
            update(qi, sb_scr)

        @pl.when((qi & 1) == 0)
        def _():
            trip(qi - 1, sa_scr, sb_scr, False)
            trip(qi, sb_scr, sa_scr, True)
            update(qi, sa_scr)

    for hh in range(ATT_HB):
        acc = acc_scr[hh]
        o_t = acc[:V_HEAD_DIM] / acc[V_HEAD_DIM:V_HEAD_DIM + 1]
        o_ref[:, hh * V_HEAD_DIM:(hh + 1) * V_HEAD_DIM] = o_t.T.astype(o_ref.dtype)


def _prompt_attention(q, k, vt, batch, seq):
    assert ATT_TQ == ATT_TK
    nq = seq // ATT_TQ
    hb = ATT_HB
    return pl.pallas_call(
        _pattn_kernel,
        grid=(batch, N_HEADS // hb, nq),
        in_specs=[
            pl.BlockSpec((hb, ATT_TQ, QK_HEAD_DIM), lambda b, h, i: (h, b * nq + i, 0)),
            pl.BlockSpec((hb, seq, QK_HEAD_DIM), lambda b, h, i: (h, b, 0)),
            pl.BlockSpec((hb, V_HEAD_DIM, seq), lambda b, h, i: (h, 0, b)),
        ],
        out_specs=pl.BlockSpec((ATT_TQ, hb * V_HEAD_DIM), lambda b, h, i: (b * nq + i, h)),
        out_shape=jax.ShapeDtypeStruct((batch * seq, N_HEADS * V_HEAD_DIM), _BF16),
        scratch_shapes=[pltpu.VMEM((hb, ATT_TK, ATT_TQ), _F32),
                        pltpu.VMEM((hb, ATT_TK, ATT_TQ), _F32),
                        pltpu.VMEM((hb, 1, ATT_TQ), _F32),
                        pltpu.VMEM((hb, V_HEAD_DIM + ATT_ONES_ROWS, ATT_TQ), _F32)],
        compiler_params=pltpu.CompilerParams(
            dimension_semantics=("parallel", "parallel", "arbitrary"), vmem_limit_bytes=VMEM_LIMIT),
        name="prompt_attention",
    )(q, k, vt)


def _sattn_kernel(n_pages, pt_ref, qcat_ref, cnew_ref, krnew_ref, wukt_ref, lat_hbm, ropet_hbm, o_ref,
                  a_scr, lat_buf, ropet_buf, sca_scr, scb_scr, scl_scr, cbl_scr, sem_lat, sem_rope,
                  m_scr, l_scr, acc_scr):
    pages = SATT_PAGES_PER_CHUNK
    n_chunks = n_pages // pages
    tk = pages * PAGE_SIZE
    nk = N_HEADS * QK_NOPE_DIM
    nrow = N_HEADS * 8
    s = pl.program_id(0)
    n_seq = pl.num_programs(0) - 1
    slot = lax.rem(s, 2)

    def copies(seq, dst_slot):
        out = []
        for g in range(n_pages):
            page = 0 if seq is None else pt_ref[seq * n_pages + g]
            dst = pl.ds(g * PAGE_SIZE, PAGE_SIZE)
            out.append(pltpu.make_async_copy(lat_hbm.at[page], lat_buf.at[dst_slot, dst], sem_lat.at[dst_slot]))
            out.append(pltpu.make_async_copy(ropet_hbm.at[page], ropet_buf.at[dst_slot, :, dst],
                                             sem_rope.at[dst_slot]))
        return out

    @pl.when(s == 0)
    def _():
        a_scr[0:nk, :] = wukt_ref[...]
        scl_scr[...] = jnp.zeros_like(scl_scr)
        cbl_scr[...] = jnp.zeros_like(cbl_scr)
        m_scr[...] = jnp.zeros_like(m_scr)
        l_scr[...] = jnp.ones_like(l_scr)
        acc_scr[...] = jnp.zeros_like(acc_scr)
        for d in copies(0, 0):
            d.start()

    @pl.when(s < n_seq)
    def _():
        for d in copies(jnp.minimum(s + 1, n_seq - 1), 1 - slot):
            d.start()

    for d in copies(None, slot):
        d.wait()

    q2 = qcat_ref[...].reshape(nrow, KV_LORA_RANK + QK_ROPE_DIM)
    a_scr[nk:nk + nrow, :] = q2[:, :KV_LORA_RANK].astype(_BF16)
    q_rope = q2[:, KV_LORA_RANK:].astype(_BF16)

    def scores(c_b, s_rope):
        big = _dot_t(a_scr[...], c_b)
        rows = []
        for hd in range(N_HEADS):
            kx = big[hd * QK_NOPE_DIM:(hd + 1) * QK_NOPE_DIM, :]
            r = lax.rsqrt(jnp.sum(kx * kx, axis=0, keepdims=True) * (1.0 / QK_NOPE_DIM) + EPS)
            rows.append(big[nk + 8 * hd:nk + 8 * hd + 8, :] * r + s_rope[8 * hd:8 * hd + 8, :])
        return jnp.concatenate(rows, axis=0)

    sc_bufs = [sca_scr, scb_scr] * (n_chunks // 2 + 1)
    sc_bufs = sc_bufs[:n_chunks - 1] + [scl_scr]
    pad = PAGE_SIZE - cnew_ref.shape[0]

    def latent(c):
        return lat_buf[slot, pl.ds(c * tk, tk), :].astype(_BF16)

    def stage_scores(c):
        s_rope = _dot(q_rope, ropet_buf[slot, :, pl.ds(c * tk, tk)].astype(_BF16))
        if c < n_chunks - 1:
            sc_bufs[c][...] = scores(latent(c), s_rope)
        else:
            c_new = jnp.concatenate([cnew_ref[...], jnp.zeros((pad, KV_LORA_RANK), _F32)], axis=0)
            c_b = jnp.concatenate([latent(c), c_new.astype(_BF16)], axis=0)
            cbl_scr[...] = c_b
            kr_new = jnp.concatenate([krnew_ref[...], jnp.zeros((pad, QK_ROPE_DIM), _F32)], axis=0)
            s_rope = jnp.concatenate([s_rope, _dot_t(q_rope, kr_new.astype(_BF16))], axis=1)
            row = lax.broadcasted_iota(jnp.int32, (nrow, tk + PAGE_SIZE), 0)
            col = lax.broadcasted_iota(jnp.int32, (nrow, tk + PAGE_SIZE), 1)
            sc_bufs[c][...] = jnp.where(col - tk <= (row & 7), scores(c_b, s_rope), NEG_INF)

    def stage_update(c):
        _softmax_step(sc_bufs[c][...], latent(c), m_scr, l_scr, acc_scr)

    stage_scores(0)

    _softmax_step(scl_scr[...], cbl_scr[...], m_scr, l_scr, acc_scr)
    o_ref[...] = (acc_scr[...] / l_scr[...]).reshape(o_ref.shape)
    m_scr[...] = jnp.full_like(m_scr, NEG_INF)
    l_scr[...] = jnp.zeros_like(l_scr)
    acc_scr[...] = jnp.zeros_like(acc_scr)

    for c in range(1, n_chunks):
        stage_scores(c)
        stage_update(c - 1)


def _sample_attention(page_table, qcat, c_new, kr_new, wukt, cache_lat, cache_rope_t, t_new):
    n_seq, n_pages = page_table.shape
    assert t_new == 8 and n_pages % SATT_PAGES_PER_CHUNK == 0 and n_pages // SATT_PAGES_PER_CHUNK >= 2
    tk = SATT_PAGES_PER_CHUNK * PAGE_SIZE
    past = n_pages * PAGE_SIZE
    nrow = N_HEADS * t_new
    dq = KV_LORA_RANK + QK_ROPE_DIM
    cur = lambda s: jnp.minimum(s, n_seq - 1)
    prev = lambda s: jnp.maximum(s - 1, 0)
    grid_spec = pltpu.PrefetchScalarGridSpec(
        num_scalar_prefetch=1,
        grid=(n_seq + 1,),
        in_specs=[
            pl.BlockSpec((N_HEADS, t_new, dq), lambda s, pt: (0, cur(s), 0)),
            pl.BlockSpec((t_new, KV_LORA_RANK), lambda s, pt: (cur(s), 0)),
            pl.BlockSpec((t_new, QK_ROPE_DIM), lambda s, pt: (cur(s), 0)),
            pl.BlockSpec((N_HEADS * QK_NOPE_DIM, KV_LORA_RANK), lambda s, pt: (0, 0)),
            pl.BlockSpec(memory_space=pl.ANY),
            pl.BlockSpec(memory_space=pl.ANY),
        ],
        out_specs=pl.BlockSpec((N_HEADS, t_new, KV_LORA_RANK), lambda s, pt: (0, prev(s), 0)),
        scratch_shapes=[
            pltpu.VMEM((N_HEADS * QK_NOPE_DIM + nrow, KV_LORA_RANK), _BF16),
            pltpu.VMEM((2, past, KV_LORA_RANK), _F32),
            pltpu.VMEM((2, QK_ROPE_DIM, past), _F32),
            pltpu.VMEM((nrow, tk), _F32),
            pltpu.VMEM((nrow, tk), _F32),
            pltpu.VMEM((nrow, tk + PAGE_SIZE), _F32),
            pltpu.VMEM((tk + PAGE_SIZE, KV_LORA_RANK), _BF16),
            pltpu.SemaphoreType.DMA((2,)),
            pltpu.SemaphoreType.DMA((2,)),
            pltpu.VMEM((nrow, 1), _F32),
            pltpu.VMEM((nrow, 1), _F32),
            pltpu.VMEM((nrow, KV_LORA_RANK), _F32),
        ],
    )
    return pl.pallas_call(
        functools.partial(_sattn_kernel, n_pages),
        grid_spec=grid_spec,
        out_shape=jax.ShapeDtypeStruct((N_HEADS, n_seq * t_new, KV_LORA_RANK), _F32),
        compiler_params=pltpu.CompilerParams(
            dimension_semantics=("arbitrary",), vmem_limit_bytes=VMEM_LIMIT),
        name="sample_attention",
    )(page_table.reshape(-1), qcat, c_new, kr_new, wukt, cache_lat, cache_rope_t)


def _merge_kernel(from_latent, a_ref, gate_ref, gout_ref, x_ref, wuv_ref, woa_ref, wout_ref, o_ref):
    if from_latent:
        heads = [_dot(a_ref[hd].astype(_BF16), wuv_ref[:, hd * V_HEAD_DIM:(hd + 1) * V_HEAD_DIM])
                 for hd in range(N_HEADS)]
        attn = jnp.concatenate(heads, axis=-1).astype(_BF16)
    else:
        attn = a_ref[...]
    merged = gate_ref[...].astype(_F32) * _dot(attn, woa_ref[...]) + gout_ref[...]
    o_ref[...] = x_ref[...] + _dot(merged.astype(_BF16), wout_ref[...])


def _merge(from_latent, attn, gate, gout, x, wuv, woa, wout):
    n = x.shape[0]
    tm = min(MERGE_TM, n // 2)
    tok = pl.BlockSpec((tm, D_MODEL), lambda i: (i, 0))
    if from_latent:
        a_spec = pl.BlockSpec((N_HEADS, tm, KV_LORA_RANK), lambda i: (0, i, 0))
    else:
        a_spec = tok
    return pl.pallas_call(
        functools.partial(_merge_kernel, from_latent),
        grid=(n // tm,),
        in_specs=[a_spec, tok, tok, tok, _const_spec(wuv.shape), _const_spec(woa.shape),
                  _const_spec(wout.shape)],
        out_specs=tok,
        out_shape=jax.ShapeDtypeStruct((n, D_MODEL), _F32),
        compiler_params=pltpu.CompilerParams(
            dimension_semantics=("parallel",), vmem_limit_bytes=VMEM_LIMIT),
        name="merge_sample" if from_latent else "merge_prompt",
    )(attn, gate, gout, x, wuv, woa, wout)


def _swap_halves(a, axis):
    lo, hi = jnp.split(a, 2, axis=axis)
    return jnp.concatenate([hi, lo], axis=axis)


def _slot(a):
    return jnp.concatenate([a, jnp.zeros_like(a)], axis=-1)


def _rope_tables(pos):
    half = QK_ROPE_DIM // 2
    inv = ROPE_BASE ** (-np.arange(half, dtype=np.float64) / half)
    ang = np.asarray(pos, np.float64)[:, None] * inv[None, :]
    cos, sin = np.cos(ang), np.sin(ang)
    zero = np.zeros((ang.shape[0], QK_ROPE_DIM))
    return (jnp.asarray(np.concatenate([cos, cos, zero], axis=-1), _F32),
            jnp.asarray(np.concatenate([-sin, sin, zero], axis=-1), _F32))


def _prep_weights(w_in, w_uq, w_uk, w_uv, w_o_gmlp, mix_norm, q_lora_norm, kv_lora_norm, q_nope_norm,
                  q_rope_norm, k_nope_norm, k_rope_norm, gmlp_v_norm):
    w_uq = w_uq.astype(_BF16)
    off_kr = Q_LORA_RANK + KV_LORA_RANK
    w_in_t = w_in.T.astype(_BF16)
    k_r_t = w_in_t[off_kr:off_kr + QK_ROPE_DIM]
    zero = jnp.zeros_like(k_r_t)
    w_in_kr_t = jnp.concatenate([k_r_t, zero, _swap_halves(k_r_t, 0), zero], axis=0)
    assert w_in_kr_t.shape[0] == 2 * ROPE_SLOT and _D_IN_PADDED == _OFF_KR + 2 * ROPE_SLOT
    wq = w_uq.reshape(Q_LORA_RANK, N_HEADS, QK_HEAD_DIM)
    wq_nope = wq[:, :, :QK_NOPE_DIM].reshape(Q_LORA_RANK, -1)
    wq_rope = wq[:, :, QK_NOPE_DIM:]
    w_uq_p = jnp.concatenate([wq_nope, _slot(wq_rope).reshape(Q_LORA_RANK, -1),
                              _slot(_swap_halves(wq_rope, -1)).reshape(Q_LORA_RANK, -1)], axis=-1)
    row = lambda a: a.reshape(1, -1).astype(_F32)
    return dict(
        mix_norm=row(mix_norm), w_in_t=w_in_t, w_in_kr_t=w_in_kr_t, q_lora_norm=row(q_lora_norm),
        w_uq=w_uq_p.astype(_BF16), kv_lora_norm=row(kv_lora_norm), w_uk=w_uk.astype(_BF16),
        w_uv=w_uv.astype(_BF16), w_uv_t=w_uv.T.astype(_BF16), q_nope_norm=row(q_nope_norm), q_rope_norm=row(_slot(q_rope_norm)),
        q_rope_norm_s=row(_slot(_swap_halves(q_rope_norm, -1))), k_nope_norm=row(k_nope_norm),
        k_rope_norm=row(_slot(k_rope_norm)), k_rope_norm_s=row(_slot(_swap_halves(k_rope_norm, -1))),
        gmlp_v_norm=row(gmlp_v_norm), w_o_gmlp=w_o_gmlp.astype(_BF16))


def kernel(x_prompt, x_sample, cache_kv_latent, cache_k_rope, page_table, ffn1_norm, ffn1_w_gate, ffn1_w_up, ffn1_w_down, mix_norm, w_in, q_lora_norm, w_uq, kv_lora_norm, w_uk, w_uv, q_nope_norm, q_rope_norm, k_nope_norm, k_rope_norm, gmlp_v_norm, gmlp_w_s, gmlp_b_s, w_o_attn, w_o_gmlp, w_out, ffn2_norm, ffn2_w_gate, ffn2_w_up, ffn2_w_down):
    batch, seq, _ = x_prompt.shape
    n_seq, t_new, _ = x_sample.shape
    assert ffn1_norm.shape[0] == 1
    past = page_table.shape[1] * PAGE_SIZE
    n_pool = cache_kv_latent.shape[1]
    l = 0
    row = lambda a: a.reshape(1, -1).astype(_F32)
    w = _prep_weights(w_in[l], w_uq[l], w_uk[l], w_uv[l], w_o_gmlp[l], mix_norm[l], q_lora_norm[l],
                      kv_lora_norm[l], q_nope_norm[l], q_rope_norm[l], k_nope_norm[l], k_rope_norm[l],
                      gmlp_v_norm[l])
    ffn1 = (row(ffn1_norm[l]), ffn1_w_gate.reshape(D_MODEL, D_FF), ffn1_w_up.reshape(D_MODEL, D_FF),
            ffn1_w_down.reshape(D_FF, D_MODEL))
    ffn2 = (row(ffn2_norm[l]), ffn2_w_gate.reshape(D_MODEL, D_FF), ffn2_w_up.reshape(D_MODEL, D_FF),
            ffn2_w_down.reshape(D_FF, D_MODEL))
    woa = w_o_attn[l].astype(_BF16)
    wout = w_out[l].astype(_BF16)

    wp = dict(w, gmlp_w_s=gmlp_w_s[l].astype(_F32), gmlp_b_s=gmlp_b_s[l].reshape(GMLP_GROUPS, CHUNK, 1))
    cos_p, sin_p = _rope_tables(np.arange(seq))
    xp = _ffn(x_prompt.reshape(batch * seq, D_MODEL), *ffn1)
    q, k, v, ckv_p, kr_p, gate_p, gout_p = _proj(False, xp, wp, cos_p, sin_p, seq // PROJ_TM)
    attn_p = _prompt_attention(q, k, v, batch, seq)
    xp = _merge(False, attn_p, gate_p, gout_p, xp, w["w_uv"], woa, wout)
    xp = _ffn(xp, *ffn2)

    reps = CHUNK // t_new
    ws_s = jax.vmap(lambda m: jnp.kron(jnp.eye(reps, dtype=_F32), m))(gmlp_w_s[l][:, :t_new, :t_new])
    bs_s = jnp.tile(gmlp_b_s[l][:, :t_new], (1, reps)).reshape(GMLP_GROUPS, CHUNK, 1)
    ws = dict(w, gmlp_w_s=ws_s, gmlp_b_s=bs_s)
    cos_s, sin_s = _rope_tables(past + np.arange(PROJ_TM) % t_new)
    xs = _ffn(x_sample.reshape(n_seq * t_new, D_MODEL), *ffn1)
    qcat, ckv_s, kr_s, vn_s, gate_s, gout_s = _proj(True, xs, ws, cos_s, sin_s, 1)
    o_lat = _sample_attention(page_table, qcat, ckv_s, kr_s, w["w_uk"].T,
                              cache_kv_latent.reshape(n_pool, PAGE_SIZE, KV_LORA_RANK),
                              jnp.swapaxes(cache_k_rope.reshape(n_pool, PAGE_SIZE, QK_ROPE_DIM), 1, 2),
                              t_new)
    xs = _merge(True, o_lat, gate_s, gout_s, xs, w["w_uv"], woa, wout)
    xs = _ffn(xs, *ffn2)

    return (xp.reshape(batch, seq, D_MODEL), xs.reshape(n_seq, t_new, D_MODEL),
            ckv_p.reshape(1, batch, seq, KV_LORA_RANK), jnp.swapaxes(kr_p, 1, 2)[None],
            ckv_s.reshape(1, n_seq, t_new, KV_LORA_RANK), kr_s.reshape(1, n_seq, t_new, QK_ROPE_DIM),
            vn_s.reshape(1, n_seq, t_new, GMLP_WIDTH))
```

```python
import functools

import jax
import jax.numpy as jnp
import numpy as np
from jax import lax
from jax.experimental import pallas as pl
from jax.experimental.pallas import tpu as pltpu

D_MODEL = 1024
N_HEADS = 8
QK_NOPE_DIM = 128
QK_ROPE_DIM = 64
QK_HEAD_DIM = QK_NOPE_DIM + QK_ROPE_DIM
V_HEAD_DIM = 128
Q_LORA_RANK = 256
KV_LORA_RANK = 256
ROPE_BASE = 10000.0
ATTN_SCALE = QK_HEAD_DIM ** -0.5
LOG2E = 1.4426950408889634
GMLP_GROUPS = 4
GMLP_WIDTH = 1024
GMLP_GROUP_DIM = GMLP_WIDTH // GMLP_GROUPS
CHUNK = 128
PAGE_SIZE = 128
D_FF = 2816
EPS = 1e-6
NEG_INF = -1e30

LANE = 128
ROPE_SLOT = LANE

_OFF_CQ = 0
_OFF_CKV = _OFF_CQ + Q_LORA_RANK
_OFF_U = _OFF_CKV + KV_LORA_RANK
_OFF_V = _OFF_U + GMLP_WIDTH
_OFF_GA = _OFF_V + GMLP_WIDTH
_OFF_GG = _OFF_GA + D_MODEL
_OFF_KR = _OFF_GG + D_MODEL
_OFF_KRS = _OFF_KR + ROPE_SLOT
_D_IN_PADDED = _OFF_KRS + ROPE_SLOT

FFN_TM = 1024
FFN_TF = 256
PROJ_TM = 512
ATT_TQ = 512
ATT_TK = 512
ATT_HB = 4
ATT_ONES_ROWS = 16
MERGE_TM = 1024
SATT_PAGES_PER_CHUNK = 16
VMEM_LIMIT = 56 * 1024 * 1024

_BF16 = jnp.bfloat16
_F32 = jnp.float32


def _dot(a, b):
    return jnp.dot(a, b, preferred_element_type=_F32)


def _dot_t(a, b):
    return lax.dot_general(a, b, (((1,), (1,)), ((), ())), preferred_element_type=_F32)


def _rms(x, n):
    return lax.rsqrt(jnp.sum(x * x, axis=-1, keepdims=True) * (1.0 / n) + EPS)


def _ffn_kernel(x_ref, g_ref, wg_hbm, wu_hbm, wd_hbm, o_ref, wg_scr, wu_scr, wd_scr, stage_g, stage_u,
                stage_d, sem, act_scr):
    n_blk = D_FF // FFN_TF

    def block_copies(j, slot):
        cols = pl.ds(j * FFN_TF, FFN_TF)
        return [pltpu.make_async_copy(wg_hbm.at[:, cols], stage_g.at[slot], sem.at[slot]),
                pltpu.make_async_copy(wu_hbm.at[:, cols], stage_u.at[slot], sem.at[slot]),
                pltpu.make_async_copy(wd_hbm.at[cols, :], stage_d.at[slot], sem.at[slot])]

    def fetch_block(j):
        slot = j % 2
        if j + 1 < n_blk:
            for d in block_copies(j + 1, 1 - slot):
                d.start()
        for d in block_copies(j, slot):
            d.wait()
        sl = slice(j * FFN_TF, (j + 1) * FFN_TF)
        wg_scr[:, sl] = stage_g[slot].astype(_BF16)
        wu_scr[:, sl] = stage_u[slot].astype(_BF16)
        wd_scr[sl, :] = stage_d[slot].astype(_BF16)

    def tile(first_step):
        x = x_ref[...]
        h = (x * _rms(x, D_MODEL) * g_ref[...]).astype(_BF16)
        if first_step:
            for d in block_copies(0, 0):
                d.start()
        for j in range(n_blk):
            if first_step:
                fetch_block(j)
            sl = slice(j * FFN_TF, (j + 1) * FFN_TF)
            gate = _dot(h, wg_scr[:, sl])
            up = _dot(h, wu_scr[:, sl])
            act_scr[:, sl] = (gate * jax.nn.sigmoid(gate) * up).astype(_BF16)
        o_ref[...] = x + 0.5 * _dot(act_scr[...], wd_scr[...])

    pl.when(pl.program_id(0) == 0)(functools.partial(tile, True))
    pl.when(pl.program_id(0) > 0)(functools.partial(tile, False))


def _ffn(x, norm_g, wg, wu, wd):
    n = x.shape[0]
    tm = min(FFN_TM, n)
    any_spec = pl.BlockSpec(memory_space=pl.ANY)
    return pl.pallas_call(
        _ffn_kernel,
        grid=(n // tm,),
        in_specs=[pl.BlockSpec((tm, D_MODEL), lambda i: (i, 0)), _const_spec(norm_g.shape),
                  any_spec, any_spec, any_spec],
        out_specs=pl.BlockSpec((tm, D_MODEL), lambda i: (i, 0)),
        out_shape=jax.ShapeDtypeStruct((n, D_MODEL), _F32),
        scratch_shapes=[pltpu.VMEM((D_MODEL, D_FF), _BF16), pltpu.VMEM((D_MODEL, D_FF), _BF16),
                        pltpu.VMEM((D_FF, D_MODEL), _BF16),
                        pltpu.VMEM((2, D_MODEL, FFN_TF), _F32), pltpu.VMEM((2, D_MODEL, FFN_TF), _F32),
                        pltpu.VMEM((2, FFN_TF, D_MODEL), _F32),
                        pltpu.SemaphoreType.DMA((2,)),
                        pltpu.VMEM((tm, D_FF), _BF16)],
        compiler_params=pltpu.CompilerParams(
            dimension_semantics=("arbitrary",), vmem_limit_bytes=VMEM_LIMIT),
        name="ffn",
    )(x, norm_g, wg, wu, wd)


def _rope_slot(x, xs, g, gs, cos, sin):
    r = _rms(x, QK_ROPE_DIM)
    return (x * r * g) * cos + (xs * r * gs) * sin


def _gelu(x):
    return 0.5 * x * (1.0 + lax.erf(x * (2.0 ** -0.5)))


def _proj_kernel(sample, x_ref, mixg_ref, wt_ref, wkr_ref, qlg_ref, wuq_ref, kvg_ref, wuk_ref, wuv_ref,
                 qng_ref, qrg_ref, qrgs_ref, kng_ref, krg_ref, krgs_ref, vg_ref, ws_ref, bs_ref,
                 wog_ref, cos_ref, sin_ref, *rest):
    if sample:
        qcat_ref, ckv_ref, kr_ref, vn_ref, gate_ref, gout_ref, gm_scr = rest
    else:
        q_ref, k_ref, v_ref, ckv_ref, kr_ref, gate_ref, gout_ref, gm_scr = rest
    tm = x_ref.shape[0]
    x = x_ref[...]
    h = (x * _rms(x, D_MODEL) * mixg_ref[...]).astype(_BF16)
    cos = cos_ref[...]
    sin = sin_ref[...]

    def win(off, width):
        if off >= _OFF_KR:
            assert (off, width) == (_OFF_KR, 2 * ROPE_SLOT)
            return _dot_t(h, wkr_ref[...])
        row = off if off < _OFF_U else off + QK_ROPE_DIM
        return _dot_t(h, wt_ref[row:row + width, :])

    kr_raw = win(_OFF_KR, 2 * ROPE_SLOT)
    ckv = win(_OFF_CKV, KV_LORA_RANK)
    cq = win(_OFF_CQ, Q_LORA_RANK)
    u = win(_OFF_U, GMLP_WIDTH)
    v = win(_OFF_V, GMLP_WIDTH)

    kr = _rope_slot(kr_raw[:, :ROPE_SLOT], kr_raw[:, ROPE_SLOT:], krg_ref[...], krgs_ref[...],
                    cos, sin)[:, :QK_ROPE_DIM]
    if sample:
        kr_ref[...] = kr
    else:
        kr_ref[0] = kr.T
    ckv = ckv * _rms(ckv, KV_LORA_RANK) * kvg_ref[...]
    ckv_ref[...] = ckv
    ckv_b = ckv.astype(_BF16)
    cq_b = (cq * _rms(cq, Q_LORA_RANK) * qlg_ref[...]).astype(_BF16)

    nq = N_HEADS * QK_NOPE_DIM
    nr = N_HEADS * ROPE_SLOT
    if not sample:
        kexp = _dot(ckv_b, wuk_ref[...])
        vals_t = _dot_t(wuv_ref[...], ckv_b)
    q_nope = _dot(cq_b, wuq_ref[:, 0:nq])
    q_rope = _dot(cq_b, wuq_ref[:, nq:nq + nr])
    q_rope_s = _dot(cq_b, wuq_ref[:, nq + nr:nq + 2 * nr])
    gate_g = jax.nn.sigmoid(win(_OFF_GG, D_MODEL))
    gate_ref[...] = jax.nn.sigmoid(win(_OFF_GA, D_MODEL)).astype(gate_ref.dtype)

    if not sample:
        kr_b = kr.astype(_BF16)
        for hd in range(N_HEADS):
            sl = slice(hd * QK_NOPE_DIM, (hd + 1) * QK_NOPE_DIM)
            kh = kexp[:, sl]
            k_ref[hd, :, 0:QK_NOPE_DIM] = (kh * _rms(kh, QK_NOPE_DIM) * kng_ref[...]).astype(_BF16)
            k_ref[hd, :, QK_NOPE_DIM:QK_HEAD_DIM] = kr_b
            v_ref[hd] = vals_t[hd * V_HEAD_DIM:(hd + 1) * V_HEAD_DIM, :].astype(_BF16)

    q_scale = ATTN_SCALE if sample else ATTN_SCALE * LOG2E
    for hd in range(N_HEADS):
        sl = slice(hd * QK_NOPE_DIM, (hd + 1) * QK_NOPE_DIM)
        qh = q_nope[:, sl]
        qh = qh * _rms(qh, QK_NOPE_DIM) * qng_ref[...] * q_scale
        rs = slice(hd * ROPE_SLOT, (hd + 1) * ROPE_SLOT)
        qr = _rope_slot(q_rope[:, rs], q_rope_s[:, rs], qrg_ref[...], qrgs_ref[...], cos, sin)
        qr = qr[:, :QK_ROPE_DIM] * q_scale
        if sample:
            qa = _dot_t((qh * kng_ref[...]).astype(_BF16), wuk_ref[:, sl])
            qcat_ref[hd, :, 0:KV_LORA_RANK] = qa
            qcat_ref[hd, :, KV_LORA_RANK:KV_LORA_RANK + QK_ROPE_DIM] = qr
        else:
            q_ref[hd, :, 0:QK_NOPE_DIM] = qh.astype(_BF16)
            q_ref[hd, :, QK_NOPE_DIM:QK_HEAD_DIM] = qr.astype(_BF16)

    u = _gelu(u)
    v = _gelu(v)
    row = lax.broadcasted_iota(jnp.int32, (CHUNK, CHUNK), 0)
    col = lax.broadcasted_iota(jnp.int32, (CHUNK, CHUNK), 1)
    for g in range(GMLP_GROUPS):
        gs = slice(g * GMLP_GROUP_DIM, (g + 1) * GMLP_GROUP_DIM)
        vg = v[:, gs]
        vg = vg * _rms(vg, GMLP_GROUP_DIM) * vg_ref[:, gs]
        if sample:
            vn_ref[:, gs] = vg
        vg_b = vg.astype(_BF16)
        w = jnp.where(col <= row, ws_ref[g], 0.0).astype(_BF16)
        for c in range(tm // CHUNK):
            cs = slice(c * CHUNK, (c + 1) * CHUNK)
            mix = _dot(w, vg_b[cs]) + bs_ref[g]
            gm_scr[cs, gs] = (u[cs, gs] * mix).astype(_BF16)
    gout_ref[...] = (gate_g * _dot(gm_scr[...], wog_ref[...])).astype(gout_ref.dtype)


def _const_spec(shape):
    nd = len(shape)
    return pl.BlockSpec(shape, lambda i: (0,) * nd, pipeline_mode=pl.Buffered(1))


def _proj(sample, x, w, cos, sin, n_pos_tiles):
    n = x.shape[0]
    tm = PROJ_TM
    tok = lambda width: pl.BlockSpec((tm, width), lambda i: (i, 0))
    heads = lambda width: pl.BlockSpec((N_HEADS, tm, width), lambda i: (0, i, 0))
    weights = [w["mix_norm"], w["w_in_t"], w["w_in_kr_t"], w["q_lora_norm"], w["w_uq"],
               w["kv_lora_norm"], w["w_uk"],
               w["w_uv_t"], w["q_nope_norm"], w["q_rope_norm"], w["q_rope_norm_s"], w["k_nope_norm"],
               w["k_rope_norm"], w["k_rope_norm_s"], w["gmlp_v_norm"], w["gmlp_w_s"], w["gmlp_b_s"],
               w["w_o_gmlp"]]
    pos_spec = pl.BlockSpec((tm, ROPE_SLOT), lambda i: (i % n_pos_tiles, 0))
    in_specs = [tok(D_MODEL)] + [_const_spec(a.shape) for a in weights] + [pos_spec, pos_spec]
    sds = jax.ShapeDtypeStruct
    if sample:
        out_shape = [sds((N_HEADS, n, KV_LORA_RANK + QK_ROPE_DIM), _F32), sds((n, KV_LORA_RANK), _F32),
                     sds((n, QK_ROPE_DIM), _F32), sds((n, GMLP_WIDTH), _F32),
                     sds((n, D_MODEL), _BF16), sds((n, D_MODEL), _BF16)]
        out_specs = [heads(KV_LORA_RANK + QK_ROPE_DIM), tok(KV_LORA_RANK), tok(QK_ROPE_DIM),
                     tok(GMLP_WIDTH), tok(D_MODEL), tok(D_MODEL)]
    else:
        out_shape = [sds((N_HEADS, n, QK_HEAD_DIM), _BF16), sds((N_HEADS, n, QK_HEAD_DIM), _BF16),
                     sds((N_HEADS, V_HEAD_DIM, n), _BF16), sds((n, KV_LORA_RANK), _F32),
                     sds((n // (n_pos_tiles * tm), QK_ROPE_DIM, n_pos_tiles * tm), _F32),
                     sds((n, D_MODEL), _BF16), sds((n, D_MODEL), _BF16)]
        vt_spec = pl.BlockSpec((N_HEADS, V_HEAD_DIM, tm), lambda i: (0, 0, i))
        krt_spec = pl.BlockSpec((1, QK_ROPE_DIM, tm), lambda i: (i // n_pos_tiles, 0, i % n_pos_tiles))
        out_specs = [heads(QK_HEAD_DIM), heads(QK_HEAD_DIM), vt_spec, tok(KV_LORA_RANK),
                     krt_spec, tok(D_MODEL), tok(D_MODEL)]
    return pl.pallas_call(
        functools.partial(_proj_kernel, sample),
        grid=(n // tm,),
        in_specs=in_specs,
        out_specs=out_specs,
        out_shape=out_shape,
        scratch_shapes=[pltpu.VMEM((tm, GMLP_WIDTH), _BF16)],
        compiler_params=pltpu.CompilerParams(
            dimension_semantics=("parallel",), vmem_limit_bytes=VMEM_LIMIT),
        name="proj_sample" if sample else "proj_prompt",
    )(x, *weights, cos, sin)


def _softmax_step(s, v_b, m_scr, l_scr, acc_scr):
    m_prev = m_scr[...]
    m_new = jnp.maximum(m_prev, jnp.max(s, axis=-1, keepdims=True))
    alpha = jnp.exp(m_prev - m_new)
    p = jnp.exp(s - m_new)
    l_scr[...] = alpha * l_scr[...] + jnp.sum(p, axis=-1, keepdims=True)
    acc_scr[...] = alpha * acc_scr[...] + _dot(p.astype(_BF16), v_b)
    m_scr[...] = m_new


def _pattn_kernel(q_ref, k_ref, vt_ref, o_ref, sa_scr, sb_scr, m_scr, acc_scr):
    qi = pl.program_id(2)
    m_scr[...] = jnp.full_like(m_scr, NEG_INF)
    acc_scr[...] = jnp.zeros_like(acc_scr)

    def scores_head(kb, buf, masked, hh):
        off = pl.multiple_of(kb * ATT_TK, ATT_TK)
        s = _dot_t(k_ref[hh, pl.ds(off, ATT_TK), :], q_ref[hh])
        if masked:
            key = lax.broadcasted_iota(jnp.int32, s.shape, 0)
            qry = lax.broadcasted_iota(jnp.int32, s.shape, 1)
            s = jnp.where(key <= qry, s, NEG_INF)
        buf[hh] = s

    def update_head(kb, buf, hh):
        off = pl.multiple_of(kb * ATT_TK, ATT_TK)
        s = buf[hh]
        m_prev = m_scr[hh]
        m_new = jnp.maximum(m_prev, jnp.max(s, axis=0, keepdims=True))
        p = jnp.exp2((s - m_new).astype(_BF16))
        vt = vt_ref[hh, :, pl.ds(off, ATT_TK)]
        vt1 = jnp.concatenate([vt, jnp.ones((ATT_ONES_ROWS, ATT_TK), _BF16)], axis=0)
        acc_scr[hh] = jnp.exp2(m_prev - m_new) * acc_scr[hh] + _dot(vt1, p)
        m_scr[hh] = m_new

    def scores(kb, buf, masked):
        for hh in range(ATT_HB):
            scores_head(kb, buf, masked, hh)

    def update(kb, buf):
        for hh in range(ATT_HB):
            update_head(kb, buf, hh)

    def trip(t, ybuf, xbuf, masked):
        for hh in range(ATT_HB):
            scores_head(t, xbuf, masked, hh)
            update_head(t - 1, ybuf, hh)

    @pl.when(qi == 0)
    def _():
        scores(0, sa_scr, True)
        update(0, sa_scr)

    @pl.when(qi > 0)
    def _():
        scores(0, sa_scr, False)

        def pair(j, carry):
            trip(2 * j + 1, sa_scr, sb_scr, False)
            trip(2 * j + 2, sb_scr, sa_scr, False)
            return carry

        lax.fori_loop(0, lax.shift_right_logical(qi - 1, 1), pair, 0)

        @pl.when((qi & 1) == 1)
        def _():
            trip(qi, sa_scr, sb_scr, True)
            update(qi, sb_scr)

        @pl.when((qi & 1) == 0)
        def _():
            trip(qi - 1, sa_scr, sb_scr, False)
            trip(qi, sb_scr, sa_scr, True)
            update(qi, sa_scr)

    for hh in range(ATT_HB):
        acc = acc_scr[hh]
        o_t = acc[:V_HEAD_DIM] / acc[V_HEAD_DIM:V_HEAD_DIM + 1]
        o_ref[:, hh * V_HEAD_DIM:(hh + 1) * V_HEAD_DIM] = o_t.T.astype(o_ref.dtype)


def _prompt_attention(q, k, vt, batch, seq):
    assert ATT_TQ == ATT_TK
    nq = seq // ATT_TQ
    hb = ATT_HB
    return pl.pallas_call(
        _pattn_kernel,
        grid=(batch, N_HEADS // hb, nq),
        in_specs=[
            pl.BlockSpec((hb, ATT_TQ, QK_HEAD_DIM), lambda b, h, i: (h, b * nq + i, 0)),
            pl.BlockSpec((hb, seq, QK_HEAD_DIM), lambda b, h, i: (h, b, 0)),
            pl.BlockSpec((hb, V_HEAD_DIM, seq), lambda b, h, i: (h, 0, b)),
        ],
        out_specs=pl.BlockSpec((ATT_TQ, hb * V_HEAD_DIM), lambda b, h, i: (b * nq + i, h)),
        out_shape=jax.ShapeDtypeStruct((batch * seq, N_HEADS * V_HEAD_DIM), _BF16),
        scratch_shapes=[pltpu.VMEM((hb, ATT_TK, ATT_TQ), _F32),
                        pltpu.VMEM((hb, ATT_TK, ATT_TQ), _F32),
                        pltpu.VMEM((hb, 1, ATT_TQ), _F32),
                        pltpu.VMEM((hb, V_HEAD_DIM + ATT_ONES_ROWS, ATT_TQ), _F32)],
        compiler_params=pltpu.CompilerParams(
            dimension_semantics=("parallel", "parallel", "arbitrary"), vmem_limit_bytes=VMEM_LIMIT),
        name="prompt_attention",
    )(q, k, vt)


def _sattn_kernel(n_pages, pt_ref, qcat_ref, cnew_ref, krnew_ref, wukt_ref, lat_hbm, ropet_hbm, o_ref,
                  a_scr, lat_buf, ropet_buf, sca_scr, scb_scr, scl_scr, cbl_scr, sem_lat, sem_rope,
                  m_scr, l_scr, acc_scr):
    pages = SATT_PAGES_PER_CHUNK
    n_chunks = n_pages // pages
    tk = pages * PAGE_SIZE
    nk = N_HEADS * QK_NOPE_DIM
    nrow = N_HEADS * 8
    s = pl.program_id(0)
    n_seq = pl.num_programs(0) - 1
    slot = lax.rem(s, 2)

    def copies(seq, dst_slot):
        out = []
        for g in range(n_pages):
            page = 0 if seq is None else pt_ref[seq * n_pages + g]
            dst = pl.ds(g * PAGE_SIZE, PAGE_SIZE)
            out.append(pltpu.make_async_copy(lat_hbm.at[page], lat_buf.at[dst_slot, dst], sem_lat.at[dst_slot]))
            out.append(pltpu.make_async_copy(ropet_hbm.at[page], ropet_buf.at[dst_slot, :, dst],
                                             sem_rope.at[dst_slot]))
        return out

    @pl.when(s == 0)
    def _():
        a_scr[0:nk, :] = wukt_ref[...]
        scl_scr[...] = jnp.zeros_like(scl_scr)
        cbl_scr[...] = jnp.zeros_like(cbl_scr)
        m_scr[...] = jnp.zeros_like(m_scr)
        l_scr[...] = jnp.ones_like(l_scr)
        acc_scr[...] = jnp.zeros_like(acc_scr)
        for d in copies(0, 0):
            d.start()

    @pl.when(s < n_seq)
    def _():
        for d in copies(jnp.minimum(s + 1, n_seq - 1), 1 - slot):
            d.start()

    for d in copies(None, slot):
        d.wait()

    q2 = qcat_ref[...].reshape(nrow, KV_LORA_RANK + QK_ROPE_DIM)
    a_scr[nk:nk + nrow, :] = q2[:, :KV_LORA_RANK].astype(_BF16)
    q_rope = q2[:, KV_LORA_RANK:].astype(_BF16)

    def scores(c_b, s_rope):
        big = _dot_t(a_scr[...], c_b)
        rows = []
        for hd in range(N_HEADS):
            kx = big[hd * QK_NOPE_DIM:(hd + 1) * QK_NOPE_DIM, :]
            r = lax.rsqrt(jnp.sum(kx * kx, axis=0, keepdims=True) * (1.0 / QK_NOPE_DIM) + EPS)
            rows.append(big[nk + 8 * hd:nk + 8 * hd + 8, :] * r + s_rope[8 * hd:8 * hd + 8, :])
        return jnp.concatenate(rows, axis=0)

    sc_bufs = [sca_scr, scb_scr] * (n_chunks // 2 + 1)
    sc_bufs = sc_bufs[:n_chunks - 1] + [scl_scr]
    pad = PAGE_SIZE - cnew_ref.shape[0]

    def latent(c):
        return lat_buf[slot, pl.ds(c * tk, tk), :].astype(_BF16)

    def stage_scores(c):
        s_rope = _dot(q_rope, ropet_buf[slot, :, pl.ds(c * tk, tk)].astype(_BF16))
        if c < n_chunks - 1:
            sc_bufs[c][...] = scores(latent(c), s_rope)
        else:
            c_new = jnp.concatenate([cnew_ref[...], jnp.zeros((pad, KV_LORA_RANK), _F32)], axis=0)
            c_b = jnp.concatenate([latent(c), c_new.astype(_BF16)], axis=0)
            cbl_scr[...] = c_b
            kr_new = jnp.concatenate([krnew_ref[...], jnp.zeros((pad, QK_ROPE_DIM), _F32)], axis=0)
            s_rope = jnp.concatenate([s_rope, _dot_t(q_rope, kr_new.astype(_BF16))], axis=1)
            row = lax.broadcasted_iota(jnp.int32, (nrow, tk + PAGE_SIZE), 0)
            col = lax.broadcasted_iota(jnp.int32, (nrow, tk + PAGE_SIZE), 1)
            sc_bufs[c][...] = jnp.where(col - tk <= (row & 7), scores(c_b, s_rope), NEG_INF)

    def stage_update(c):
        _softmax_step(sc_bufs[c][...], latent(c), m_scr, l_scr, acc_scr)

    stage_scores(0)

    _softmax_step(scl_scr[...], cbl_scr[...], m_scr, l_scr, acc_scr)
    o_ref[...] = (acc_scr[...] / l_scr[...]).reshape(o_ref.shape)
    m_scr[...] = jnp.full_like(m_scr, NEG_INF)
    l_scr[...] = jnp.zeros_like(l_scr)
    acc_scr[...] = jnp.zeros_like(acc_scr)

    for c in range(1, n_chunks):
        stage_scores(c)
        stage_update(c - 1)


def _sample_attention(page_table, qcat, c_new, kr_new, wukt, cache_lat, cache_rope_t, t_new):
    n_seq, n_pages = page_table.shape
    assert t_new == 8 and n_pages % SATT_PAGES_PER_CHUNK == 0 and n_pages // SATT_PAGES_PER_CHUNK >= 2
    tk = SATT_PAGES_PER_CHUNK * PAGE_SIZE
    past = n_pages * PAGE_SIZE
    nrow = N_HEADS * t_new
    dq = KV_LORA_RANK + QK_ROPE_DIM
    cur = lambda s: jnp.minimum(s, n_seq - 1)
    prev = lambda s: jnp.maximum(s - 1, 0)
    grid_spec = pltpu.PrefetchScalarGridSpec(
        num_scalar_prefetch=1,
        grid=(n_seq + 1,),
        in_specs=[
            pl.BlockSpec((N_HEADS, t_new, dq), lambda s, pt: (0, cur(s), 0)),
            pl.BlockSpec((t_new, KV_LORA_RANK), lambda s, pt: (cur(s), 0)),
            pl.BlockSpec((t_new, QK_ROPE_DIM), lambda s, pt: (cur(s), 0)),
            pl.BlockSpec((N_HEADS * QK_NOPE_DIM, KV_LORA_RANK), lambda s, pt: (0, 0)),
            pl.BlockSpec(memory_space=pl.ANY),
            pl.BlockSpec(memory_space=pl.ANY),
        ],
        out_specs=pl.BlockSpec((N_HEADS, t_new, KV_LORA_RANK), lambda s, pt: (0, prev(s), 0)),
        scratch_shapes=[
            pltpu.VMEM((N_HEADS * QK_NOPE_DIM + nrow, KV_LORA_RANK), _BF16),
            pltpu.VMEM((2, past, KV_LORA_RANK), _F32),
            pltpu.VMEM((2, QK_ROPE_DIM, past), _F32),
            pltpu.VMEM((nrow, tk), _F32),
            pltpu.VMEM((nrow, tk), _F32),
            pltpu.VMEM((nrow, tk + PAGE_SIZE), _F32),
            pltpu.VMEM((tk + PAGE_SIZE, KV_LORA_RANK), _BF16),
            pltpu.SemaphoreType.DMA((2,)),
            pltpu.SemaphoreType.DMA((2,)),
            pltpu.VMEM((nrow, 1), _F32),
            pltpu.VMEM((nrow, 1), _F32),
            pltpu.VMEM((nrow, KV_LORA_RANK), _F32),
        ],
    )
    return pl.pallas_call(
        functools.partial(_sattn_kernel, n_pages),
        grid_spec=grid_spec,
        out_shape=jax.ShapeDtypeStruct((N_HEADS, n_seq * t_new, KV_LORA_RANK), _F32),
        compiler_params=pltpu.CompilerParams(
            dimension_semantics=("arbitrary",), vmem_limit_bytes=VMEM_LIMIT),
        name="sample_attention",
    )(page_table.reshape(-1), qcat, c_new, kr_new, wukt, cache_lat, cache_rope_t)


def _merge_kernel(from_latent, a_ref, gate_ref, gout_ref, x_ref, wuv_ref, woa_ref, wout_ref, o_ref):
    if from_latent:
        heads = [_dot(a_ref[hd].astype(_BF16), wuv_ref[:, hd * V_HEAD_DIM:(hd + 1) * V_HEAD_DIM])
                 for hd in range(N_HEADS)]
        attn = jnp.concatenate(heads, axis=-1).astype(_BF16)
    else:
        attn = a_ref[...]
    merged = gate_ref[...].astype(_F32) * _dot(attn, woa_ref[...]) + gout_ref[...]
    o_ref[...] = x_ref[...] + _dot(merged.astype(_BF16), wout_ref[...])


def _merge(from_latent, attn, gate, gout, x, wuv, woa, wout):
    n = x.shape[0]
    tm = min(MERGE_TM, n // 2)
    tok = pl.BlockSpec((tm, D_MODEL), lambda i: (i, 0))
    if from_latent:
        a_spec = pl.BlockSpec((N_HEADS, tm, KV_LORA_RANK), lambda i: (0, i, 0))
    else:
        a_spec = tok
    return pl.pallas_call(
        functools.partial(_merge_kernel, from_latent),
        grid=(n // tm,),
        in_specs=[a_spec, tok, tok, tok, _const_spec(wuv.shape), _const_spec(woa.shape),
                  _const_spec(wout.shape)],
        out_specs=tok,
        out_shape=jax.ShapeDtypeStruct((n, D_MODEL), _F32),
        compiler_params=pltpu.CompilerParams(
            dimension_semantics=("parallel",), vmem_limit_bytes=VMEM_LIMIT),
        name="merge_sample" if from_latent else "merge_prompt",
    )(attn, gate, gout, x, wuv, woa, wout)


def _swap_halves(a, axis):
    lo, hi = jnp.split(a, 2, axis=axis)
    return jnp.concatenate([hi, lo], axis=axis)


def _slot(a):
    return jnp.concatenate([a, jnp.zeros_like(a)], axis=-1)


def _rope_tables(pos):
    half = QK_ROPE_DIM // 2
    inv = ROPE_BASE ** (-np.arange(half, dtype=np.float64) / half)
    ang = np.asarray(pos, np.float64)[:, None] * inv[None, :]
    cos, sin = np.cos(ang), np.sin(ang)
    zero = np.zeros((ang.shape[0], QK_ROPE_DIM))
    return (jnp.asarray(np.concatenate([cos, cos, zero], axis=-1), _F32),
            jnp.asarray(np.concatenate([-sin, sin, zero], axis=-1), _F32))


def _prep_weights(w_in, w_uq, w_uk, w_uv, w_o_gmlp, mix_norm, q_lora_norm, kv_lora_norm, q_nope_norm,
                  q_rope_norm, k_nope_norm, k_rope_norm, gmlp_v_norm):
    w_uq = w_uq.astype(_BF16)
    off_kr = Q_LORA_RANK + KV_LORA_RANK
    w_in_t = w_in.T.astype(_BF16)
    k_r_t = w_in_t[off_kr:off_kr + QK_ROPE_DIM]
    zero = jnp.zeros_like(k_r_t)
    w_in_kr_t = jnp.concatenate([k_r_t, zero, _swap_halves(k_r_t, 0), zero], axis=0)
    assert w_in_kr_t.shape[0] == 2 * ROPE_SLOT and _D_IN_PADDED == _OFF_KR + 2 * ROPE_SLOT
    wq = w_uq.reshape(Q_LORA_RANK, N_HEADS, QK_HEAD_DIM)
    wq_nope = wq[:, :, :QK_NOPE_DIM].reshape(Q_LORA_RANK, -1)
    wq_rope = wq[:, :, QK_NOPE_DIM:]
    w_uq_p = jnp.concatenate([wq_nope, _slot(wq_rope).reshape(Q_LORA_RANK, -1),
                              _slot(_swap_halves(wq_rope, -1)).reshape(Q_LORA_RANK, -1)], axis=-1)
    row = lambda a: a.reshape(1, -1).astype(_F32)
    return dict(
        mix_norm=row(mix_norm), w_in_t=w_in_t, w_in_kr_t=w_in_kr_t, q_lora_norm=row(q_lora_norm),
        w_uq=w_uq_p.astype(_BF16), kv_lora_norm=row(kv_lora_norm), w_uk=w_uk.astype(_BF16),
        w_uv=w_uv.astype(_BF16), w_uv_t=w_uv.T.astype(_BF16), q_nope_norm=row(q_nope_norm), q_rope_norm=row(_slot(q_rope_norm)),
        q_rope_norm_s=row(_slot(_swap_halves(q_rope_norm, -1))), k_nope_norm=row(k_nope_norm),
        k_rope_norm=row(_slot(k_rope_norm)), k_rope_norm_s=row(_slot(_swap_halves(k_rope_norm, -1))),
        gmlp_v_norm=row(gmlp_v_norm), w_o_gmlp=w_o_gmlp.astype(_BF16))


def kernel(x_prompt, x_sample, cache_kv_latent, cache_k_rope, page_table, ffn1_norm, ffn1_w_gate, ffn1_w_up, ffn1_w_down, mix_norm, w_in, q_lora_norm, w_uq, kv_lora_norm, w_uk, w_uv, q_nope_norm, q_rope_norm, k_nope_norm, k_rope_norm, gmlp_v_norm, gmlp_w_s, gmlp_b_s, w_o_attn, w_o_gmlp, w_out, ffn2_norm, ffn2_w_gate, ffn2_w_up, ffn2_w_down):
    batch, seq, _ = x_prompt.shape
    n_seq, t_new, _ = x_sample.shape
    assert ffn1_norm.shape[0] == 1
    past = page_table.shape[1] * PAGE_SIZE
    n_pool = cache_kv_latent.shape[1]
    l = 0
    row = lambda a: a.reshape(1, -1).astype(_F32)
    w = _prep_weights(w_in[l], w_uq[l], w_uk[l], w_uv[l], w_o_gmlp[l], mix_norm[l], q_lora_norm[l],
                      kv_lora_norm[l], q_nope_norm[l], q_rope_norm[l], k_nope_norm[l], k_rope_norm[l],
                      gmlp_v_norm[l])
    ffn1 = (row(ffn1_norm[l]), ffn1_w_gate.reshape(D_MODEL, D_FF), ffn1_w_up.reshape(D_MODEL, D_FF),
            ffn1_w_down.reshape(D_FF, D_MODEL))
    ffn2 = (row(ffn2_norm[l]), ffn2_w_gate.reshape(D_MODEL, D_FF), ffn2_w_up.reshape(D_MODEL, D_FF),
            ffn2_w_down.reshape(D_FF, D_MODEL))
    woa = w_o_attn[l].astype(_BF16)
    wout = w_out[l].astype(_BF16)

    wp = dict(w, gmlp_w_s=gmlp_w_s[l].astype(_F32), gmlp_b_s=gmlp_b_s[l].reshape(GMLP_GROUPS, CHUNK, 1))
    cos_p, sin_p = _rope_tables(np.arange(seq))
    xp = _ffn(x_prompt.reshape(batch * seq, D_MODEL), *ffn1)
    q, k, v, ckv_p, kr_p, gate_p, gout_p = _proj(False, xp, wp, cos_p, sin_p, seq // PROJ_TM)
    attn_p = _prompt_attention(q, k, v, batch, seq)
    xp = _merge(False, attn_p, gate_p, gout_p, xp, w["w_uv"], woa, wout)
    xp = _ffn(xp, *ffn2)

    reps = CHUNK // t_new
    ws_s = jax.vmap(lambda m: jnp.kron(jnp.eye(reps, dtype=_F32), m))(gmlp_w_s[l][:, :t_new, :t_new])
    bs_s = jnp.tile(gmlp_b_s[l][:, :t_new], (1, reps)).reshape(GMLP_GROUPS, CHUNK, 1)
    ws = dict(w, gmlp_w_s=ws_s, gmlp_b_s=bs_s)
    cos_s, sin_s = _rope_tables(past + np.arange(PROJ_TM) % t_new)
    xs = _ffn(x_sample.reshape(n_seq * t_new, D_MODEL), *ffn1)
    qcat, ckv_s, kr_s, vn_s, gate_s, gout_s = _proj(True, xs, ws, cos_s, sin_s, 1)
    o_lat = _sample_attention(page_table, qcat, ckv_s, kr_s, w["w_uk"].T,
                              cache_kv_latent.reshape(n_pool, PAGE_SIZE, KV_LORA_RANK),
                              jnp.swapaxes(cache_k_rope.reshape(n_pool, PAGE_SIZE, QK_ROPE_DIM), 1, 2),
                              t_new)
    xs = _merge(True, o_lat, gate_s, gout_s, xs, w["w_uv"], woa, wout)
    xs = _ffn(xs, *ffn2)

    return (xp.reshape(batch, seq, D_MODEL), xs.reshape(n_seq, t_new, D_MODEL),
            ckv_p.reshape(1, batch, seq, KV_LORA_RANK), jnp.swapaxes(kr_p, 1, 2)[None],
            ckv_s.reshape(1, n_seq, t_new, KV_LORA_RANK), kr_s.reshape(1, n_seq, t_new, QK_ROPE_DIM),
            vn_s.reshape(1, n_seq, t_new, GMLP_WIDTH))
```

```python
import functools

import jax
import jax.numpy as jnp
import numpy as np
from jax import lax
from jax.experimental import pallas as pl
from jax.experimental.pallas import tpu as pltpu

D_MODEL = 1024
N_HEADS = 8
QK_NOPE_DIM = 128
QK_ROPE_DIM = 64
QK_HEAD_DIM = QK_NOPE_DIM + QK_ROPE_DIM
V_HEAD_DIM = 128
Q_LORA_RANK = 256
KV_LORA_RANK = 256
ROPE_BASE = 10000.0
ATTN_SCALE = QK_HEAD_DIM ** -0.5
LOG2E = 1.4426950408889634
GMLP_GROUPS = 4
GMLP_WIDTH = 1024
GMLP_GROUP_DIM = GMLP_WIDTH // GMLP_GROUPS
CHUNK = 128
PAGE_SIZE = 128
D_FF = 2816
EPS = 1e-6
NEG_INF = -1e30

LANE = 128
ROPE_SLOT = LANE

_OFF_CQ = 0
_OFF_CKV = _OFF_CQ + Q_LORA_RANK
_OFF_U = _OFF_CKV + KV_LORA_RANK
_OFF_V = _OFF_U + GMLP_WIDTH
_OFF_GA = _OFF_V + GMLP_WIDTH
_OFF_GG = _OFF_GA + D_MODEL
_OFF_KR = _OFF_GG + D_MODEL
_OFF_KRS = _OFF_KR + ROPE_SLOT
_D_IN_PADDED = _OFF_KRS + ROPE_SLOT

FFN_TM = 1024
FFN_TF = 256
PROJ_TM = 512
ATT_TQ = 512
ATT_TK = 512
ATT_HB = 4
ATT_ONES_ROWS = 16
MERGE_TM = 1024
SATT_PAGES_PER_CHUNK = 16
VMEM_LIMIT = 56 * 1024 * 1024

_BF16 = jnp.bfloat16
_F32 = jnp.float32


def _dot(a, b):
    return jnp.dot(a, b, preferred_element_type=_F32)


def _dot_t(a, b):
    return lax.dot_general(a, b, (((1,), (1,)), ((), ())), preferred_element_type=_F32)


def _rms(x, n):
    return lax.rsqrt(jnp.sum(x * x, axis=-1, keepdims=True) * (1.0 / n) + EPS)


def _ffn_kernel(x_ref, g_ref, wg_hbm, wu_hbm, wd_hbm, o_ref, wg_scr, wu_scr, wd_scr, stage_g, stage_u,
                stage_d, sem, act_scr):
    n_blk = D_FF // FFN_TF

    def block_copies(j, slot):
        cols = pl.ds(j * FFN_TF, FFN_TF)
        return [pltpu.make_async_copy(wg_hbm.at[:, cols], stage_g.at[slot], sem.at[slot]),
                pltpu.make_async_copy(wu_hbm.at[:, cols], stage_u.at[slot], sem.at[slot]),
                pltpu.make_async_copy(wd_hbm.at[cols, :], stage_d.at[slot], sem.at[slot])]

    def fetch_block(j):
        slot = j % 2
        if j + 1 < n_blk:
            for d in block_copies(j + 1, 1 - slot):
                d.start()
        for d in block_copies(j, slot):
            d.wait()
        sl = slice(j * FFN_TF, (j + 1) * FFN_TF)
        wg_scr[:, sl] = stage_g[slot].astype(_BF16)
        wu_scr[:, sl] = stage_u[slot].astype(_BF16)
        wd_scr[sl, :] = stage_d[slot].astype(_BF16)

    def tile(first_step):
        x = x_ref[...]
        h = (x * _rms(x, D_MODEL) * g_ref[...]).astype(_BF16)
        if first_step:
            for d in block_copies(0, 0):
                d.start()
        for j in range(n_blk):
            if first_step:
                fetch_block(j)
            sl = slice(j * FFN_TF, (j + 1) * FFN_TF)
            gate = _dot(h, wg_scr[:, sl])
            up = _dot(h, wu_scr[:, sl])
            act_scr[:, sl] = (gate * jax.nn.sigmoid(gate) * up).astype(_BF16)
        o_ref[...] = x + 0.5 * _dot(act_scr[...], wd_scr[...])

    pl.when(pl.program_id(0) == 0)(functools.partial(tile, True))
    pl.when(pl.program_id(0) > 0)(functools.partial(tile, False))


def _ffn(x, norm_g, wg, wu, wd):
    n = x.shape[0]
    tm = min(FFN_TM, n)
    any_spec = pl.BlockSpec(memory_space=pl.ANY)
    return pl.pallas_call(
        _ffn_kernel,
        grid=(n // tm,),
        in_specs=[pl.BlockSpec((tm, D_MODEL), lambda i: (i, 0)), _const_spec(norm_g.shape),
                  any_spec, any_spec, any_spec],
        out_specs=pl.BlockSpec((tm, D_MODEL), lambda i: (i, 0)),
        out_shape=jax.ShapeDtypeStruct((n, D_MODEL), _F32),
        scratch_shapes=[pltpu.VMEM((D_MODEL, D_FF), _BF16), pltpu.VMEM((D_MODEL, D_FF), _BF16),
                        pltpu.VMEM((D_FF, D_MODEL), _BF16),
                        pltpu.VMEM((2, D_MODEL, FFN_TF), _F32), pltpu.VMEM((2, D_MODEL, FFN_TF), _F32),
                        pltpu.VMEM((2, FFN_TF, D_MODEL), _F32),
                        pltpu.SemaphoreType.DMA((2,)),
                        pltpu.VMEM((tm, D_FF), _BF16)],
        compiler_params=pltpu.CompilerParams(
            dimension_semantics=("arbitrary",), vmem_limit_bytes=VMEM_LIMIT),
        name="ffn",
    )(x, norm_g, wg, wu, wd)


def _rope_slot(x, xs, g, gs, cos, sin):
    r = _rms(x, QK_ROPE_DIM)
    return (x * r * g) * cos + (xs * r * gs) * sin


def _gelu(x):
    return 0.5 * x * (1.0 + lax.erf(x * (2.0 ** -0.5)))


def _proj_kernel(sample, x_ref, mixg_ref, wt_ref, wkr_ref, qlg_ref, wuq_ref, kvg_ref, wuk_ref, wuv_ref,
                 qng_ref, qrg_ref, qrgs_ref, kng_ref, krg_ref, krgs_ref, vg_ref, ws_ref, bs_ref,
                 wog_ref, cos_ref, sin_ref, *rest):
    if sample:
        qcat_ref, ckv_ref, kr_ref, vn_ref, gate_ref, gout_ref, gm_scr = rest
    else:
        q_ref, k_ref, v_ref, ckv_ref, kr_ref, gate_ref, gout_ref, gm_scr = rest
    tm = x_ref.shape[0]
    x = x_ref[...]
    h = (x * _rms(x, D_MODEL) * mixg_ref[...]).astype(_BF16)
    cos = cos_ref[...]
    sin = sin_ref[...]

    def win(off, width):
        if off >= _OFF_KR:
            assert (off, width) == (_OFF_KR, 2 * ROPE_SLOT)
            return _dot_t(h, wkr_ref[...])
        row = off if off < _OFF_U else off + QK_ROPE_DIM
        return _dot_t(h, wt_ref[row:row + width, :])

    kr_raw = win(_OFF_KR, 2 * ROPE_SLOT)
    ckv = win(_OFF_CKV, KV_LORA_RANK)
    cq = win(_OFF_CQ, Q_LORA_RANK)
    u = win(_OFF_U, GMLP_WIDTH)
    v = win(_OFF_V, GMLP_WIDTH)

    kr = _rope_slot(kr_raw[:, :ROPE_SLOT], kr_raw[:, ROPE_SLOT:], krg_ref[...], krgs_ref[...],
                    cos, sin)[:, :QK_ROPE_DIM]
    if sample:
        kr_ref[...] = kr
    else:
        kr_ref[0] = kr.T
    ckv = ckv * _rms(ckv, KV_LORA_RANK) * kvg_ref[...]
    ckv_ref[...] = ckv
    ckv_b = ckv.astype(_BF16)
    cq_b = (cq * _rms(cq, Q_LORA_RANK) * qlg_ref[...]).astype(_BF16)

    nq = N_HEADS * QK_NOPE_DIM
    nr = N_HEADS * ROPE_SLOT
    if not sample:
        kexp = _dot(ckv_b, wuk_ref[...])
        vals_t = _dot_t(wuv_ref[...], ckv_b)
    q_nope = _dot(cq_b, wuq_ref[:, 0:nq])
    q_rope = _dot(cq_b, wuq_ref[:, nq:nq + nr])
    q_rope_s = _dot(cq_b, wuq_ref[:, nq + nr:nq + 2 * nr])
    gate_g = jax.nn.sigmoid(win(_OFF_GG, D_MODEL))
    gate_ref[...] = jax.nn.sigmoid(win(_OFF_GA, D_MODEL)).astype(gate_ref.dtype)

    if not sample:
        kr_b = kr.astype(_BF16)
        for hd in range(N_HEADS):
            sl = slice(hd * QK_NOPE_DIM, (hd + 1) * QK_NOPE_DIM)
            kh = kexp[:, sl]
            k_ref[hd, :, 0:QK_NOPE_DIM] = (kh * _rms(kh, QK_NOPE_DIM) * kng_ref[...]).astype(_BF16)
            k_ref[hd, :, QK_NOPE_DIM:QK_HEAD_DIM] = kr_b
            v_ref[hd] = vals_t[hd * V_HEAD_DIM:(hd + 1) * V_HEAD_DIM, :].astype(_BF16)

    q_scale = ATTN_SCALE if sample else ATTN_SCALE * LOG2E
    for hd in range(N_HEADS):
        sl = slice(hd * QK_NOPE_DIM, (hd + 1) * QK_NOPE_DIM)
        qh = q_nope[:, sl]
        qh = qh * _rms(qh, QK_NOPE_DIM) * qng_ref[...] * q_scale
        rs = slice(hd * ROPE_SLOT, (hd + 1) * ROPE_SLOT)
        qr = _rope_slot(q_rope[:, rs], q_rope_s[:, rs], qrg_ref[...], qrgs_ref[...], cos, sin)
        qr = qr[:, :QK_ROPE_DIM] * q_scale
        if sample:
            qa = _dot_t((qh * kng_ref[...]).astype(_BF16), wuk_ref[:, sl])
            qcat_ref[hd, :, 0:KV_LORA_RANK] = qa
            qcat_ref[hd, :, KV_LORA_RANK:KV_LORA_RANK + QK_ROPE_DIM] = qr
        else:
            q_ref[hd, :, 0:QK_NOPE_DIM] = qh.astype(_BF16)
            q_ref[hd, :, QK_NOPE_DIM:QK_HEAD_DIM] = qr.astype(_BF16)

    u = _gelu(u)
    v = _gelu(v)
    row = lax.broadcasted_iota(jnp.int32, (CHUNK, CHUNK), 0)
    col = lax.broadcasted_iota(jnp.int32, (CHUNK, CHUNK), 1)
    for g in range(GMLP_GROUPS):
        gs = slice(g * GMLP_GROUP_DIM, (g + 1) * GMLP_GROUP_DIM)
        vg = v[:, gs]
        vg = vg * _rms(vg, GMLP_GROUP_DIM) * vg_ref[:, gs]
        if sample:
            vn_ref[:, gs] = vg
        vg_b = vg.astype(_BF16)
        w = jnp.where(col <= row, ws_ref[g], 0.0).astype(_BF16)
        for c in range(tm // CHUNK):
            cs = slice(c * CHUNK, (c + 1) * CHUNK)
            mix = _dot(w, vg_b[cs]) + bs_ref[g]
            gm_scr[cs, gs] = (u[cs, gs] * mix).astype(_BF16)
    gout_ref[...] = (gate_g * _dot(gm_scr[...], wog_ref[...])).astype(gout_ref.dtype)


def _const_spec(shape):
    nd = len(shape)
    return pl.BlockSpec(shape, lambda i: (0,) * nd, pipeline_mode=pl.Buffered(1))


def _proj(sample, x, w, cos, sin, n_pos_tiles):
    n = x.shape[0]
    tm = PROJ_TM
    tok = lambda width: pl.BlockSpec((tm, width), lambda i: (i, 0))
    heads = lambda width: pl.BlockSpec((N_HEADS, tm, width), lambda i: (0, i, 0))
    weights = [w["mix_norm"], w["w_in_t"], w["w_in_kr_t"], w["q_lora_norm"], w["w_uq"],
               w["kv_lora_norm"], w["w_uk"],
               w["w_uv_t"], w["q_nope_norm"], w["q_rope_norm"], w["q_rope_norm_s"], w["k_nope_norm"],
               w["k_rope_norm"], w["k_rope_norm_s"], w["gmlp_v_norm"], w["gmlp_w_s"], w["gmlp_b_s"],
               w["w_o_gmlp"]]
    pos_spec = pl.BlockSpec((tm, ROPE_SLOT), lambda i: (i % n_pos_tiles, 0))
    in_specs = [tok(D_MODEL)] + [_const_spec(a.shape) for a in weights] + [pos_spec, pos_spec]
    sds = jax.ShapeDtypeStruct
    if sample:
        out_shape = [sds((N_HEADS, n, KV_LORA_RANK + QK_ROPE_DIM), _F32), sds((n, KV_LORA_RANK), _F32),
                     sds((n, QK_ROPE_DIM), _F32), sds((n, GMLP_WIDTH), _F32),
                     sds((n, D_MODEL), _BF16), sds((n, D_MODEL), _BF16)]
        out_specs = [heads(KV_LORA_RANK + QK_ROPE_DIM), tok(KV_LORA_RANK), tok(QK_ROPE_DIM),
                     tok(GMLP_WIDTH), tok(D_MODEL), tok(D_MODEL)]
    else:
        out_shape = [sds((N_HEADS, n, QK_HEAD_DIM), _BF16), sds((N_HEADS, n, QK_HEAD_DIM), _BF16),
                     sds((N_HEADS, V_HEAD_DIM, n), _BF16), sds((n, KV_LORA_RANK), _F32),
                     sds((n // (n_pos_tiles * tm), QK_ROPE_DIM, n_pos_tiles * tm), _F32),
                     sds((n, D_MODEL), _BF16), sds((n, D_MODEL), _BF16)]
        vt_spec = pl.BlockSpec((N_HEADS, V_HEAD_DIM, tm), lambda i: (0, 0, i))
        krt_spec = pl.BlockSpec((1, QK_ROPE_DIM, tm), lambda i: (i // n_pos_tiles, 0, i % n_pos_tiles))
        out_specs = [heads(QK_HEAD_DIM), heads(QK_HEAD_DIM), vt_spec, tok(KV_LORA_RANK),
                     krt_spec, tok(D_MODEL), tok(D_MODEL)]
    return pl.pallas_call(
        functools.partial(_proj_kernel, sample),
        grid=(n // tm,),
        in_specs=in_specs,
        out_specs=out_specs,
        out_shape=out_shape,
        scratch_shapes=[pltpu.VMEM((tm, GMLP_WIDTH), _BF16)],
        compiler_params=pltpu.CompilerParams(
            dimension_semantics=("parallel",), vmem_limit_bytes=VMEM_LIMIT),
        name="proj_sample" if sample else "proj_prompt",
    )(x, *weights, cos, sin)


def _softmax_step(s, v_b, m_scr, l_scr, acc_scr):
    m_prev = m_scr[...]
    m_new = jnp.maximum(m_prev, jnp.max(s, axis=-1, keepdims=True))
    alpha = jnp.exp(m_prev - m_new)
    p = jnp.exp(s - m_new)
    l_scr[...] = alpha * l_scr[...] + jnp.sum(p, axis=-1, keepdims=True)
    acc_scr[...] = alpha * acc_scr[...] + _dot(p.astype(_BF16), v_b)
    m_scr[...] = m_new


def _pattn_kernel(q_ref, k_ref, vt_ref, o_ref, sa_scr, sb_scr, m_scr, acc_scr):
    qi = pl.program_id(2)
    m_scr[...] = jnp.full_like(m_scr, NEG_INF)
    acc_scr[...] = jnp.zeros_like(acc_scr)

    def scores_head(kb, buf, masked, hh):
        off = pl.multiple_of(kb * ATT_TK, ATT_TK)
        s = _dot_t(k_ref[hh, pl.ds(off, ATT_TK), :], q_ref[hh])
        if masked:
            key = lax.broadcasted_iota(jnp.int32, s.shape, 0)
            qry = lax.broadcasted_iota(jnp.int32, s.shape, 1)
            s = jnp.where(key <= qry, s, NEG_INF)
        buf[hh] = s

    def update_head(kb, buf, hh):
        off = pl.multiple_of(kb * ATT_TK, ATT_TK)
        s = buf[hh]
        m_prev = m_scr[hh]
        m_new = jnp.maximum(m_prev, jnp.max(s, axis=0, keepdims=True))
        p = jnp.exp2(s - m_new).astype(_BF16)
        vt = vt_ref[hh, :, pl.ds(off, ATT_TK)]
        vt1 = jnp.concatenate([vt, jnp.ones((ATT_ONES_ROWS, ATT_TK), _BF16)], axis=0)
        acc_scr[hh] = jnp.exp2(m_prev - m_new) * acc_scr[hh] + _dot(vt1, p)
        m_scr[hh] = m_new

    def scores(kb, buf, masked):
        for hh in range(ATT_HB):
            scores_head(kb, buf, masked, hh)

    def update(kb, buf):
        for hh in range(ATT_HB):
            update_head(kb, buf, hh)

    def trip(t, ybuf, xbuf, masked):
        for hh in range(ATT_HB):
            scores_head(t, xbuf, masked, hh)
            update_head(t - 1, ybuf, hh)

    @pl.when(qi == 0)
    def _():
        scores(0, sa_scr, True)
        update(0, sa_scr)

    @pl.when(qi > 0)
    def _():
        scores(0, sa_scr, False)

        def pair(j, carry):
            trip(2 * j + 1, sa_scr, sb_scr, False)
            trip(2 * j + 2, sb_scr, sa_scr, False)
            return carry

        lax.fori_loop(0, lax.shift_right_logical(qi - 1, 1), pair, 0)

        @pl.when((qi & 1) == 1)
        def _():
            trip(qi, sa_scr, sb_scr, True)
            update(qi, sb_scr)

        @pl.when((qi & 1) == 0)
        def _():
            trip(qi - 1, sa_scr, sb_scr, False)
            trip(qi, sb_scr, sa_scr, True)
            update(qi, sa_scr)

    for hh in range(ATT_HB):
        acc = acc_scr[hh]
        o_t = acc[:V_HEAD_DIM] / acc[V_HEAD_DIM:V_HEAD_DIM + 1]
        o_ref[:, hh * V_HEAD_DIM:(hh + 1) * V_HEAD_DIM] = o_t.T.astype(o_ref.dtype)


def _prompt_attention(q, k, vt, batch, seq):
    assert ATT_TQ == ATT_TK
    nq = seq // ATT_TQ
    hb = ATT_HB
    return pl.pallas_call(
        _pattn_kernel,
        grid=(batch, N_HEADS // hb, nq),
        in_specs=[
            pl.BlockSpec((hb, ATT_TQ, QK_HEAD_DIM), lambda b, h, i: (h, b * nq + i, 0)),
            pl.BlockSpec((hb, seq, QK_HEAD_DIM), lambda b, h, i: (h, b, 0)),
            pl.BlockSpec((hb, V_HEAD_DIM, seq), lambda b, h, i: (h, 0, b)),
        ],
        out_specs=pl.BlockSpec((ATT_TQ, hb * V_HEAD_DIM), lambda b, h, i: (b * nq + i, h)),
        out_shape=jax.ShapeDtypeStruct((batch * seq, N_HEADS * V_HEAD_DIM), _BF16),
        scratch_shapes=[pltpu.VMEM((hb, ATT_TK, ATT_TQ), _F32),
                        pltpu.VMEM((hb, ATT_TK, ATT_TQ), _F32),
                        pltpu.VMEM((hb, 1, ATT_TQ), _F32),
                        pltpu.VMEM((hb, V_HEAD_DIM + ATT_ONES_ROWS, ATT_TQ), _F32)],
        compiler_params=pltpu.CompilerParams(
            dimension_semantics=("parallel", "parallel", "arbitrary"), vmem_limit_bytes=VMEM_LIMIT),
        name="prompt_attention",
    )(q, k, vt)


def _sattn_kernel(n_pages, pt_ref, qcat_ref, cnew_ref, krnew_ref, wukt_ref, lat_hbm, ropet_hbm, o_ref,
                  a_scr, lat_buf, ropet_buf, sca_scr, scb_scr, scl_scr, cbl_scr, sem_lat, sem_rope,
                  m_scr, l_scr, acc_scr):
    pages = SATT_PAGES_PER_CHUNK
    n_chunks = n_pages // pages
    tk = pages * PAGE_SIZE
    nk = N_HEADS * QK_NOPE_DIM
    nrow = N_HEADS * 8
    s = pl.program_id(0)
    n_seq = pl.num_programs(0) - 1
    slot = lax.rem(s, 2)

    def copies(seq, dst_slot):
        out = []
        for g in range(n_pages):
            page = 0 if seq is None else pt_ref[seq * n_pages + g]
            dst = pl.ds(g * PAGE_SIZE, PAGE_SIZE)
            out.append(pltpu.make_async_copy(lat_hbm.at[page], lat_buf.at[dst_slot, dst], sem_lat.at[dst_slot]))
            out.append(pltpu.make_async_copy(ropet_hbm.at[page], ropet_buf.at[dst_slot, :, dst],
                                             sem_rope.at[dst_slot]))
        return out

    @pl.when(s == 0)
    def _():
        a_scr[0:nk, :] = wukt_ref[...]
        scl_scr[...] = jnp.zeros_like(scl_scr)
        cbl_scr[...] = jnp.zeros_like(cbl_scr)
        m_scr[...] = jnp.zeros_like(m_scr)
        l_scr[...] = jnp.ones_like(l_scr)
        acc_scr[...] = jnp.zeros_like(acc_scr)
        for i, d in enumerate(copies(0, 0)):
            d.start(priority=i % 2)

    @pl.when(s < n_seq)
    def _():
        for i, d in enumerate(copies(jnp.minimum(s + 1, n_seq - 1), 1 - slot)):
            d.start(priority=i % 2)

    for d in copies(None, slot):
        d.wait()

    q2 = qcat_ref[...].reshape(nrow, KV_LORA_RANK + QK_ROPE_DIM)
    a_scr[nk:nk + nrow, :] = q2[:, :KV_LORA_RANK].astype(_BF16)
    q_rope = q2[:, KV_LORA_RANK:].astype(_BF16)

    def scores(c_b, s_rope):
        big = _dot_t(a_scr[...], c_b)
        rows = []
        for hd in range(N_HEADS):
            kx = big[hd * QK_NOPE_DIM:(hd + 1) * QK_NOPE_DIM, :]
            r = lax.rsqrt(jnp.sum(kx * kx, axis=0, keepdims=True) * (1.0 / QK_NOPE_DIM) + EPS)
            rows.append(big[nk + 8 * hd:nk + 8 * hd + 8, :] * r + s_rope[8 * hd:8 * hd + 8, :])
        return jnp.concatenate(rows, axis=0)

    sc_bufs = [sca_scr, scb_scr] * (n_chunks // 2 + 1)
    sc_bufs = sc_bufs[:n_chunks - 1] + [scl_scr]
    pad = PAGE_SIZE - cnew_ref.shape[0]

    def latent(c):
        return lat_buf[slot, pl.ds(c * tk, tk), :].astype(_BF16)

    def stage_scores(c):
        s_rope = _dot(q_rope, ropet_buf[slot, :, pl.ds(c * tk, tk)].astype(_BF16))
        if c < n_chunks - 1:
            sc_bufs[c][...] = scores(latent(c), s_rope)
        else:
            c_new = jnp.concatenate([cnew_ref[...], jnp.zeros((pad, KV_LORA_RANK), _F32)], axis=0)
            c_b = jnp.concatenate([latent(c), c_new.astype(_BF16)], axis=0)
            cbl_scr[...] = c_b
            kr_new = jnp.concatenate([krnew_ref[...], jnp.zeros((pad, QK_ROPE_DIM), _F32)], axis=0)
            s_rope = jnp.concatenate([s_rope, _dot_t(q_rope, kr_new.astype(_BF16))], axis=1)
            row = lax.broadcasted_iota(jnp.int32, (nrow, tk + PAGE_SIZE), 0)
            col = lax.broadcasted_iota(jnp.int32, (nrow, tk + PAGE_SIZE), 1)
            sc_bufs[c][...] = jnp.where(col - tk <= (row & 7), scores(c_b, s_rope), NEG_INF)

    def stage_update(c):
        _softmax_step(sc_bufs[c][...], latent(c), m_scr, l_scr, acc_scr)

    stage_scores(0)

    _softmax_step(scl_scr[...], cbl_scr[...], m_scr, l_scr, acc_scr)
    o_ref[...] = (acc_scr[...] / l_scr[...]).reshape(o_ref.shape)
    m_scr[...] = jnp.full_like(m_scr, NEG_INF)
    l_scr[...] = jnp.zeros_like(l_scr)
    acc_scr[...] = jnp.zeros_like(acc_scr)

    for c in range(1, n_chunks):
        stage_scores(c)
        stage_update(c - 1)


def _sample_attention(page_table, qcat, c_new, kr_new, wukt, cache_lat, cache_rope_t, t_new):
    n_seq, n_pages = page_table.shape
    assert t_new == 8 and n_pages % SATT_PAGES_PER_CHUNK == 0 and n_pages // SATT_PAGES_PER_CHUNK >= 2
    tk = SATT_PAGES_PER_CHUNK * PAGE_SIZE
    past = n_pages * PAGE_SIZE
    nrow = N_HEADS * t_new
    dq = KV_LORA_RANK + QK_ROPE_DIM
    cur = lambda s: jnp.minimum(s, n_seq - 1)
    prev = lambda s: jnp.maximum(s - 1, 0)
    grid_spec = pltpu.PrefetchScalarGridSpec(
        num_scalar_prefetch=1,
        grid=(n_seq + 1,),
        in_specs=[
            pl.BlockSpec((N_HEADS, t_new, dq), lambda s, pt: (0, cur(s), 0)),
            pl.BlockSpec((t_new, KV_LORA_RANK), lambda s, pt: (cur(s), 0)),
            pl.BlockSpec((t_new, QK_ROPE_DIM), lambda s, pt: (cur(s), 0)),
            pl.BlockSpec((N_HEADS * QK_NOPE_DIM, KV_LORA_RANK), lambda s, pt: (0, 0)),
            pl.BlockSpec(memory_space=pl.ANY),
            pl.BlockSpec(memory_space=pl.ANY),
        ],
        out_specs=pl.BlockSpec((N_HEADS, t_new, KV_LORA_RANK), lambda s, pt: (0, prev(s), 0)),
        scratch_shapes=[
            pltpu.VMEM((N_HEADS * QK_NOPE_DIM + nrow, KV_LORA_RANK), _BF16),
            pltpu.VMEM((2, past, KV_LORA_RANK), _F32),
            pltpu.VMEM((2, QK_ROPE_DIM, past), _F32),
            pltpu.VMEM((nrow, tk), _F32),
            pltpu.VMEM((nrow, tk), _F32),
            pltpu.VMEM((nrow, tk + PAGE_SIZE), _F32),
            pltpu.VMEM((tk + PAGE_SIZE, KV_LORA_RANK), _BF16),
            pltpu.SemaphoreType.DMA((2,)),
            pltpu.SemaphoreType.DMA((2,)),
            pltpu.VMEM((nrow, 1), _F32),
            pltpu.VMEM((nrow, 1), _F32),
            pltpu.VMEM((nrow, KV_LORA_RANK), _F32),
        ],
    )
    return pl.pallas_call(
        functools.partial(_sattn_kernel, n_pages),
        grid_spec=grid_spec,
        out_shape=jax.ShapeDtypeStruct((N_HEADS, n_seq * t_new, KV_LORA_RANK), _F32),
        compiler_params=pltpu.CompilerParams(
            dimension_semantics=("arbitrary",), vmem_limit_bytes=VMEM_LIMIT),
        name="sample_attention",
    )(page_table.reshape(-1), qcat, c_new, kr_new, wukt, cache_lat, cache_rope_t)


def _merge_kernel(from_latent, a_ref, gate_ref, gout_ref, x_ref, wuv_ref, woa_ref, wout_ref, o_ref):
    if from_latent:
        heads = [_dot(a_ref[hd].astype(_BF16), wuv_ref[:, hd * V_HEAD_DIM:(hd + 1) * V_HEAD_DIM])
                 for hd in range(N_HEADS)]
        attn = jnp.concatenate(heads, axis=-1).astype(_BF16)
    else:
        attn = a_ref[...]
    merged = gate_ref[...].astype(_F32) * _dot(attn, woa_ref[...]) + gout_ref[...]
    o_ref[...] = x_ref[...] + _dot(merged.astype(_BF16), wout_ref[...])


def _merge(from_latent, attn, gate, gout, x, wuv, woa, wout):
    n = x.shape[0]
    tm = min(MERGE_TM, n // 2)
    tok = pl.BlockSpec((tm, D_MODEL), lambda i: (i, 0))
    if from_latent:
        a_spec = pl.BlockSpec((N_HEADS, tm, KV_LORA_RANK), lambda i: (0, i, 0))
    else:
        a_spec = tok
    return pl.pallas_call(
        functools.partial(_merge_kernel, from_latent),
        grid=(n // tm,),
        in_specs=[a_spec, tok, tok, tok, _const_spec(wuv.shape), _const_spec(woa.shape),
                  _const_spec(wout.shape)],
        out_specs=tok,
        out_shape=jax.ShapeDtypeStruct((n, D_MODEL), _F32),
        compiler_params=pltpu.CompilerParams(
            dimension_semantics=("parallel",), vmem_limit_bytes=VMEM_LIMIT),
        name="merge_sample" if from_latent else "merge_prompt",
    )(attn, gate, gout, x, wuv, woa, wout)


def _swap_halves(a, axis):
    lo, hi = jnp.split(a, 2, axis=axis)
    return jnp.concatenate([hi, lo], axis=axis)


def _slot(a):
    return jnp.concatenate([a, jnp.zeros_like(a)], axis=-1)


def _rope_tables(pos):
    half = QK_ROPE_DIM // 2
    inv = ROPE_BASE ** (-np.arange(half, dtype=np.float64) / half)
    ang = np.asarray(pos, np.float64)[:, None] * inv[None, :]
    cos, sin = np.cos(ang), np.sin(ang)
    zero = np.zeros((ang.shape[0], QK_ROPE_DIM))
    return (jnp.asarray(np.concatenate([cos, cos, zero], axis=-1), _F32),
            jnp.asarray(np.concatenate([-sin, sin, zero], axis=-1), _F32))


def _prep_weights(w_in, w_uq, w_uk, w_uv, w_o_gmlp, mix_norm, q_lora_norm, kv_lora_norm, q_nope_norm,
                  q_rope_norm, k_nope_norm, k_rope_norm, gmlp_v_norm):
    w_uq = w_uq.astype(_BF16)
    off_kr = Q_LORA_RANK + KV_LORA_RANK
    w_in_t = w_in.T.astype(_BF16)
    k_r_t = w_in_t[off_kr:off_kr + QK_ROPE_DIM]
    zero = jnp.zeros_like(k_r_t)
    w_in_kr_t = jnp.concatenate([k_r_t, zero, _swap_halves(k_r_t, 0), zero], axis=0)
    assert w_in_kr_t.shape[0] == 2 * ROPE_SLOT and _D_IN_PADDED == _OFF_KR + 2 * ROPE_SLOT
    wq = w_uq.reshape(Q_LORA_RANK, N_HEADS, QK_HEAD_DIM)
    wq_nope = wq[:, :, :QK_NOPE_DIM].reshape(Q_LORA_RANK, -1)
    wq_rope = wq[:, :, QK_NOPE_DIM:]
    w_uq_p = jnp.concatenate([wq_nope, _slot(wq_rope).reshape(Q_LORA_RANK, -1),
                              _slot(_swap_halves(wq_rope, -1)).reshape(Q_LORA_RANK, -1)], axis=-1)
    row = lambda a: a.reshape(1, -1).astype(_F32)
    return dict(
        mix_norm=row(mix_norm), w_in_t=w_in_t, w_in_kr_t=w_in_kr_t, q_lora_norm=row(q_lora_norm),
        w_uq=w_uq_p.astype(_BF16), kv_lora_norm=row(kv_lora_norm), w_uk=w_uk.astype(_BF16),
        w_uv=w_uv.astype(_BF16), w_uv_t=w_uv.T.astype(_BF16), q_nope_norm=row(q_nope_norm), q_rope_norm=row(_slot(q_rope_norm)),
        q_rope_norm_s=row(_slot(_swap_halves(q_rope_norm, -1))), k_nope_norm=row(k_nope_norm),
        k_rope_norm=row(_slot(k_rope_norm)), k_rope_norm_s=row(_slot(_swap_halves(k_rope_norm, -1))),
        gmlp_v_norm=row(gmlp_v_norm), w_o_gmlp=w_o_gmlp.astype(_BF16))


def kernel(x_prompt, x_sample, cache_kv_latent, cache_k_rope, page_table, ffn1_norm, ffn1_w_gate, ffn1_w_up, ffn1_w_down, mix_norm, w_in, q_lora_norm, w_uq, kv_lora_norm, w_uk, w_uv, q_nope_norm, q_rope_norm, k_nope_norm, k_rope_norm, gmlp_v_norm, gmlp_w_s, gmlp_b_s, w_o_attn, w_o_gmlp, w_out, ffn2_norm, ffn2_w_gate, ffn2_w_up, ffn2_w_down):
    batch, seq, _ = x_prompt.shape
    n_seq, t_new, _ = x_sample.shape
    assert ffn1_norm.shape[0] == 1
    past = page_table.shape[1] * PAGE_SIZE
    n_pool = cache_kv_latent.shape[1]
    l = 0
    row = lambda a: a.reshape(1, -1).astype(_F32)
    w = _prep_weights(w_in[l], w_uq[l], w_uk[l], w_uv[l], w_o_gmlp[l], mix_norm[l], q_lora_norm[l],
                      kv_lora_norm[l], q_nope_norm[l], q_rope_norm[l], k_nope_norm[l], k_rope_norm[l],
                      gmlp_v_norm[l])
    ffn1 = (row(ffn1_norm[l]), ffn1_w_gate.reshape(D_MODEL, D_FF), ffn1_w_up.reshape(D_MODEL, D_FF),
            ffn1_w_down.reshape(D_FF, D_MODEL))
    ffn2 = (row(ffn2_norm[l]), ffn2_w_gate.reshape(D_MODEL, D_FF), ffn2_w_up.reshape(D_MODEL, D_FF),
            ffn2_w_down.reshape(D_FF, D_MODEL))
    woa = w_o_attn[l].astype(_BF16)
    wout = w_out[l].astype(_BF16)

    wp = dict(w, gmlp_w_s=gmlp_w_s[l].astype(_F32), gmlp_b_s=gmlp_b_s[l].reshape(GMLP_GROUPS, CHUNK, 1))
    cos_p, sin_p = _rope_tables(np.arange(seq))
    xp = _ffn(x_prompt.reshape(batch * seq, D_MODEL), *ffn1)
    q, k, v, ckv_p, kr_p, gate_p, gout_p = _proj(False, xp, wp, cos_p, sin_p, seq // PROJ_TM)
    attn_p = _prompt_attention(q, k, v, batch, seq)
    xp = _merge(False, attn_p, gate_p, gout_p, xp, w["w_uv"], woa, wout)
    xp = _ffn(xp, *ffn2)

    reps = CHUNK // t_new
    ws_s = jax.vmap(lambda m: jnp.kron(jnp.eye(reps, dtype=_F32), m))(gmlp_w_s[l][:, :t_new, :t_new])
    bs_s = jnp.tile(gmlp_b_s[l][:, :t_new], (1, reps)).reshape(GMLP_GROUPS, CHUNK, 1)
    ws = dict(w, gmlp_w_s=ws_s, gmlp_b_s=bs_s)
    cos_s, sin_s = _rope_tables(past + np.arange(PROJ_TM) % t_new)
    xs = _ffn(x_sample.reshape(n_seq * t_new, D_MODEL), *ffn1)
    qcat, ckv_s, kr_s, vn_s, gate_s, gout_s = _proj(True, xs, ws, cos_s, sin_s, 1)
    o_lat = _sample_attention(page_table, qcat, ckv_s, kr_s, w["w_uk"].T,
                              cache_kv_latent.reshape(n_pool, PAGE_SIZE, KV_LORA_RANK),
                              jnp.swapaxes(cache_k_rope.reshape(n_pool, PAGE_SIZE, QK_ROPE_DIM), 1, 2),
                              t_new)
    xs = _merge(True, o_lat, gate_s, gout_s, xs, w["w_uv"], woa, wout)
    xs = _ffn(xs, *ffn2)

    return (xp.reshape(batch, seq, D_MODEL), xs.reshape(n_seq, t_new, D_MODEL),
            ckv_p.reshape(1, batch, seq, KV_LORA_RANK), jnp.swapaxes(kr_p, 1, 2)[None],
            ckv_s.reshape(1, n_seq, t_new, KV_LORA_RANK), kr_s.reshape(1, n_seq, t_new, QK_ROPE_DIM),
            vn_s.reshape(1, n_seq, t_new, GMLP_WIDTH))
```

```python
import functools

import jax
import jax.numpy as jnp
import numpy as np
from jax import lax
from jax.experimental import pallas as pl
from jax.experimental.pallas import tpu as pltpu

D_MODEL = 1024
N_HEADS = 8
QK_NOPE_DIM = 128
QK_ROPE_DIM = 64
QK_HEAD_DIM = QK_NOPE_DIM + QK_ROPE_DIM
V_HEAD_DIM = 128
Q_LORA_RANK = 256
KV_LORA_RANK = 256
ROPE_BASE = 10000.0
ATTN_SCALE = QK_HEAD_DIM ** -0.5
LOG2E = 1.4426950408889634
GMLP_GROUPS = 4
GMLP_WIDTH = 1024
GMLP_GROUP_DIM = GMLP_WIDTH // GMLP_GROUPS
CHUNK = 128
PAGE_SIZE = 128
D_FF = 2816
EPS = 1e-6
NEG_INF = -1e30

LANE = 128
ROPE_SLOT = LANE

_OFF_CQ = 0
_OFF_CKV = _OFF_CQ + Q_LORA_RANK
_OFF_U = _OFF_CKV + KV_LORA_RANK
_OFF_V = _OFF_U + GMLP_WIDTH
_OFF_GA = _OFF_V + GMLP_WIDTH
_OFF_GG = _OFF_GA + D_MODEL
_OFF_KR = _OFF_GG + D_MODEL
_OFF_KRS = _OFF_KR + ROPE_SLOT
_D_IN_PADDED = _OFF_KRS + ROPE_SLOT

FFN_TM = 1024
FFN_TF = 256
PROJ_TM = 512
ATT_TQ = 512
ATT_TK = 512
ATT_HB = 4
ATT_ONES_ROWS = 16
MERGE_TM = 1024
SATT_PAGES_PER_CHUNK = 16
VMEM_LIMIT = 56 * 1024 * 1024

_BF16 = jnp.bfloat16
_F32 = jnp.float32


def _dot(a, b):
    return jnp.dot(a, b, preferred_element_type=_F32)


def _dot_t(a, b):
    return lax.dot_general(a, b, (((1,), (1,)), ((), ())), preferred_element_type=_F32)


def _rms(x, n):
    return lax.rsqrt(jnp.sum(x * x, axis=-1, keepdims=True) * (1.0 / n) + EPS)


def _ffn_kernel(x_ref, g_ref, wg_hbm, wu_hbm, wd_hbm, o_ref, wg_scr, wu_scr, wd_scr, stage_g, stage_u,
                stage_d, sem, act_scr):
    n_blk = D_FF // FFN_TF

    def block_copies(j, slot):
        cols = pl.ds(j * FFN_TF, FFN_TF)
        return [pltpu.make_async_copy(wg_hbm.at[:, cols], stage_g.at[slot], sem.at[slot]),
                pltpu.make_async_copy(wu_hbm.at[:, cols], stage_u.at[slot], sem.at[slot]),
                pltpu.make_async_copy(wd_hbm.at[cols, :], stage_d.at[slot], sem.at[slot])]

    def fetch_block(j):
        slot = j % 2
        if j + 1 < n_blk:
            for d in block_copies(j + 1, 1 - slot):
                d.start()
        for d in block_copies(j, slot):
            d.wait()
        sl = slice(j * FFN_TF, (j + 1) * FFN_TF)
        wg_scr[:, sl] = stage_g[slot].astype(_BF16)
        wu_scr[:, sl] = stage_u[slot].astype(_BF16)
        wd_scr[sl, :] = stage_d[slot].astype(_BF16)

    def tile(first_step):
        x = x_ref[...]
        h = (x * _rms(x, D_MODEL) * g_ref[...]).astype(_BF16)
        if first_step:
            for d in block_copies(0, 0):
                d.start()
        for j in range(n_blk):
            if first_step:
                fetch_block(j)
            sl = slice(j * FFN_TF, (j + 1) * FFN_TF)
            gate = _dot(h, wg_scr[:, sl])
            up = _dot(h, wu_scr[:, sl])
            act_scr[:, sl] = (gate * jax.nn.sigmoid(gate) * up).astype(_BF16)
        o_ref[...] = x + 0.5 * _dot(act_scr[...], wd_scr[...])

    pl.when(pl.program_id(0) == 0)(functools.partial(tile, True))
    pl.when(pl.program_id(0) > 0)(functools.partial(tile, False))


def _ffn(x, norm_g, wg, wu, wd):
    n = x.shape[0]
    tm = min(FFN_TM, n)
    any_spec = pl.BlockSpec(memory_space=pl.ANY)
    return pl.pallas_call(
        _ffn_kernel,
        grid=(n // tm,),
        in_specs=[pl.BlockSpec((tm, D_MODEL), lambda i: (i, 0)), _const_spec(norm_g.shape),
                  any_spec, any_spec, any_spec],
        out_specs=pl.BlockSpec((tm, D_MODEL), lambda i: (i, 0)),
        out_shape=jax.ShapeDtypeStruct((n, D_MODEL), _F32),
        scratch_shapes=[pltpu.VMEM((D_MODEL, D_FF), _BF16), pltpu.VMEM((D_MODEL, D_FF), _BF16),
                        pltpu.VMEM((D_FF, D_MODEL), _BF16),
                        pltpu.VMEM((2, D_MODEL, FFN_TF), _F32), pltpu.VMEM((2, D_MODEL, FFN_TF), _F32),
                        pltpu.VMEM((2, FFN_TF, D_MODEL), _F32),
                        pltpu.SemaphoreType.DMA((2,)),
                        pltpu.VMEM((tm, D_FF), _BF16)],
        compiler_params=pltpu.CompilerParams(
            dimension_semantics=("arbitrary",), vmem_limit_bytes=VMEM_LIMIT),
        name="ffn",
    )(x, norm_g, wg, wu, wd)


def _rope_slot(x, xs, g, gs, cos, sin):
    r = _rms(x, QK_ROPE_DIM)
    return (x * r * g) * cos + (xs * r * gs) * sin


def _gelu(x):
    return 0.5 * x * (1.0 + lax.erf(x * (2.0 ** -0.5)))


def _proj_kernel(sample, x_ref, mixg_ref, wt_ref, wkr_ref, qlg_ref, wuq_ref, kvg_ref, wuk_ref, wuv_ref,
                 qng_ref, qrg_ref, qrgs_ref, kng_ref, krg_ref, krgs_ref, vg_ref, ws_ref, bs_ref,
                 wog_ref, cos_ref, sin_ref, *rest):
    if sample:
        qcat_ref, ckv_ref, kr_ref, vn_ref, gate_ref, gout_ref, gm_scr = rest
    else:
        q_ref, k_ref, v_ref, ckv_ref, kr_ref, gate_ref, gout_ref, gm_scr = rest
    tm = x_ref.shape[0]
    x = x_ref[...]
    h = (x * _rms(x, D_MODEL) * mixg_ref[...]).astype(_BF16)
    cos = cos_ref[...]
    sin = sin_ref[...]

    def win(off, width):
        if off >= _OFF_KR:
            assert (off, width) == (_OFF_KR, 2 * ROPE_SLOT)
            return _dot_t(h, wkr_ref[...])
        row = off if off < _OFF_U else off + QK_ROPE_DIM
        return _dot_t(h, wt_ref[row:row + width, :])

    kr_raw = win(_OFF_KR, 2 * ROPE_SLOT)
    ckv = win(_OFF_CKV, KV_LORA_RANK)
    cq = win(_OFF_CQ, Q_LORA_RANK)
    u = win(_OFF_U, GMLP_WIDTH)
    v = win(_OFF_V, GMLP_WIDTH)

    kr = _rope_slot(kr_raw[:, :ROPE_SLOT], kr_raw[:, ROPE_SLOT:], krg_ref[...], krgs_ref[...],
                    cos, sin)[:, :QK_ROPE_DIM]
    if sample:
        kr_ref[...] = kr
    else:
        kr_ref[0] = kr.T
    ckv = ckv * _rms(ckv, KV_LORA_RANK) * kvg_ref[...]
    ckv_ref[...] = ckv
    ckv_b = ckv.astype(_BF16)
    cq_b = (cq * _rms(cq, Q_LORA_RANK) * qlg_ref[...]).astype(_BF16)

    nq = N_HEADS * QK_NOPE_DIM
    nr = N_HEADS * ROPE_SLOT
    if not sample:
        kexp = _dot(ckv_b, wuk_ref[...])
        vals_t = _dot_t(wuv_ref[...], ckv_b)
    q_nope = _dot(cq_b, wuq_ref[:, 0:nq])
    q_rope = _dot(cq_b, wuq_ref[:, nq:nq + nr])
    q_rope_s = _dot(cq_b, wuq_ref[:, nq + nr:nq + 2 * nr])
    gate_g = jax.nn.sigmoid(win(_OFF_GG, D_MODEL))
    gate_ref[...] = jax.nn.sigmoid(win(_OFF_GA, D_MODEL)).astype(gate_ref.dtype)

    if not sample:
        kr_b = kr.astype(_BF16)
        for hd in range(N_HEADS):
            sl = slice(hd * QK_NOPE_DIM, (hd + 1) * QK_NOPE_DIM)
            kh = kexp[:, sl]
            k_ref[hd, :, 0:QK_NOPE_DIM] = (kh * _rms(kh, QK_NOPE_DIM) * kng_ref[...]).astype(_BF16)
            k_ref[hd, :, QK_NOPE_DIM:QK_HEAD_DIM] = kr_b
            v_ref[hd] = vals_t[hd * V_HEAD_DIM:(hd + 1) * V_HEAD_DIM, :].astype(_BF16)

    q_scale = ATTN_SCALE if sample else ATTN_SCALE * LOG2E
    for hd in range(N_HEADS):
        sl = slice(hd * QK_NOPE_DIM, (hd + 1) * QK_NOPE_DIM)
        qh = q_nope[:, sl]
        qh = qh * _rms(qh, QK_NOPE_DIM) * qng_ref[...] * q_scale
        rs = slice(hd * ROPE_SLOT, (hd + 1) * ROPE_SLOT)
        qr = _rope_slot(q_rope[:, rs], q_rope_s[:, rs], qrg_ref[...], qrgs_ref[...], cos, sin)
        qr = qr[:, :QK_ROPE_DIM] * q_scale
        if sample:
            qa = _dot_t((qh * kng_ref[...]).astype(_BF16), wuk_ref[:, sl])
            qcat_ref[hd, :, 0:KV_LORA_RANK] = qa
            qcat_ref[hd, :, KV_LORA_RANK:KV_LORA_RANK + QK_ROPE_DIM] = qr
        else:
            q_ref[hd, :, 0:QK_NOPE_DIM] = qh.astype(_BF16)
            q_ref[hd, :, QK_NOPE_DIM:QK_HEAD_DIM] = qr.astype(_BF16)

    u = _gelu(u)
    v = _gelu(v)
    row = lax.broadcasted_iota(jnp.int32, (CHUNK, CHUNK), 0)
    col = lax.broadcasted_iota(jnp.int32, (CHUNK, CHUNK), 1)
    for g in range(GMLP_GROUPS):
        gs = slice(g * GMLP_GROUP_DIM, (g + 1) * GMLP_GROUP_DIM)
        vg = v[:, gs]
        vg = vg * _rms(vg, GMLP_GROUP_DIM) * vg_ref[:, gs]
        if sample:
            vn_ref[:, gs] = vg
        vg_b = vg.astype(_BF16)
        w = jnp.where(col <= row, ws_ref[g], 0.0).astype(_BF16)
        for c in range(tm // CHUNK):
            cs = slice(c * CHUNK, (c + 1) * CHUNK)
            mix = _dot(w, vg_b[cs]) + bs_ref[g]
            gm_scr[cs, gs] = (u[cs, gs] * mix).astype(_BF16)
    gout_ref[...] = (gate_g * _dot(gm_scr[...], wog_ref[...])).astype(gout_ref.dtype)


def _const_spec(shape):
    nd = len(shape)
    return pl.BlockSpec(shape, lambda i: (0,) * nd, pipeline_mode=pl.Buffered(1))


def _proj(sample, x, w, cos, sin, n_pos_tiles):
    n = x.shape[0]
    tm = PROJ_TM
    tok = lambda width: pl.BlockSpec((tm, width), lambda i: (i, 0))
    heads = lambda width: pl.BlockSpec((N_HEADS, tm, width), lambda i: (0, i, 0))
    weights = [w["mix_norm"], w["w_in_t"], w["w_in_kr_t"], w["q_lora_norm"], w["w_uq"],
               w["kv_lora_norm"], w["w_uk"],
               w["w_uv_t"], w["q_nope_norm"], w["q_rope_norm"], w["q_rope_norm_s"], w["k_nope_norm"],
               w["k_rope_norm"], w["k_rope_norm_s"], w["gmlp_v_norm"], w["gmlp_w_s"], w["gmlp_b_s"],
               w["w_o_gmlp"]]
    pos_spec = pl.BlockSpec((tm, ROPE_SLOT), lambda i: (i % n_pos_tiles, 0))
    in_specs = [tok(D_MODEL)] + [_const_spec(a.shape) for a in weights] + [pos_spec, pos_spec]
    sds = jax.ShapeDtypeStruct
    if sample:
        out_shape = [sds((N_HEADS, n, KV_LORA_RANK + QK_ROPE_DIM), _F32), sds((n, KV_LORA_RANK), _F32),
                     sds((n, QK_ROPE_DIM), _F32), sds((n, GMLP_WIDTH), _F32),
                     sds((n, D_MODEL), _BF16), sds((n, D_MODEL), _BF16)]
        out_specs = [heads(KV_LORA_RANK + QK_ROPE_DIM), tok(KV_LORA_RANK), tok(QK_ROPE_DIM),
                     tok(GMLP_WIDTH), tok(D_MODEL), tok(D_MODEL)]
    else:
        out_shape = [sds((N_HEADS, n, QK_HEAD_DIM), _BF16), sds((N_HEADS, n, QK_HEAD_DIM), _BF16),
                     sds((N_HEADS, V_HEAD_DIM, n), _BF16), sds((n, KV_LORA_RANK), _F32),
                     sds((n // (n_pos_tiles * tm), QK_ROPE_DIM, n_pos_tiles * tm), _F32),
                     sds((n, D_MODEL), _BF16), sds((n, D_MODEL), _BF16)]
        vt_spec = pl.BlockSpec((N_HEADS, V_HEAD_DIM, tm), lambda i: (0, 0, i))
        krt_spec = pl.BlockSpec((1, QK_ROPE_DIM, tm), lambda i: (i // n_pos_tiles, 0, i % n_pos_tiles))
        out_specs = [heads(QK_HEAD_DIM), heads(QK_HEAD_DIM), vt_spec, tok(KV_LORA_RANK),
                     krt_spec, tok(D_MODEL), tok(D_MODEL)]
    return pl.pallas_call(
        functools.partial(_proj_kernel, sample),
        grid=(n // tm,),
        in_specs=in_specs,
        out_specs=out_specs,
        out_shape=out_shape,
        scratch_shapes=[pltpu.VMEM((tm, GMLP_WIDTH), _BF16)],
        compiler_params=pltpu.CompilerParams(
            dimension_semantics=("parallel",), vmem_limit_bytes=VMEM_LIMIT),
        name="proj_sample" if sample else "proj_prompt",
    )(x, *weights, cos, sin)


def _softmax_step(s, v_b, m_scr, l_scr, acc_scr):
    m_prev = m_scr[...]
    m_new = jnp.maximum(m_prev, jnp.max(s, axis=-1, keepdims=True))
    alpha = jnp.exp(m_prev - m_new)
    p = jnp.exp(s - m_new)
    l_scr[...] = alpha * l_scr[...] + jnp.sum(p, axis=-1, keepdims=True)
    acc_scr[...] = alpha * acc_scr[...] + _dot(p.astype(_BF16), v_b)
    m_scr[...] = m_new


def _pattn_kernel(q_ref, k_ref, vt_ref, o_ref, sa_scr, sb_scr, m_scr, acc_scr):
    qi = pl.program_id(2)
    m_scr[...] = jnp.full_like(m_scr, NEG_INF)
    acc_scr[...] = jnp.zeros_like(acc_scr)

    def scores_head(kb, buf, masked, hh):
        off = pl.multiple_of(kb * ATT_TK, ATT_TK)
        s = _dot_t(k_ref[hh, pl.ds(off, ATT_TK), :], q_ref[hh])
        if masked:
            key = lax.broadcasted_iota(jnp.int32, s.shape, 0)
            qry = lax.broadcasted_iota(jnp.int32, s.shape, 1)
            s = jnp.where(key <= qry, s, NEG_INF)
        buf[hh] = s

    def update_head(kb, buf, hh):
        off = pl.multiple_of(kb * ATT_TK, ATT_TK)
        s = buf[hh]
        m_prev = m_scr[hh]
        m_new = jnp.maximum(m_prev, jnp.max(s, axis=0, keepdims=True))
        p = jnp.exp2(s - m_new).astype(_BF16)
        vt = vt_ref[hh, :, pl.ds(off, ATT_TK)]
        vt1 = jnp.concatenate([vt, jnp.ones((ATT_ONES_ROWS, ATT_TK), _BF16)], axis=0)
        acc_scr[hh] = jnp.exp2(m_prev - m_new) * acc_scr[hh] + _dot(vt1, p)
        m_scr[hh] = m_new

    def scores(kb, buf, masked):
        for hh in range(ATT_HB):
            scores_head(kb, buf, masked, hh)

    def update(kb, buf):
        for hh in range(ATT_HB):
            update_head(kb, buf, hh)

    def trip(t, ybuf, xbuf, masked):
        for hh in range(ATT_HB):
            scores_head(t, xbuf, masked, hh)
            update_head(t - 1, ybuf, hh)

    @pl.when(qi == 0)
    def _():
        scores(0, sa_scr, True)
        update(0, sa_scr)

    @pl.when(qi > 0)
    def _():
        scores(0, sa_scr, False)

        def pair(j, carry):
            trip(2 * j + 1, sa_scr, sb_scr, False)
            trip(2 * j + 2, sb_scr, sa_scr, False)
            return carry

        lax.fori_loop(0, lax.shift_right_logical(qi - 1, 1), pair, 0)

        @pl.when((qi & 1) == 1)
        def _():
            trip(qi, sa_scr, sb_scr, True)
            update(qi, sb_scr)

        @pl.when((qi & 1) == 0)
        def _():
            trip(qi - 1, sa_scr, sb_scr, False)
            trip(qi, sb_scr, sa_scr, True)
            update(qi, sa_scr)

    for hh in range(ATT_HB):
        acc = acc_scr[hh]
        o_t = acc[:V_HEAD_DIM] / acc[V_HEAD_DIM:V_HEAD_DIM + 1]
        o_ref[:, hh * V_HEAD_DIM:(hh + 1) * V_HEAD_DIM] = o_t.T.astype(o_ref.dtype)


def _prompt_attention(q, k, vt, batch, seq):
    assert ATT_TQ == ATT_TK
    nq = seq // ATT_TQ
    hb = ATT_HB
    return pl.pallas_call(
        _pattn_kernel,
        grid=(batch, N_HEADS // hb, nq),
        in_specs=[
            pl.BlockSpec((hb, ATT_TQ, QK_HEAD_DIM), lambda b, h, i: (h, b * nq + i, 0)),
            pl.BlockSpec((hb, seq, QK_HEAD_DIM), lambda b, h, i: (h, b, 0)),
            pl.BlockSpec((hb, V_HEAD_DIM, seq), lambda b, h, i: (h, 0, b)),
        ],
        out_specs=pl.BlockSpec((ATT_TQ, hb * V_HEAD_DIM), lambda b, h, i: (b * nq + i, h)),
        out_shape=jax.ShapeDtypeStruct((batch * seq, N_HEADS * V_HEAD_DIM), _BF16),
        scratch_shapes=[pltpu.VMEM((hb, ATT_TK, ATT_TQ), _F32),
                        pltpu.VMEM((hb, ATT_TK, ATT_TQ), _F32),
                        pltpu.VMEM((hb, 1, ATT_TQ), _F32),
                        pltpu.VMEM((hb, V_HEAD_DIM + ATT_ONES_ROWS, ATT_TQ), _F32)],
        compiler_params=pltpu.CompilerParams(
            dimension_semantics=("parallel", "parallel", "arbitrary"), vmem_limit_bytes=VMEM_LIMIT),
        name="prompt_attention",
    )(q, k, vt)


def _sattn_kernel(n_pages, pt_ref, qcat_ref, cnew_ref, krnew_ref, wukt_ref, lat_hbm, ropet_hbm, o_ref,
                  a_scr, lat_buf, ropet_buf, sca_scr, scb_scr, scl_scr, cbl_scr, sem_lat, sem_rope,
                  m_scr, l_scr, acc_scr):
    pages = SATT_PAGES_PER_CHUNK
    n_chunks = n_pages // pages
    tk = pages * PAGE_SIZE
    nk = N_HEADS * QK_NOPE_DIM
    nrow = N_HEADS * 8
    s = pl.program_id(0)
    n_seq = pl.num_programs(0) - 1
    slot = lax.rem(s, 2)

    def copies(seq, dst_slot):
        out = []
        for g in range(n_pages):
            page = 0 if seq is None else pt_ref[seq * n_pages + g]
            dst = pl.ds(g * PAGE_SIZE, PAGE_SIZE)
            out.append(pltpu.make_async_copy(lat_hbm.at[page], lat_buf.at[dst_slot, dst], sem_lat.at[dst_slot]))
            out.append(pltpu.make_async_copy(ropet_hbm.at[page], ropet_buf.at[dst_slot, :, dst],
                                             sem_rope.at[dst_slot]))
        return out

    @pl.when(s == 0)
    def _():
        a_scr[0:nk, :] = wukt_ref[...]
        scl_scr[...] = jnp.zeros_like(scl_scr)
        cbl_scr[...] = jnp.zeros_like(cbl_scr)
        m_scr[...] = jnp.zeros_like(m_scr)
        l_scr[...] = jnp.ones_like(l_scr)
        acc_scr[...] = jnp.zeros_like(acc_scr)
        for d in copies(0, 0):
            d.start()

    q2 = qcat_ref[...].reshape(nrow, KV_LORA_RANK + QK_ROPE_DIM)
    a_scr[nk:nk + nrow, :] = q2[:, :KV_LORA_RANK].astype(_BF16)
    q_rope = q2[:, KV_LORA_RANK:].astype(_BF16)

    @pl.when(s < n_seq)
    def _():
        for d in copies(jnp.minimum(s + 1, n_seq - 1), 1 - slot):
            d.start()

    for d in copies(None, slot):
        d.wait()

    def scores(c_b, s_rope):
        big = _dot_t(a_scr[...], c_b)
        rows = []
        for hd in range(N_HEADS):
            kx = big[hd * QK_NOPE_DIM:(hd + 1) * QK_NOPE_DIM, :]
            r = lax.rsqrt(jnp.sum(kx * kx, axis=0, keepdims=True) * (1.0 / QK_NOPE_DIM) + EPS)
            rows.append(big[nk + 8 * hd:nk + 8 * hd + 8, :] * r + s_rope[8 * hd:8 * hd + 8, :])
        return jnp.concatenate(rows, axis=0)

    sc_bufs = [sca_scr, scb_scr] * (n_chunks // 2 + 1)
    sc_bufs = sc_bufs[:n_chunks - 1] + [scl_scr]
    pad = PAGE_SIZE - cnew_ref.shape[0]

    def latent(c):
        return lat_buf[slot, pl.ds(c * tk, tk), :].astype(_BF16)

    def stage_scores(c):
        s_rope = _dot(q_rope, ropet_buf[slot, :, pl.ds(c * tk, tk)].astype(_BF16))
        if c < n_chunks - 1:
            sc_bufs[c][...] = scores(latent(c), s_rope)
        else:
            c_new = jnp.concatenate([cnew_ref[...], jnp.zeros((pad, KV_LORA_RANK), _F32)], axis=0)
            c_b = jnp.concatenate([latent(c), c_new.astype(_BF16)], axis=0)
            cbl_scr[...] = c_b
            kr_new = jnp.concatenate([krnew_ref[...], jnp.zeros((pad, QK_ROPE_DIM), _F32)], axis=0)
            s_rope = jnp.concatenate([s_rope, _dot_t(q_rope, kr_new.astype(_BF16))], axis=1)
            row = lax.broadcasted_iota(jnp.int32, (nrow, tk + PAGE_SIZE), 0)
            col = lax.broadcasted_iota(jnp.int32, (nrow, tk + PAGE_SIZE), 1)
            sc_bufs[c][...] = jnp.where(col - tk <= (row & 7), scores(c_b, s_rope), NEG_INF)

    def stage_update(c):
        _softmax_step(sc_bufs[c][...], latent(c), m_scr, l_scr, acc_scr)

    stage_scores(0)

    _softmax_step(scl_scr[...], cbl_scr[...], m_scr, l_scr, acc_scr)
    o_ref[...] = (acc_scr[...] / l_scr[...]).reshape(o_ref.shape)
    m_scr[...] = jnp.full_like(m_scr, NEG_INF)
    l_scr[...] = jnp.zeros_like(l_scr)
    acc_scr[...] = jnp.zeros_like(acc_scr)

    for c in range(1, n_chunks):
        stage_scores(c)
        stage_update(c - 1)


def _sample_attention(page_table, qcat, c_new, kr_new, wukt, cache_lat, cache_rope_t, t_new):
    n_seq, n_pages = page_table.shape
    assert t_new == 8 and n_pages % SATT_PAGES_PER_CHUNK == 0 and n_pages // SATT_PAGES_PER_CHUNK >= 2
    tk = SATT_PAGES_PER_CHUNK * PAGE_SIZE
    past = n_pages * PAGE_SIZE
    nrow = N_HEADS * t_new
    dq = KV_LORA_RANK + QK_ROPE_DIM
    cur = lambda s: jnp.minimum(s, n_seq - 1)
    prev = lambda s: jnp.maximum(s - 1, 0)
    grid_spec = pltpu.PrefetchScalarGridSpec(
        num_scalar_prefetch=1,
        grid=(n_seq + 1,),
        in_specs=[
            pl.BlockSpec((N_HEADS, t_new, dq), lambda s, pt: (0, cur(s), 0)),
            pl.BlockSpec((t_new, KV_LORA_RANK), lambda s, pt: (cur(s), 0)),
            pl.BlockSpec((t_new, QK_ROPE_DIM), lambda s, pt: (cur(s), 0)),
            pl.BlockSpec((N_HEADS * QK_NOPE_DIM, KV_LORA_RANK), lambda s, pt: (0, 0)),
            pl.BlockSpec(memory_space=pl.ANY),
            pl.BlockSpec(memory_space=pl.ANY),
        ],
        out_specs=pl.BlockSpec((N_HEADS, t_new, KV_LORA_RANK), lambda s, pt: (0, prev(s), 0)),
        scratch_shapes=[
            pltpu.VMEM((N_HEADS * QK_NOPE_DIM + nrow, KV_LORA_RANK), _BF16),
            pltpu.VMEM((2, past, KV_LORA_RANK), _F32),
            pltpu.VMEM((2, QK_ROPE_DIM, past), _F32),
            pltpu.VMEM((nrow, tk), _F32),
            pltpu.VMEM((nrow, tk), _F32),
            pltpu.VMEM((nrow, tk + PAGE_SIZE), _F32),
            pltpu.VMEM((tk + PAGE_SIZE, KV_LORA_RANK), _BF16),
            pltpu.SemaphoreType.DMA((2,)),
            pltpu.SemaphoreType.DMA((2,)),
            pltpu.VMEM((nrow, 1), _F32),
            pltpu.VMEM((nrow, 1), _F32),
            pltpu.VMEM((nrow, KV_LORA_RANK), _F32),
        ],
    )
    return pl.pallas_call(
        functools.partial(_sattn_kernel, n_pages),
        grid_spec=grid_spec,
        out_shape=jax.ShapeDtypeStruct((N_HEADS, n_seq * t_new, KV_LORA_RANK), _F32),
        compiler_params=pltpu.CompilerParams(
            dimension_semantics=("arbitrary",), vmem_limit_bytes=VMEM_LIMIT),
        name="sample_attention",
    )(page_table.reshape(-1), qcat, c_new, kr_new, wukt, cache_lat, cache_rope_t)


def _merge_kernel(from_latent, a_ref, gate_ref, gout_ref, x_ref, wuv_ref, woa_ref, wout_ref, o_ref):
    if from_latent:
        heads = [_dot(a_ref[hd].astype(_BF16), wuv_ref[:, hd * V_HEAD_DIM:(hd + 1) * V_HEAD_DIM])
                 for hd in range(N_HEADS)]
        attn = jnp.concatenate(heads, axis=-1).astype(_BF16)
    else:
        attn = a_ref[...]
    merged = gate_ref[...].astype(_F32) * _dot(attn, woa_ref[...]) + gout_ref[...]
    o_ref[...] = x_ref[...] + _dot(merged.astype(_BF16), wout_ref[...])


def _merge(from_latent, attn, gate, gout, x, wuv, woa, wout):
    n = x.shape[0]
    tm = min(MERGE_TM, n // 2)
    tok = pl.BlockSpec((tm, D_MODEL), lambda i: (i, 0))
    if from_latent:
        a_spec = pl.BlockSpec((N_HEADS, tm, KV_LORA_RANK), lambda i: (0, i, 0))
    else:
        a_spec = tok
    return pl.pallas_call(
        functools.partial(_merge_kernel, from_latent),
        grid=(n // tm,),
        in_specs=[a_spec, tok, tok, tok, _const_spec(wuv.shape), _const_spec(woa.shape),
                  _const_spec(wout.shape)],
        out_specs=tok,
        out_shape=jax.ShapeDtypeStruct((n, D_MODEL), _F32),
        compiler_params=pltpu.CompilerParams(
            dimension_semantics=("parallel",), vmem_limit_bytes=VMEM_LIMIT),
        name="merge_sample" if from_latent else "merge_prompt",
    )(attn, gate, gout, x, wuv, woa, wout)


def _swap_halves(a, axis):
    lo, hi = jnp.split(a, 2, axis=axis)
    return jnp.concatenate([hi, lo], axis=axis)


def _slot(a):
    return jnp.concatenate([a, jnp.zeros_like(a)], axis=-1)


def _rope_tables(pos):
    half = QK_ROPE_DIM // 2
    inv = ROPE_BASE ** (-np.arange(half, dtype=np.float64) / half)
    ang = np.asarray(pos, np.float64)[:, None] * inv[None, :]
    cos, sin = np.cos(ang), np.sin(ang)
    zero = np.zeros((ang.shape[0], QK_ROPE_DIM))
    return (jnp.asarray(np.concatenate([cos, cos, zero], axis=-1), _F32),
            jnp.asarray(np.concatenate([-sin, sin, zero], axis=-1), _F32))


def _prep_weights(w_in, w_uq, w_uk, w_uv, w_o_gmlp, mix_norm, q_lora_norm, kv_lora_norm, q_nope_norm,
                  q_rope_norm, k_nope_norm, k_rope_norm, gmlp_v_norm):
    w_uq = w_uq.astype(_BF16)
    off_kr = Q_LORA_RANK + KV_LORA_RANK
    w_in_t = w_in.T.astype(_BF16)
    k_r_t = w_in_t[off_kr:off_kr + QK_ROPE_DIM]
    zero = jnp.zeros_like(k_r_t)
    w_in_kr_t = jnp.concatenate([k_r_t, zero, _swap_halves(k_r_t, 0), zero], axis=0)
    assert w_in_kr_t.shape[0] == 2 * ROPE_SLOT and _D_IN_PADDED == _OFF_KR + 2 * ROPE_SLOT
    wq = w_uq.reshape(Q_LORA_RANK, N_HEADS, QK_HEAD_DIM)
    wq_nope = wq[:, :, :QK_NOPE_DIM].reshape(Q_LORA_RANK, -1)
    wq_rope = wq[:, :, QK_NOPE_DIM:]
    w_uq_p = jnp.concatenate([wq_nope, _slot(wq_rope).reshape(Q_LORA_RANK, -1),
                              _slot(_swap_halves(wq_rope, -1)).reshape(Q_LORA_RANK, -1)], axis=-1)
    row = lambda a: a.reshape(1, -1).astype(_F32)
    return dict(
        mix_norm=row(mix_norm), w_in_t=w_in_t, w_in_kr_t=w_in_kr_t, q_lora_norm=row(q_lora_norm),
        w_uq=w_uq_p.astype(_BF16), kv_lora_norm=row(kv_lora_norm), w_uk=w_uk.astype(_BF16),
        w_uv=w_uv.astype(_BF16), w_uv_t=w_uv.T.astype(_BF16), q_nope_norm=row(q_nope_norm), q_rope_norm=row(_slot(q_rope_norm)),
        q_rope_norm_s=row(_slot(_swap_halves(q_rope_norm, -1))), k_nope_norm=row(k_nope_norm),
        k_rope_norm=row(_slot(k_rope_norm)), k_rope_norm_s=row(_slot(_swap_halves(k_rope_norm, -1))),
        gmlp_v_norm=row(gmlp_v_norm), w_o_gmlp=w_o_gmlp.astype(_BF16))


def kernel(x_prompt, x_sample, cache_kv_latent, cache_k_rope, page_table, ffn1_norm, ffn1_w_gate, ffn1_w_up, ffn1_w_down, mix_norm, w_in, q_lora_norm, w_uq, kv_lora_norm, w_uk, w_uv, q_nope_norm, q_rope_norm, k_nope_norm, k_rope_norm, gmlp_v_norm, gmlp_w_s, gmlp_b_s, w_o_attn, w_o_gmlp, w_out, ffn2_norm, ffn2_w_gate, ffn2_w_up, ffn2_w_down):
    batch, seq, _ = x_prompt.shape
    n_seq, t_new, _ = x_sample.shape
    assert ffn1_norm.shape[0] == 1
    past = page_table.shape[1] * PAGE_SIZE
    n_pool = cache_kv_latent.shape[1]
    l = 0
    row = lambda a: a.reshape(1, -1).astype(_F32)
    w = _prep_weights(w_in[l], w_uq[l], w_uk[l], w_uv[l], w_o_gmlp[l], mix_norm[l], q_lora_norm[l],
                      kv_lora_norm[l], q_nope_norm[l], q_rope_norm[l], k_nope_norm[l], k_rope_norm[l],
                      gmlp_v_norm[l])
    ffn1 = (row(ffn1_norm[l]), ffn1_w_gate.reshape(D_MODEL, D_FF), ffn1_w_up.reshape(D_MODEL, D_FF),
            ffn1_w_down.reshape(D_FF, D_MODEL))
    ffn2 = (row(ffn2_norm[l]), ffn2_w_gate.reshape(D_MODEL, D_FF), ffn2_w_up.reshape(D_MODEL, D_FF),
            ffn2_w_down.reshape(D_FF, D_MODEL))
    woa = w_o_attn[l].astype(_BF16)
    wout = w_out[l].astype(_BF16)

    wp = dict(w, gmlp_w_s=gmlp_w_s[l].astype(_F32), gmlp_b_s=gmlp_b_s[l].reshape(GMLP_GROUPS, CHUNK, 1))
    cos_p, sin_p = _rope_tables(np.arange(seq))
    xp = _ffn(x_prompt.reshape(batch * seq, D_MODEL), *ffn1)
    q, k, v, ckv_p, kr_p, gate_p, gout_p = _proj(False, xp, wp, cos_p, sin_p, seq // PROJ_TM)
    attn_p = _prompt_attention(q, k, v, batch, seq)
    xp = _merge(False, attn_p, gate_p, gout_p, xp, w["w_uv"], woa, wout)
    xp = _ffn(xp, *ffn2)

    reps = CHUNK // t_new
    ws_s = jax.vmap(lambda m: jnp.kron(jnp.eye(reps, dtype=_F32), m))(gmlp_w_s[l][:, :t_new, :t_new])
    bs_s = jnp.tile(gmlp_b_s[l][:, :t_new], (1, reps)).reshape(GMLP_GROUPS, CHUNK, 1)
    ws = dict(w, gmlp_w_s=ws_s, gmlp_b_s=bs_s)
    cos_s, sin_s = _rope_tables(past + np.arange(PROJ_TM) % t_new)
    xs = _ffn(x_sample.reshape(n_seq * t_new, D_MODEL), *ffn1)
    qcat, ckv_s, kr_s, vn_s, gate_s, gout_s = _proj(True, xs, ws, cos_s, sin_s, 1)
    o_lat = _sample_attention(page_table, qcat, ckv_s, kr_s, w["w_uk"].T,
                              cache_kv_latent.reshape(n_pool, PAGE_SIZE, KV_LORA_RANK),
                              jnp.swapaxes(cache_k_rope.reshape(n_pool, PAGE_SIZE, QK_ROPE_DIM), 1, 2),
                              t_new)
    xs = _merge(True, o_lat, gate_s, gout_s, xs, w["w_uv"], woa, wout)
    xs = _ffn(xs, *ffn2)

    return (xp.reshape(batch, seq, D_MODEL), xs.reshape(n_seq, t_new, D_MODEL),
            ckv_p.reshape(1, batch, seq, KV_LORA_RANK), jnp.swapaxes(kr_p, 1, 2)[None],
            ckv_s.reshape(1, n_seq, t_new, KV_LORA_RANK), kr_s.reshape(1, n_seq, t_new, QK_ROPE_DIM),
            vn_s.reshape(1, n_seq, t_new, GMLP_WIDTH))
```
